```python
import math
import jax, jax.numpy as jnp
from jax import lax
import numpy as np

D_MODEL = 1024
BATCH = 8
SEQ = 4096
DEPTH = 1

MIX_WIDTH = D_MODEL
ATT_HEAD_DIM = 64
ATT_HEADS = (MIX_WIDTH // 2) // ATT_HEAD_DIM
ATT_KV_HEADS = max(1, ATT_HEADS // 4)
ATT_GROUP = ATT_HEADS // ATT_KV_HEADS
ATT_SCALE = ATT_HEAD_DIM ** -0.5
WINDOW = 128
BLOCK = WINDOW
N_BUCKETS = 32
MAX_DISTANCE = 128
HG_DK = 128
HG_DV = 128
HG_HEADS = (MIX_WIDTH // 2) // HG_DV
CHUNK = 64
N_EXPERTS = 32
TOP_K = 4
D_FF = D_MODEL
SWIGLU_LIMIT = 7.0
SWIGLU_ALPHA = 1.702
EPS = 1e-5

ATT_Q = ATT_HEADS * ATT_HEAD_DIM
ATT_KV = ATT_KV_HEADS * ATT_HEAD_DIM
HG_K = HG_HEADS * HG_DK
HG_V = HG_HEADS * HG_DV
OUT_IN = ATT_Q + HG_V
IN_SIZES = (ATT_Q, ATT_KV, ATT_KV, HG_K, HG_K, HG_V, HG_V)
IN_COLS = sum(IN_SIZES)
IN_SPLITS = tuple(sum(IN_SIZES[:j + 1]) for j in range(len(IN_SIZES) - 1))

kernel_name = 'hymba_swa_sink_hgrn2_moe_adaln'


def rms_norm(x, g):
    xf = x.astype(jnp.float32)
    y = xf * lax.rsqrt(jnp.mean(xf * xf, axis=-1, keepdims=True) + EPS)
    return (y * g.astype(jnp.float32)).astype(x.dtype)


def t5_causal_bucket(dist):
    n = jnp.maximum(dist, 0)
    max_exact = N_BUCKETS // 2
    nf = jnp.maximum(n, 1).astype(jnp.float32)
    large = max_exact + (jnp.log(nf / max_exact) / math.log(MAX_DISTANCE / max_exact)
                         * (N_BUCKETS - max_exact)).astype(jnp.int32)
    large = jnp.minimum(large, N_BUCKETS - 1)
    return jnp.where(n < max_exact, n, large)


def band_structure(rel_bias, n_blocks):
    i = jnp.arange(BLOCK, dtype=jnp.int32)[:, None]
    m = jnp.arange(2 * BLOCK, dtype=jnp.int32)[None, :]
    dist = i + BLOCK - m
    in_window = (dist >= 0) & (dist < WINDOW)
    bias = rel_bias.astype(jnp.float32)[t5_causal_bucket(dist)]
    bias = jnp.transpose(bias, (2, 0, 1)).reshape(ATT_KV_HEADS, ATT_GROUP, BLOCK, 2 * BLOCK)
    blk = jnp.arange(n_blocks, dtype=jnp.int32)[:, None, None]
    key_exists = (blk > 0) | (m[None] >= BLOCK)
    valid = in_window[None] & key_exists
    return bias, valid


def sliding_window_attention(q, k, v, sinks, band_bias, valid):
    B, S = q.shape[0], q.shape[1]
    nb = S // BLOCK
    qb = q.reshape(B, nb, BLOCK, ATT_KV_HEADS, ATT_GROUP, ATT_HEAD_DIM)

    def band(t):
        prev = jnp.pad(t, ((0, 0), (BLOCK, 0), (0, 0), (0, 0)))[:, :S]
        prev = prev.reshape(B, nb, BLOCK, ATT_KV_HEADS, ATT_HEAD_DIM)
        cur = t.reshape(B, nb, BLOCK, ATT_KV_HEADS, ATT_HEAD_DIM)
        return jnp.concatenate([prev, cur], axis=2)

    kb, vb = band(k), band(v)
    s = jnp.einsum('bnqkgd,bnskd->bnkgqs', qb, kb).astype(jnp.float32) * ATT_SCALE + band_bias
    s = jnp.where(valid[None, :, None, None], s, -jnp.inf)
    sink = jnp.broadcast_to(sinks.astype(jnp.float32).reshape(ATT_KV_HEADS, ATT_GROUP, 1, 1),
                            s.shape[:-1] + (1,))
    p = jax.nn.softmax(jnp.concatenate([s, sink], axis=-1), axis=-1)[..., :-1]
    o = jnp.einsum('bnkgqs,bnskd->bnqkgd', p.astype(v.dtype), vb)
    return o.reshape(B, S, ATT_Q)


def hgrn2_chunkwise(q, f_logit, inp, lb):
    B, S, H, DK = q.shape
    DV = inp.shape[-1]
    nc = S // CHUNK
    f = lb + (1.0 - lb) * jax.nn.sigmoid(f_logit.astype(jnp.float32))
    log_f = jnp.log(f)
    kk = 1.0 - f
    qf = jax.nn.silu(q.astype(jnp.float32))

    def to_chunks(t):
        return t.reshape(B, nc, CHUNK, H, t.shape[-1]).transpose(1, 0, 3, 2, 4)

    causal = jnp.tril(jnp.ones((CHUNK, CHUNK), dtype=bool))[:, :, None]

    def step(state, xs):
        qc, kc, vc, gc = xs
        b = jnp.cumsum(gc, axis=2)
        b_end = b[:, :, -1:, :]
        decay = jnp.exp(jnp.where(causal, b[:, :, :, None, :] - b[:, :, None, :, :], -jnp.inf))
        scores = jnp.einsum('bhtk,bhsk,bhtsk->bhts', qc, kc, decay)
        o = (jnp.einsum('bhts,bhsv->bhtv', scores, vc)
             + jnp.einsum('bhtk,bhkv->bhtv', qc * jnp.exp(b), state))
        state = (jnp.exp(b_end[:, :, 0, :, None]) * state
                 + jnp.einsum('bhsk,bhsv->bhkv', kc * jnp.exp(b_end - b), vc))
        return state, o

    state0 = jnp.zeros((B, H, DK, DV), jnp.float32)
    _, o = lax.scan(step, state0, (to_chunks(qf), to_chunks(kk),
                                   to_chunks(inp.astype(jnp.float32)), to_chunks(log_f)))
    return o.transpose(1, 0, 3, 2, 4).reshape(B, S, H, DV).astype(inp.dtype)


def moe_ffn(h, w_router, b_router, w_gate_up, b_gate_up, w_down, b_down):
    B, S, D = h.shape
    t = h.reshape(B * S, D)
    logits = (t @ w_router + b_router).astype(jnp.float32)
    top_val, top_idx = lax.top_k(logits, TOP_K)
    top_w = jax.nn.softmax(top_val, axis=-1)
    combine = jnp.sum(jax.nn.one_hot(top_idx, N_EXPERTS, dtype=jnp.float32) * top_w[..., None], axis=1)
    out = jnp.zeros((B * S, D), jnp.float32)
    for e in range(N_EXPERTS):
        gu = t @ w_gate_up[e] + b_gate_up[e]
        gate = jnp.minimum(gu[:, 0::2], SWIGLU_LIMIT)
        up = jnp.clip(gu[:, 1::2], -SWIGLU_LIMIT, SWIGLU_LIMIT)
        y = ((up + 1.0) * gate * jax.nn.sigmoid(SWIGLU_ALPHA * gate)) @ w_down[e] + b_down[e]
        out = out + combine[:, e:e + 1] * y.astype(jnp.float32)
    return out.astype(h.dtype).reshape(B, S, D)


def setup_inputs(seed: int = 0) -> dict:
    key = jax.random.key(seed)
    ks = jax.random.split(key, 24)
    f32 = jnp.float32
    L = DEPTH

    def nrm(k, shape, scale):
        return scale * jax.random.normal(k, shape, f32)

    return {
        'x': nrm(ks[0], (BATCH, SEQ, D_MODEL), 1.0),
        'c': nrm(ks[1], (BATCH, D_MODEL), 1.0),
        'w_ada': nrm(ks[2], (L, D_MODEL, 6 * D_MODEL), 0.5 * D_MODEL ** -0.5),
        'b_ada': nrm(ks[3], (L, 6 * D_MODEL), 0.02),
        'g_mix': 1.0 + nrm(ks[4], (L, D_MODEL), 0.02),
        'w_in': nrm(ks[5], (L, D_MODEL, IN_COLS), D_MODEL ** -0.5),
        'b_in': nrm(ks[6], (L, IN_COLS), 0.02),
        'attn_sinks': nrm(ks[7], (L, ATT_HEADS), 0.5),
        'rel_bias': nrm(ks[8], (N_BUCKETS, ATT_HEADS), 0.5),
        'hg_lb': nrm(ks[9], (L + 1, HG_K), 1.0),
        'hg_norm_w': 1.0 + nrm(ks[10], (L, HG_DV), 0.02),
        'w_out': nrm(ks[11], (L, OUT_IN, D_MODEL), OUT_IN ** -0.5),
        'b_out': nrm(ks[12], (L, D_MODEL), 0.02),
        'g_ffn': 1.0 + nrm(ks[13], (L, D_MODEL), 0.02),
        'w_router': nrm(ks[14], (L, D_MODEL, N_EXPERTS), D_MODEL ** -0.5),
        'b_router': nrm(ks[15], (L, N_EXPERTS), 0.01),
        'w_gate_up': nrm(ks[16], (L, N_EXPERTS, D_MODEL, 2 * D_FF), D_MODEL ** -0.5),
        'b_gate_up': nrm(ks[17], (L, N_EXPERTS, 2 * D_FF), 0.02),
        'w_down': nrm(ks[18], (L, N_EXPERTS, D_FF, D_MODEL), D_FF ** -0.5),
        'b_down': nrm(ks[19], (L, N_EXPERTS, D_MODEL), 0.02),
        'g_final': 1.0 + nrm(ks[20], (D_MODEL,), 0.02),
    }


def reference(x, c, w_ada, b_ada, g_mix, w_in, b_in, attn_sinks, rel_bias, hg_lb, hg_norm_w,
              w_out, b_out, g_ffn, w_router, b_router, w_gate_up, b_gate_up, w_down, b_down,
              g_final):
    B, S, D = x.shape
    band_bias, valid = band_structure(rel_bias, S // BLOCK)
    lb_all = jnp.cumsum(jax.nn.softmax(hg_lb.astype(jnp.float32), axis=0), axis=0)
    cond = jax.nn.silu(c)
    for l in range(DEPTH):
        mod = (cond @ w_ada[l] + b_ada[l]).reshape(B, 6, 1, D)
        sh1, sc1, gt1, sh2, sc2, gt2 = (mod[:, j] for j in range(6))

        h = rms_norm(x, g_mix[l]) * (1.0 + sc1) + sh1
        proj = h @ w_in[l] + b_in[l]
        aq, ak, av, hq, hf, hi, hg = jnp.split(proj, IN_SPLITS, axis=-1)
        y_att = sliding_window_attention(
            aq.reshape(B, S, ATT_HEADS, ATT_HEAD_DIM),
            ak.reshape(B, S, ATT_KV_HEADS, ATT_HEAD_DIM),
            av.reshape(B, S, ATT_KV_HEADS, ATT_HEAD_DIM),
            attn_sinks[l], band_bias, valid)
        o_hg = hgrn2_chunkwise(
            hq.reshape(B, S, HG_HEADS, HG_DK),
            hf.reshape(B, S, HG_HEADS, HG_DK),
            hi.reshape(B, S, HG_HEADS, HG_DV),
            lb_all[l].reshape(HG_HEADS, HG_DK))
        o_hg = rms_norm(o_hg, hg_norm_w[l]) * jax.nn.silu(hg.reshape(B, S, HG_HEADS, HG_DV))
        mixed = jnp.concatenate([y_att, o_hg.reshape(B, S, HG_V)], axis=-1)
        x = x + gt1 * (mixed @ w_out[l] + b_out[l])

        h = rms_norm(x, g_ffn[l]) * (1.0 + sc2) + sh2
        x = x + gt2 * moe_ffn(h, w_router[l], b_router[l], w_gate_up[l], b_gate_up[l],
                              w_down[l], b_down[l])
    return rms_norm(x, g_final)
```

```python
import functools
import math

import numpy as np
import jax
import jax.numpy as jnp
from jax import lax
from jax.experimental import pallas as pl
from jax.experimental.pallas import tpu as pltpu

D_MODEL = 1024
ATT_HEAD_DIM = 64
ATT_HEADS = 8
ATT_KV_HEADS = 2
ATT_SCALE = ATT_HEAD_DIM ** -0.5
BLOCK = 128
N_BUCKETS = 32
MAX_DISTANCE = 128
HG_HEADS = 4
HG_D = 128
CHUNK = 64
N_EXPERTS = 32
TOP_K = 4
D_FF = 1024
SWIGLU_LIMIT = 7.0
SWIGLU_ALPHA = 1.702
EPS = 1e-5

ATT_Q = ATT_HEADS * ATT_HEAD_DIM
HG_W = HG_HEADS * HG_D
COL_Q = 0
COL_KV = ATT_Q
COL_HQ = COL_KV + 4 * 128
COL_HF = COL_HQ + HG_W
COL_HI = COL_HF + HG_W
COL_HG = COL_HI + HG_W
PROJ_COLS = COL_HG + HG_W

LANES = 128
SUBLANES = 8
TS = 256
VMEM_LIMIT = 56 * 1024 * 1024
HG_SAFE_LEVELS = (32, 16, 8)
HG_DIAG = 8
HG_EXP_CAP = 80.0

F32 = jnp.float32
BF16 = jnp.bfloat16
NEG_INF = float("-inf")


def _dot(a, b):
    return jnp.dot(a, b, preferred_element_type=F32)


def _dot_nt(a, b):
    return lax.dot_general(a, b, (((1,), (1,)), ((), ())), preferred_element_type=F32)


def _dot_tn(a, b):
    return lax.dot_general(a, b, (((0,), (0,)), ((), ())), preferred_element_type=F32)


def _sigmoid(x):
    return 1.0 / (1.0 + jnp.exp(-x))


def _prep_kernel(c_ref, w_ref, b_ref, o_ref):
    c = c_ref[...]
    cond = c * _sigmoid(c)
    o_ref[0] = jnp.dot(cond, w_ref[...], precision=lax.Precision.HIGHEST,
                       preferred_element_type=F32) + b_ref[...]


def _prep_call(c, w_ada, b_ada):
    B = c.shape[0]
    return pl.pallas_call(
        _prep_kernel,
        grid=(6,),
        in_specs=[
            pl.BlockSpec((B, D_MODEL), lambda j: (0, 0)),
            pl.BlockSpec((D_MODEL, D_MODEL), lambda j: (0, j)),
            pl.BlockSpec((1, D_MODEL), lambda j: (0, j)),
        ],
        out_specs=pl.BlockSpec((1, B, D_MODEL), lambda j: (j, 0, 0)),
        out_shape=jax.ShapeDtypeStruct((6, B, D_MODEL), F32),
        compiler_params=pltpu.CompilerParams(dimension_semantics=("arbitrary",)),
        name="adaln_mod",
    )(c, w_ada, b_ada)


def _tables_kernel(rb_ref, lbp_ref, bucket_ref, lb_ref, bias_ref):
    p = lbp_ref[...]
    e = jnp.exp(p - jnp.max(p, axis=0, keepdims=True))
    sm = e / jnp.sum(e, axis=0, keepdims=True)
    lb_ref[...] = sm[0:1]
    bucket = bucket_ref[...]
    for h in range(ATT_HEADS):
        acc = jnp.full(bucket.shape, NEG_INF, F32)
        for bk in range(N_BUCKETS):
            acc = jnp.where(bucket == bk, rb_ref[bk, h], acc)
        bias_ref[h] = acc


def _tables_call(rel_bias, hg_lb, bucket):
    return pl.pallas_call(
        _tables_kernel,
        in_specs=[
            pl.BlockSpec(memory_space=pltpu.SMEM),
            pl.BlockSpec(memory_space=pltpu.VMEM),
            pl.BlockSpec(memory_space=pltpu.VMEM),
        ],
        out_specs=[pl.BlockSpec(memory_space=pltpu.VMEM), pl.BlockSpec(memory_space=pltpu.VMEM)],
        out_shape=[jax.ShapeDtypeStruct((1, HG_W), F32),
                   jax.ShapeDtypeStruct((ATT_HEADS, BLOCK, 2 * BLOCK), F32)],
        name="tables",
    )(rel_bias, hg_lb, bucket)


def _bucket_table():
    i = jnp.arange(BLOCK, dtype=jnp.int32)[:, None]
    m = jnp.arange(2 * BLOCK, dtype=jnp.int32)[None, :]
    dist = i + BLOCK - m
    n = jnp.maximum(dist, 0)
    max_exact = N_BUCKETS // 2
    nf = jnp.maximum(n, 1).astype(F32)
    large = max_exact + (jnp.log(nf / max_exact) / math.log(MAX_DISTANCE / max_exact)
                         * (N_BUCKETS - max_exact)).astype(jnp.int32)
    large = jnp.minimum(large, N_BUCKETS - 1)
    bucket = jnp.where(n < max_exact, n, large)
    return jnp.where((dist >= 0) & (dist < BLOCK), bucket, -1).astype(jnp.int32)


def _rms(x):
    return x * lax.rsqrt(jnp.mean(x * x, axis=-1, keepdims=True) + EPS)


def _attention_block(proj_ref, kvprev_ref, bias_ref, sink_ref, mixed_ref, blk, first_tile):
    r0 = blk * BLOCK
    cur = proj_ref[r0:r0 + BLOCK, COL_KV:COL_KV + 512]
    if blk == 0:
        prev = kvprev_ref[...]
    else:
        prev = proj_ref[r0 - BLOCK:r0, COL_KV:COL_KV + 512]
    band = jnp.concatenate([prev, cur], axis=0)
    lane = lax.broadcasted_iota(jnp.int32, (2 * BLOCK, LANES), 1)
    lo = lane < ATT_HEAD_DIM
    col = lax.broadcasted_iota(jnp.int32, (1, 2 * BLOCK), 1)
    if blk == 0:
        pen = jnp.where((col < BLOCK) & first_tile, NEG_INF, 0.0)
    for g in range(ATT_KV_HEADS):
        kd = band[:, g * 128:(g + 1) * 128]
        vd = band[:, 256 + g * 128:256 + (g + 1) * 128]
        kb = jnp.concatenate([jnp.where(lo, kd, 0.0), jnp.where(lo, 0.0, kd)], axis=0).astype(BF16)
        vb = jnp.concatenate([jnp.where(lo, vd, 0.0), jnp.where(lo, 0.0, vd)], axis=0).astype(BF16)
        for pp in range(2):
            pair = g * 2 + pp
            qp = (proj_ref[r0:r0 + BLOCK, pair * 128:(pair + 1) * 128] * ATT_SCALE).astype(BF16)
            s = _dot_nt(qp, kb)
            ps = []
            inv = []
            for hh in range(2):
                h = pair * 2 + hh
                sh = s[:, hh * 256:(hh + 1) * 256] + bias_ref[h]
                if blk == 0:
                    sh = sh + pen
                sink = sink_ref[h]
                mx = jnp.maximum(jnp.max(sh, axis=-1, keepdims=True), sink)
                p = jnp.exp(sh - mx)
                den = jnp.sum(p, axis=-1, keepdims=True) + jnp.exp(sink - mx)
                ps.append(p.astype(BF16))
                inv.append(1.0 / den)
            o = _dot(jnp.concatenate(ps, axis=1), vb)
            lane_o = lax.broadcasted_iota(jnp.int32, (BLOCK, LANES), 1)
            o = o * jnp.where(lane_o < ATT_HEAD_DIM, inv[0], inv[1])
            mixed_ref[r0:r0 + BLOCK, pair * 128:(pair + 1) * 128] = o.astype(BF16)


def _hgrn_head(proj_ref, state_ref, b_all, kk, hgw_ref, mixed_ref, hd, masks):
    c0 = hd * HG_D
    kk_all = kk[:, c0:c0 + HG_D]
    qr = proj_ref[:, COL_HQ + c0:COL_HQ + c0 + HG_D]
    qf_all = qr * _sigmoid(qr)
    st = state_ref[hd]
    for ck in range(TS // CHUNK):
        r0 = ck * CHUNK
        b = b_all[r0:r0 + CHUNK, c0:c0 + HG_D]
        q = qf_all[r0:r0 + CHUNK]
        k = kk_all[r0:r0 + CHUNK]
        v = proj_ref[r0:r0 + CHUNK, COL_HI + c0:COL_HI + c0 + HG_D].astype(BF16)
        scores = jnp.zeros((CHUNK, CHUNK), F32)
        for m in HG_SAFE_LEVELS:
            qs, ks = [], []
            for bi in range(CHUNK // m):
                rows = slice(bi * m, (bi + 1) * m)
                if bi % 2 == 1:
                    ref = b[bi * m - 1:bi * m]
                    qs.append(q[rows] * jnp.exp(b[rows] - ref))
                    ks.append(jnp.zeros((m, HG_D), F32))
                else:
                    ref = b[(bi + 1) * m - 1:(bi + 1) * m]
                    qs.append(jnp.zeros((m, HG_D), F32))
                    ks.append(k[rows] * jnp.exp(ref - b[rows]))
            ql = jnp.concatenate(qs, axis=0).astype(BF16)
            kl = jnp.concatenate(ks, axis=0).astype(BF16)
            scores = scores + jnp.where(masks[m], _dot_nt(ql, kl), 0.0)
        qs, ks = [], []
        for bi in range(CHUNK // HG_DIAG):
            rows = slice(bi * HG_DIAG, (bi + 1) * HG_DIAG)
            c = b[rows] if bi == 0 else b[rows] - b[bi * HG_DIAG - 1:bi * HG_DIAG]
            qs.append(q[rows] * jnp.exp(c))
            ks.append(k[rows] * jnp.exp(jnp.minimum(-c, HG_EXP_CAP)))
        qd = jnp.concatenate(qs, axis=0).astype(BF16)
        kd = jnp.concatenate(ks, axis=0).astype(BF16)
        scores = scores + jnp.where(masks[0], _dot_nt(qd, kd), 0.0)
        o = _dot(scores.astype(BF16), v) + _dot_nt((q * jnp.exp(b)).astype(BF16), st.astype(BF16))
        bend = b[CHUNK - 1:CHUNK]
        kdec = (k * jnp.exp(bend - b)).astype(BF16)
        st = st * jnp.exp(bend) + _dot_tn(v, kdec)
        gate = proj_ref[r0:r0 + CHUNK, COL_HG + c0:COL_HG + c0 + HG_D]
        on = _rms(o) * hgw_ref[...] * (gate * _sigmoid(gate))
        mixed_ref[r0:r0 + CHUNK, ATT_Q + c0:ATT_Q + c0 + HG_D] = on.astype(BF16)
    state_ref[hd] = st


def _hgrn_masks():
    r = lax.broadcasted_iota(jnp.int32, (CHUNK, CHUNK), 0)
    c = lax.broadcasted_iota(jnp.int32, (CHUNK, CHUNK), 1)
    masks = {}
    for m in HG_SAFE_LEVELS:
        masks[m] = ((r // (2 * m)) == (c // (2 * m))) & (((r // m) % 2) == 1) & (((c // m) % 2) == 0)
    masks[0] = ((r // HG_DIAG) == (c // HG_DIAG)) & (c <= r)
    return masks


def _mix_kernel(sink_ref, x_ref, mod_ref, gmix_ref, win_ref, bin_ref, bias_ref, lb_ref, hgw_ref,
                wout_ref, bout_ref, gffn_ref, wr_ref, br_ref, ltri_ref, utri_ref,
                x1_ref, h2_ref, idx_ref, wts_ref, rank_ref, cnt_ref,
                proj_ref, mixed_ref, kvprev_ref, state_ref, run_ref):
    bi = pl.program_id(0)
    j = pl.program_id(1)

    @pl.when(j == 0)
    def _():
        kvprev_ref[...] = jnp.zeros_like(kvprev_ref)
        state_ref[...] = jnp.zeros_like(state_ref)

    @pl.when((j == 0) & (bi == 0))
    def _():
        run_ref[...] = jnp.zeros_like(run_ref)

    x = x_ref[0]
    mod = mod_ref[0]
    h = _rms(x) * (gmix_ref[...] * (1.0 + mod[1:2])) + mod[0:1]
    proj_ref[...] = _dot(h.astype(BF16), win_ref[...]) + bin_ref[...]

    for blk in range(TS // BLOCK):
        _attention_block(proj_ref, kvprev_ref, bias_ref, sink_ref, mixed_ref, blk, j == 0)
    kvprev_ref[...] = proj_ref[TS - BLOCK:TS, COL_KV:COL_KV + 512]

    lbv = lb_ref[...]
    fl = lbv + (1.0 - lbv) * _sigmoid(proj_ref[:, COL_HF:COL_HF + HG_W])
    g = jnp.log(fl)
    g_hi = g.astype(BF16)
    r1 = g - g_hi.astype(F32)
    g_mid = r1.astype(BF16)
    g_lo = (r1 - g_mid.astype(F32)).astype(BF16)
    ltri = ltri_ref[...]
    b_all = _dot(ltri, g_hi) + _dot(ltri, g_mid) + _dot(ltri, g_lo)

    masks = _hgrn_masks()
    for hd in range(HG_HEADS):
        _hgrn_head(proj_ref, state_ref, b_all, 1.0 - fl, hgw_ref, mixed_ref, hd, masks)

    y = _dot(mixed_ref[...], wout_ref[...]) + bout_ref[...]
    x1 = x + mod[2:3] * y
    x1_ref[0] = x1
    h2 = _rms(x1) * (gffn_ref[...] * (1.0 + mod[4:5])) + mod[3:4]
    for s in range(SUBLANES):
        h2_ref[pl.ds(s, TS, stride=SUBLANES), :] = h2[:, s * LANES:(s + 1) * LANES]

    h_hi = h2.astype(BF16)
    h_lo = (h2 - h_hi.astype(F32)).astype(BF16)
    wr = wr_ref[...]
    t1 = _dot_nt(wr, h_hi)
    t2 = _dot_nt(wr[0:N_EXPERTS], h_lo)
    logits = t1[0:N_EXPERTS] + t1[N_EXPERTS:2 * N_EXPERTS] + t2 + br_ref[:, 0:1]

    eidx = lax.broadcasted_iota(jnp.int32, (N_EXPERTS, TS), 0)
    l = logits
    vals, ohs, ids = [], [], []
    for _ in range(TOP_K):
        mx = jnp.max(l, axis=0, keepdims=True)
        ik = jnp.min(jnp.where(l == mx, eidx, N_EXPERTS), axis=0, keepdims=True)
        oh = eidx == ik
        vals.append(mx)
        ids.append(ik)
        ohs.append(oh)
        l = jnp.where(oh, NEG_INF, l)
    es = [jnp.exp(v - vals[0]) for v in vals]
    den = es[0] + es[1] + es[2] + es[3]
    run = run_ref[...]
    base = run[:, 0:1]
    utri = utri_ref[...]
    ranks = []
    for kq in range(TOP_K):
        ohf = jnp.where(ohs[kq], 1.0, 0.0)
        pref = _dot(ohf.astype(BF16), utri)
        ranks.append(jnp.sum(jnp.where(ohs[kq], base + pref, 0.0), axis=0, keepdims=True))
        base = base + jnp.sum(ohf, axis=1, keepdims=True)
    run_new = jnp.broadcast_to(base, run.shape)
    run_ref[...] = run_new
    cnt_ref[...] = run_new.astype(jnp.int32)
    idx_ref[...] = jnp.concatenate(ids, axis=0)
    wts_ref[...] = jnp.concatenate([e / den for e in es], axis=0)
    rank_ref[...] = jnp.concatenate(ranks, axis=0).astype(jnp.int32)


def _mix_call(sinks, x, mod, gmix, win, bin_, bias, lb, hgw, wout, bout, gffn, wr, br, ltri, utri):
    B, S, D = x.shape
    N = B * S
    nj = S // TS
    const2 = lambda b, j: (0, 0)
    tok = lambda b, j: (0, b * nj + j)
    in_specs = [
        pl.BlockSpec(memory_space=pltpu.SMEM),
        pl.BlockSpec((1, TS, D), lambda b, j: (b, j, 0)),
        pl.BlockSpec((1, 8, D), lambda b, j: (b, 0, 0)),
        pl.BlockSpec((1, D), const2),
        pl.BlockSpec((D, PROJ_COLS), const2),
        pl.BlockSpec((1, PROJ_COLS), const2),
        pl.BlockSpec((ATT_HEADS, BLOCK, 2 * BLOCK), lambda b, j: (0, 0, 0)),
        pl.BlockSpec((1, HG_W), const2),
        pl.BlockSpec((1, HG_D), const2),
        pl.BlockSpec((D, D), const2),
        pl.BlockSpec((1, D), const2),
        pl.BlockSpec((1, D), const2),
        pl.BlockSpec((2 * N_EXPERTS, D), const2),
        pl.BlockSpec((N_EXPERTS, LANES), const2),
        pl.BlockSpec((TS, TS), const2),
        pl.BlockSpec((TS, TS), const2),
    ]
    out_specs = [
        pl.BlockSpec((1, TS, D), lambda b, j: (b, j, 0)),
        pl.BlockSpec((TS * SUBLANES, LANES), lambda b, j: (b * nj + j, 0)),
        pl.BlockSpec((TOP_K, TS), tok),
        pl.BlockSpec((TOP_K, TS), tok),
        pl.BlockSpec((TOP_K, TS), tok),
        pl.BlockSpec((N_EXPERTS, LANES), const2),
    ]
    out_shape = [
        jax.ShapeDtypeStruct((B, S, D), F32),
        jax.ShapeDtypeStruct((N * SUBLANES, LANES), F32),
        jax.ShapeDtypeStruct((TOP_K, N), jnp.int32),
        jax.ShapeDtypeStruct((TOP_K, N), F32),
        jax.ShapeDtypeStruct((TOP_K, N), jnp.int32),
        jax.ShapeDtypeStruct((N_EXPERTS, LANES), jnp.int32),
    ]
    scratch = [
        pltpu.VMEM((TS, PROJ_COLS), F32),
        pltpu.VMEM((TS, D), BF16),
        pltpu.VMEM((BLOCK, 512), F32),
        pltpu.VMEM((HG_HEADS, HG_D, HG_D), F32),
        pltpu.VMEM((N_EXPERTS, LANES), F32),
    ]
    return pl.pallas_call(
        _mix_kernel,
        grid=(B, nj),
        in_specs=in_specs,
        out_specs=out_specs,
        out_shape=out_shape,
        scratch_shapes=scratch,
        compiler_params=pltpu.CompilerParams(
            dimension_semantics=("arbitrary", "arbitrary"), vmem_limit_bytes=VMEM_LIMIT),
        name="mix_router",
    )(sinks, x, mod, gmix, win, bin_, bias, lb, hgw, wout, bout, gffn, wr, br, ltri, utri)


TM = 256
ROW = SUBLANES


def _row_copy(src, src_row, dst, dst_row, sem):
    return pltpu.make_async_copy(src.at[pl.ds(pl.multiple_of(src_row * ROW, ROW), ROW)],
                                 dst.at[pl.ds(pl.multiple_of(dst_row * ROW, ROW), ROW)], sem)


def _moe_kernel(te_ref, nt_ref, src_ref, dst_ref, wrow_ref, h2_hbm, wgu_ref, bgu_ref, wd_ref, bd_ref,
                perm_ref, out_hbm, xbuf, ybuf, wgu_bf, wd_bf, gsem, ssem):
    j = pl.program_id(0)
    nt = nt_ref[0]
    t = j - 1

    @pl.when(j == 0)
    def _():
        ybuf[...] = jnp.zeros_like(ybuf)
        n_real = (out_hbm.shape[0] - N_EXPERTS * TM * ROW) // (TM * ROW)
        fills = [pltpu.make_async_copy(ybuf, out_hbm.at[pl.ds((n_real + i) * TM * ROW, TM * ROW)], ssem.at[0])
                 for i in range(N_EXPERTS)]
        for f in fills:
            f.start()
        for f in fills:
            f.wait()

    @pl.when(j < nt)
    def _():
        slot = j % 2
        for r in range(TM):
            _row_copy(h2_hbm, src_ref[0, 0, r], xbuf.at[slot], r, gsem.at[slot]).start()

    @pl.when((t >= 0) & (t < nt))
    def _():
        slot = t % 2
        e = te_ref[t]
        first = (t == 0) | (te_ref[jnp.maximum(t - 1, 0)] != e)

        @pl.when(first)
        def _():
            perm = perm_ref[...]
            for g in range(2 * D_FF // 256):
                w = wgu_ref[0, :, g * 256:(g + 1) * 256].astype(BF16)
                wgu_bf[:, g * 256:(g + 1) * 256] = _dot(w, perm).astype(BF16)
            wd_bf[...] = wd_ref[0].astype(BF16)

        pltpu.make_async_copy(h2_hbm.at[pl.ds(0, TM * ROW)], xbuf.at[slot], gsem.at[slot]).wait()
        xb = xbuf.at[slot]
        h = jnp.concatenate([xb[pl.ds(s, TM, stride=ROW), :] for s in range(ROW)], axis=1).astype(BF16)
        gu = _dot(h, wgu_bf[...]) + bgu_ref[0]
        acts = []
        for g in range(D_FF // LANES):
            gate = jnp.minimum(gu[:, g * 256:g * 256 + 128], SWIGLU_LIMIT)
            up = jnp.clip(gu[:, g * 256 + 128:(g + 1) * 256], -SWIGLU_LIMIT, SWIGLU_LIMIT)
            acts.append(((up + 1.0) * gate * _sigmoid(SWIGLU_ALPHA * gate)).astype(BF16))
        y = _dot(jnp.concatenate(acts, axis=1), wd_bf[...]) + bd_ref[0]
        rr = lax.broadcasted_iota(jnp.int32, (TM, TM), 0)
        cc = lax.broadcasted_iota(jnp.int32, (TM, TM), 1)
        wcol = jnp.sum(jnp.where(rr == cc, wrow_ref[0], 0.0), axis=1, keepdims=True)
        y = y * wcol

        @pl.when(t >= 1)
        def _():
            pltpu.make_async_copy(ybuf, out_hbm.at[pl.ds(0, TM * ROW)], ssem.at[0]).wait()

        for s in range(ROW):
            ybuf[pl.ds(s, TM, stride=ROW), :] = y[:, s * LANES:(s + 1) * LANES]
        for r in range(TM):
            _row_copy(ybuf, r, out_hbm, dst_ref[0, 0, r], ssem.at[0]).start()

        @pl.when(t == nt - 1)
        def _():
            pltpu.make_async_copy(ybuf, out_hbm.at[pl.ds(0, TM * ROW)], ssem.at[0]).wait()


def _moe_call(te, nt, src, dst, wrow, h2rows, wgu, bgu, wd, bd, perm, out_rows):
    n_tiles = src.shape[0]
    prev = lambda j, te, nt: (jnp.maximum(j - 1, 0), 0, 0)
    exp = lambda j, te, nt: (te[jnp.maximum(j - 1, 0)], 0, 0)
    grid_spec = pltpu.PrefetchScalarGridSpec(
        num_scalar_prefetch=2,
        grid=(n_tiles + 1,),
        in_specs=[
            pl.BlockSpec((1, 1, TM), lambda j, te, nt: (jnp.minimum(j, n_tiles - 1), 0, 0),
                         memory_space=pltpu.SMEM),
            pl.BlockSpec((1, 1, TM), prev, memory_space=pltpu.SMEM),
            pl.BlockSpec((1, 1, TM), prev),
            pl.BlockSpec(memory_space=pl.ANY),
            pl.BlockSpec((1, D_MODEL, 2 * D_FF), exp),
            pl.BlockSpec((1, 1, 2 * D_FF), exp),
            pl.BlockSpec((1, D_FF, D_MODEL), exp),
            pl.BlockSpec((1, 1, D_MODEL), exp),
            pl.BlockSpec((256, 256), lambda j, te, nt: (0, 0)),
        ],
        out_specs=pl.BlockSpec(memory_space=pl.ANY),
        scratch_shapes=[
            pltpu.VMEM((2, TM * ROW, LANES), F32),
            pltpu.VMEM((TM * ROW, LANES), F32),
            pltpu.VMEM((D_MODEL, 2 * D_FF), BF16),
            pltpu.VMEM((D_FF, D_MODEL), BF16),
            pltpu.SemaphoreType.DMA((2,)),
            pltpu.SemaphoreType.DMA((1,)),
        ],
    )
    return pl.pallas_call(
        _moe_kernel,
        grid_spec=grid_spec,
        out_shape=jax.ShapeDtypeStruct((out_rows * ROW, LANES), F32),
        compiler_params=pltpu.CompilerParams(
            dimension_semantics=("arbitrary",), vmem_limit_bytes=VMEM_LIMIT),
        name="moe_experts",
    )(te, nt, src, dst, wrow, h2rows, wgu, bgu, wd, bd, perm)


TF = 256


def _final_kernel(x1_ref, mod_ref, gfin_ref, o0, o1, o2, o3, out_ref):
    acc = None
    for o in (o0, o1, o2, o3):
        v = jnp.concatenate([o[pl.ds(s, TF, stride=ROW), :] for s in range(ROW)], axis=1)
        acc = v if acc is None else acc + v
    xf = x1_ref[...] + mod_ref[0][5:6] * acc
    out_ref[...] = _rms(xf) * gfin_ref[...]


def _final_call(x1, mod, gfin, yrows, S):
    N, D = x1.shape
    nblk = N // TF
    per_b = S // TF
    slab = lambda k: pl.BlockSpec((TF * ROW, LANES), lambda i, k=k: (k * nblk + i, 0))
    return pl.pallas_call(
        _final_kernel,
        grid=(nblk,),
        in_specs=[
            pl.BlockSpec((TF, D), lambda i: (i, 0)),
            pl.BlockSpec((1, 8, D), lambda i: (i // per_b, 0, 0)),
            pl.BlockSpec((1, D), lambda i: (0, 0)),
            slab(0), slab(1), slab(2), slab(3),
        ],
        out_specs=pl.BlockSpec((TF, D), lambda i: (i, 0)),
        out_shape=jax.ShapeDtypeStruct((N, D), F32),
        compiler_params=pltpu.CompilerParams(dimension_semantics=("arbitrary",)),
        name="combine_norm",
    )(x1, mod, gfin, yrows, yrows, yrows, yrows)


def _route_plan(idx, wts, rank, cnt, N):
    n_tiles = N * TOP_K // TM + N_EXPERTS
    counts = cnt[:, 0]
    ntile = (counts + TM - 1) // TM
    tile_end = jnp.cumsum(ntile)
    row_start = (tile_end - ntile) * TM
    nt = tile_end[-1:].astype(jnp.int32)
    pos = (jnp.take(row_start, idx) + rank).reshape(-1)
    te = jnp.minimum(jnp.sum(jnp.arange(n_tiles)[:, None] >= tile_end[None, :], axis=1),
                     N_EXPERTS - 1).astype(jnp.int32)
    slot = jnp.arange(n_tiles * TM, dtype=jnp.int32)
    se = jnp.take(te, slot // TM)
    pad_off = jnp.clip(slot - jnp.take(row_start + counts, se), 0, TM - 1)
    dst = TOP_K * N + se * TM + pad_off
    assign = jnp.arange(TOP_K * N, dtype=jnp.int32)
    src = jnp.zeros((n_tiles * TM,), jnp.int32).at[pos].set(assign % N)
    dst = dst.at[pos].set(assign)
    wrow = jnp.zeros((n_tiles * TM,), F32).at[pos].set(wts.reshape(-1))
    shp = (n_tiles, 1, TM)
    return te, nt, src.reshape(shp), dst.reshape(shp), wrow.reshape(shp)


def _tri_constants():
    r = np.arange(TS)[:, None]
    c = np.arange(TS)[None, :]
    ltri = ((r // CHUNK) == (c // CHUNK)) & (c <= r)
    utri = r < c
    return jnp.asarray(ltri, BF16), jnp.asarray(utri, BF16)


def kernel(x, c, w_ada, b_ada, g_mix, w_in, b_in, attn_sinks, rel_bias, hg_lb, hg_norm_w, w_out, b_out, g_ffn, w_router, b_router, w_gate_up, b_gate_up, w_down, b_down, g_final):
    B, S, D = x.shape
    N = B * S
    mod6 = _prep_call(c, w_ada[0], b_ada)
    mod = jnp.pad(jnp.transpose(mod6, (1, 0, 2)), ((0, 0), (0, 2), (0, 0)))
    lb, bias = _tables_call(rel_bias, hg_lb, _bucket_table())

    wi, bi_ = w_in[0], b_in[0]
    def cols(a):
        aq, ak, av, rest = a[..., :512], a[..., 512:640], a[..., 640:768], a[..., 768:]
        k0, k1, v0, v1 = ak[..., :64], ak[..., 64:], av[..., :64], av[..., 64:]
        return jnp.concatenate([aq, k0, k0, k1, k1, v0, v0, v1, v1, rest], axis=-1)
    win = cols(wi).astype(BF16)
    bin_ = cols(bi_)[None, :]
    wrt = jnp.transpose(w_router[0])
    wr_hi = wrt.astype(BF16)
    wr_lo = (wrt - wr_hi.astype(F32)).astype(BF16)
    wr = jnp.concatenate([wr_hi, wr_lo], axis=0)
    br = jnp.broadcast_to(b_router[0][:, None], (N_EXPERTS, LANES))
    ltri, utri = _tri_constants()

    x1, h2rows, idx, wts, rank, cnt = _mix_call(
        attn_sinks[0], x, mod, g_mix, win, bin_, bias, lb, hg_norm_w, w_out[0].astype(BF16),
        b_out, g_ffn, wr, br, ltri, utri)

    te, nt, src, dst, wrow = _route_plan(idx, wts, rank, cnt, N)
    ii = np.arange(128)
    perm_np = np.zeros((256, 256), np.float32)
    perm_np[2 * ii, ii] = 1.0
    perm_np[2 * ii + 1, 128 + ii] = 1.0
    perm = jnp.asarray(perm_np, BF16)
    bgu = b_gate_up[0].reshape(N_EXPERTS, D_FF // LANES, LANES, 2)
    bgu = jnp.transpose(bgu, (0, 1, 3, 2)).reshape(N_EXPERTS, 1, 2 * D_FF)
    out_rows = TOP_K * N + N_EXPERTS * TM
    yrows = _moe_call(te, nt, src, dst, wrow, h2rows, w_gate_up[0], bgu, w_down[0],
                      b_down[0][:, None, :], perm, out_rows)
    out = _final_call(x1.reshape(N, D), mod, g_final[None, :], yrows, S)
    return out.reshape(B, S, D)
```

```python
import functools
import math

import numpy as np
import jax
import jax.numpy as jnp
from jax import lax
from jax.experimental import pallas as pl
from jax.experimental.pallas import tpu as pltpu

D_MODEL = 1024
ATT_HEAD_DIM = 64
ATT_HEADS = 8
ATT_KV_HEADS = 2
ATT_SCALE = ATT_HEAD_DIM ** -0.5
BLOCK = 128
N_BUCKETS = 32
MAX_DISTANCE = 128
HG_HEADS = 4
HG_D = 128
CHUNK = 64
N_EXPERTS = 32
TOP_K = 4
D_FF = 1024
SWIGLU_LIMIT = 7.0
SWIGLU_ALPHA = 1.702
EPS = 1e-5

ATT_Q = ATT_HEADS * ATT_HEAD_DIM
HG_W = HG_HEADS * HG_D
COL_Q = 0
COL_KV = ATT_Q
COL_HQ = COL_KV + 4 * 128
COL_HF = COL_HQ + HG_W
COL_HI = COL_HF + HG_W
COL_HG = COL_HI + HG_W
PROJ_COLS = COL_HG + HG_W

LANES = 128
SUBLANES = 8
TS = 256
VMEM_LIMIT = 56 * 1024 * 1024
HG_SAFE_LEVELS = (32, 16, 8)
HG_DIAG = 8
HG_EXP_CAP = 80.0

F32 = jnp.float32
BF16 = jnp.bfloat16
NEG_INF = float("-inf")


def _dot(a, b):
    return jnp.dot(a, b, preferred_element_type=F32)


def _dot_nt(a, b):
    return lax.dot_general(a, b, (((1,), (1,)), ((), ())), preferred_element_type=F32)


def _dot_tn(a, b):
    return lax.dot_general(a, b, (((0,), (0,)), ((), ())), preferred_element_type=F32)


def _sigmoid(x):
    return 1.0 / (1.0 + jnp.exp(-x))


def _prep_kernel(c_ref, w_ref, b_ref, o_ref):
    c = c_ref[...]
    cond = c * _sigmoid(c)
    o_ref[0] = jnp.dot(cond, w_ref[...], precision=lax.Precision.HIGHEST,
                       preferred_element_type=F32) + b_ref[...]


def _prep_call(c, w_ada, b_ada):
    B = c.shape[0]
    return pl.pallas_call(
        _prep_kernel,
        grid=(6,),
        in_specs=[
            pl.BlockSpec((B, D_MODEL), lambda j: (0, 0)),
            pl.BlockSpec((D_MODEL, D_MODEL), lambda j: (0, j)),
            pl.BlockSpec((1, D_MODEL), lambda j: (0, j)),
        ],
        out_specs=pl.BlockSpec((1, B, D_MODEL), lambda j: (j, 0, 0)),
        out_shape=jax.ShapeDtypeStruct((6, B, D_MODEL), F32),
        compiler_params=pltpu.CompilerParams(dimension_semantics=("arbitrary",)),
        name="adaln_mod",
    )(c, w_ada, b_ada)


def _tables_kernel(rb_ref, lbp_ref, bucket_ref, lb_ref, bias_ref):
    p = lbp_ref[...]
    e = jnp.exp(p - jnp.max(p, axis=0, keepdims=True))
    sm = e / jnp.sum(e, axis=0, keepdims=True)
    lb_ref[...] = sm[0:1]
    bucket = bucket_ref[...]
    for h in range(ATT_HEADS):
        acc = jnp.full(bucket.shape, NEG_INF, F32)
        for bk in range(N_BUCKETS):
            acc = jnp.where(bucket == bk, rb_ref[bk, h], acc)
        bias_ref[h] = acc


def _tables_call(rel_bias, hg_lb, bucket):
    return pl.pallas_call(
        _tables_kernel,
        in_specs=[
            pl.BlockSpec(memory_space=pltpu.SMEM),
            pl.BlockSpec(memory_space=pltpu.VMEM),
            pl.BlockSpec(memory_space=pltpu.VMEM),
        ],
        out_specs=[pl.BlockSpec(memory_space=pltpu.VMEM), pl.BlockSpec(memory_space=pltpu.VMEM)],
        out_shape=[jax.ShapeDtypeStruct((1, HG_W), F32),
                   jax.ShapeDtypeStruct((ATT_HEADS, BLOCK, 2 * BLOCK), F32)],
        name="tables",
    )(rel_bias, hg_lb, bucket)


def _bucket_table():
    i = jnp.arange(BLOCK, dtype=jnp.int32)[:, None]
    m = jnp.arange(2 * BLOCK, dtype=jnp.int32)[None, :]
    dist = i + BLOCK - m
    n = jnp.maximum(dist, 0)
    max_exact = N_BUCKETS // 2
    nf = jnp.maximum(n, 1).astype(F32)
    large = max_exact + (jnp.log(nf / max_exact) / math.log(MAX_DISTANCE / max_exact)
                         * (N_BUCKETS - max_exact)).astype(jnp.int32)
    large = jnp.minimum(large, N_BUCKETS - 1)
    bucket = jnp.where(n < max_exact, n, large)
    return jnp.where((dist >= 0) & (dist < BLOCK), bucket, -1).astype(jnp.int32)


def _rms(x):
    return x * lax.rsqrt(jnp.mean(x * x, axis=-1, keepdims=True) + EPS)


def _attention_block(proj_ref, kvprev_ref, bias_ref, sink_ref, mixed_ref, blk, first_tile):
    r0 = blk * BLOCK
    cur = proj_ref[r0:r0 + BLOCK, COL_KV:COL_KV + 512]
    if blk == 0:
        prev = kvprev_ref[...]
    else:
        prev = proj_ref[r0 - BLOCK:r0, COL_KV:COL_KV + 512]
    band = jnp.concatenate([prev, cur], axis=0)
    lane = lax.broadcasted_iota(jnp.int32, (2 * BLOCK, LANES), 1)
    lo = lane < ATT_HEAD_DIM
    col = lax.broadcasted_iota(jnp.int32, (1, 2 * BLOCK), 1)
    if blk == 0:
        pen = jnp.where((col < BLOCK) & first_tile, NEG_INF, 0.0)
    for g in range(ATT_KV_HEADS):
        kd = band[:, g * 128:(g + 1) * 128]
        vd = band[:, 256 + g * 128:256 + (g + 1) * 128]
        kb = jnp.concatenate([jnp.where(lo, kd, 0.0), jnp.where(lo, 0.0, kd)], axis=0).astype(BF16)
        vb = jnp.concatenate([jnp.where(lo, vd, 0.0), jnp.where(lo, 0.0, vd)], axis=0).astype(BF16)
        for pp in range(2):
            pair = g * 2 + pp
            qp = (proj_ref[r0:r0 + BLOCK, pair * 128:(pair + 1) * 128] * ATT_SCALE).astype(BF16)
            s = _dot_nt(qp, kb)
            ps = []
            inv = []
            for hh in range(2):
                h = pair * 2 + hh
                sh = s[:, hh * 256:(hh + 1) * 256] + bias_ref[h]
                if blk == 0:
                    sh = sh + pen
                sink = sink_ref[h]
                mx = jnp.maximum(jnp.max(sh, axis=-1, keepdims=True), sink)
                p = jnp.exp(sh - mx)
                den = jnp.sum(p, axis=-1, keepdims=True) + jnp.exp(sink - mx)
                ps.append(p.astype(BF16))
                inv.append(1.0 / den)
            o = _dot(jnp.concatenate(ps, axis=1), vb)
            lane_o = lax.broadcasted_iota(jnp.int32, (BLOCK, LANES), 1)
            o = o * jnp.where(lane_o < ATT_HEAD_DIM, inv[0], inv[1])
            mixed_ref[r0:r0 + BLOCK, pair * 128:(pair + 1) * 128] = o.astype(BF16)


def _hgrn_head(proj_ref, state_ref, b_all, kk, hgw_ref, mixed_ref, hd, masks):
    c0 = hd * HG_D
    kk_all = kk[:, c0:c0 + HG_D]
    qr = proj_ref[:, COL_HQ + c0:COL_HQ + c0 + HG_D]
    qf_all = qr * _sigmoid(qr)
    st = state_ref[hd]
    for ck in range(TS // CHUNK):
        r0 = ck * CHUNK
        b = b_all[r0:r0 + CHUNK, c0:c0 + HG_D]
        q = qf_all[r0:r0 + CHUNK]
        k = kk_all[r0:r0 + CHUNK]
        v = proj_ref[r0:r0 + CHUNK, COL_HI + c0:COL_HI + c0 + HG_D].astype(BF16)
        scores = jnp.zeros((CHUNK, CHUNK), F32)
        for m in HG_SAFE_LEVELS:
            qs, ks = [], []
            for bi in range(CHUNK // m):
                rows = slice(bi * m, (bi + 1) * m)
                if bi % 2 == 1:
                    ref = b[bi * m - 1:bi * m]
                    qs.append(q[rows] * jnp.exp(b[rows] - ref))
                    ks.append(jnp.zeros((m, HG_D), F32))
                else:
                    ref = b[(bi + 1) * m - 1:(bi + 1) * m]
                    qs.append(jnp.zeros((m, HG_D), F32))
                    ks.append(k[rows] * jnp.exp(ref - b[rows]))
            ql = jnp.concatenate(qs, axis=0).astype(BF16)
            kl = jnp.concatenate(ks, axis=0).astype(BF16)
            scores = scores + jnp.where(masks[m], _dot_nt(ql, kl), 0.0)
        qs, ks = [], []
        for bi in range(CHUNK // HG_DIAG):
            rows = slice(bi * HG_DIAG, (bi + 1) * HG_DIAG)
            c = b[rows] if bi == 0 else b[rows] - b[bi * HG_DIAG - 1:bi * HG_DIAG]
            qs.append(q[rows] * jnp.exp(c))
            ks.append(k[rows] * jnp.exp(jnp.minimum(-c, HG_EXP_CAP)))
        qd = jnp.concatenate(qs, axis=0).astype(BF16)
        kd = jnp.concatenate(ks, axis=0).astype(BF16)
        scores = scores + jnp.where(masks[0], _dot_nt(qd, kd), 0.0)
        o = _dot(scores.astype(BF16), v) + _dot_nt((q * jnp.exp(b)).astype(BF16), st.astype(BF16))
        bend = b[CHUNK - 1:CHUNK]
        kdec = (k * jnp.exp(bend - b)).astype(BF16)
        st = st * jnp.exp(bend) + _dot_tn(v, kdec)
        gate = proj_ref[r0:r0 + CHUNK, COL_HG + c0:COL_HG + c0 + HG_D]
        on = _rms(o) * hgw_ref[...] * (gate * _sigmoid(gate))
        mixed_ref[r0:r0 + CHUNK, ATT_Q + c0:ATT_Q + c0 + HG_D] = on.astype(BF16)
    state_ref[hd] = st


def _hgrn_masks():
    r = lax.broadcasted_iota(jnp.int32, (CHUNK, CHUNK), 0)
    c = lax.broadcasted_iota(jnp.int32, (CHUNK, CHUNK), 1)
    masks = {}
    for m in HG_SAFE_LEVELS:
        masks[m] = ((r // (2 * m)) == (c // (2 * m))) & (((r // m) % 2) == 1) & (((c // m) % 2) == 0)
    masks[0] = ((r // HG_DIAG) == (c // HG_DIAG)) & (c <= r)
    return masks


def _mix_kernel(sink_ref, x_ref, mod_ref, gmix_ref, win_ref, bin_ref, bias_ref, lb_ref, hgw_ref,
                wout_ref, bout_ref, gffn_ref, wr_ref, br_ref, ltri_ref, utri_ref,
                x1_ref, h2_ref, idx_ref, wts_ref, rank_ref, cnt_ref,
                proj_ref, mixed_ref, kvprev_ref, state_ref, run_ref):
    bi = pl.program_id(0)
    j = pl.program_id(1)

    @pl.when(j == 0)
    def _():
        kvprev_ref[...] = jnp.zeros_like(kvprev_ref)
        state_ref[...] = jnp.zeros_like(state_ref)

    @pl.when((j == 0) & (bi == 0))
    def _():
        run_ref[...] = jnp.zeros_like(run_ref)

    x = x_ref[0]
    mod = mod_ref[0]
    h = _rms(x) * (gmix_ref[...] * (1.0 + mod[1:2])) + mod[0:1]
    proj_ref[...] = _dot(h.astype(BF16), win_ref[...]) + bin_ref[...]

    for blk in range(TS // BLOCK):
        _attention_block(proj_ref, kvprev_ref, bias_ref, sink_ref, mixed_ref, blk, j == 0)
    kvprev_ref[...] = proj_ref[TS - BLOCK:TS, COL_KV:COL_KV + 512]

    lbv = lb_ref[...]
    fl = lbv + (1.0 - lbv) * _sigmoid(proj_ref[:, COL_HF:COL_HF + HG_W])
    g = jnp.log(fl)
    g_hi = g.astype(BF16)
    r1 = g - g_hi.astype(F32)
    g_mid = r1.astype(BF16)
    g_lo = (r1 - g_mid.astype(F32)).astype(BF16)
    ltri = ltri_ref[...]
    b_all = _dot(ltri, g_hi) + _dot(ltri, g_mid) + _dot(ltri, g_lo)

    masks = _hgrn_masks()
    for hd in range(HG_HEADS):
        _hgrn_head(proj_ref, state_ref, b_all, 1.0 - fl, hgw_ref, mixed_ref, hd, masks)

    y = _dot(mixed_ref[...], wout_ref[...]) + bout_ref[...]
    x1 = x + mod[2:3] * y
    x1_ref[0] = x1
    h2 = _rms(x1) * (gffn_ref[...] * (1.0 + mod[4:5])) + mod[3:4]
    for s in range(SUBLANES):
        h2_ref[pl.ds(s, TS, stride=SUBLANES), :] = h2[:, s * LANES:(s + 1) * LANES]

    h_hi = h2.astype(BF16)
    h_lo = (h2 - h_hi.astype(F32)).astype(BF16)
    wr = wr_ref[...]
    t1 = _dot_nt(wr, h_hi)
    t2 = _dot_nt(wr[0:N_EXPERTS], h_lo)
    logits = t1[0:N_EXPERTS] + t1[N_EXPERTS:2 * N_EXPERTS] + t2 + br_ref[:, 0:1]

    eidx = lax.broadcasted_iota(jnp.int32, (N_EXPERTS, TS), 0)
    l = logits
    vals, ohs, ids = [], [], []
    for _ in range(TOP_K):
        mx = jnp.max(l, axis=0, keepdims=True)
        ik = jnp.min(jnp.where(l == mx, eidx, N_EXPERTS), axis=0, keepdims=True)
        oh = eidx == ik
        vals.append(mx)
        ids.append(ik)
        ohs.append(oh)
        l = jnp.where(oh, NEG_INF, l)
    es = [jnp.exp(v - vals[0]) for v in vals]
    den = es[0] + es[1] + es[2] + es[3]
    run = run_ref[...]
    base = run[:, 0:1]
    utri = utri_ref[...]
    ranks = []
    for kq in range(TOP_K):
        ohf = jnp.where(ohs[kq], 1.0, 0.0)
        pref = _dot(ohf.astype(BF16), utri)
        ranks.append(jnp.sum(jnp.where(ohs[kq], base + pref, 0.0), axis=0, keepdims=True))
        base = base + jnp.sum(ohf, axis=1, keepdims=True)
    run_new = jnp.broadcast_to(base, run.shape)
    run_ref[...] = run_new
    cnt_ref[...] = run_new.astype(jnp.int32)
    idx_ref[...] = jnp.concatenate(ids, axis=0)
    wts_ref[...] = jnp.concatenate([e / den for e in es], axis=0)
    rank_ref[...] = jnp.concatenate(ranks, axis=0).astype(jnp.int32)


def _mix_call(sinks, x, mod, gmix, win, bin_, bias, lb, hgw, wout, bout, gffn, wr, br, ltri, utri):
    B, S, D = x.shape
    N = B * S
    nj = S // TS
    const2 = lambda b, j: (0, 0)
    tok = lambda b, j: (0, b * nj + j)
    in_specs = [
        pl.BlockSpec(memory_space=pltpu.SMEM),
        pl.BlockSpec((1, TS, D), lambda b, j: (b, j, 0)),
        pl.BlockSpec((1, 8, D), lambda b, j: (b, 0, 0)),
        pl.BlockSpec((1, D), const2),
        pl.BlockSpec((D, PROJ_COLS), const2),
        pl.BlockSpec((1, PROJ_COLS), const2),
        pl.BlockSpec((ATT_HEADS, BLOCK, 2 * BLOCK), lambda b, j: (0, 0, 0)),
        pl.BlockSpec((1, HG_W), const2),
        pl.BlockSpec((1, HG_D), const2),
        pl.BlockSpec((D, D), const2),
        pl.BlockSpec((1, D), const2),
        pl.BlockSpec((1, D), const2),
        pl.BlockSpec((2 * N_EXPERTS, D), const2),
        pl.BlockSpec((N_EXPERTS, LANES), const2),
        pl.BlockSpec((TS, TS), const2),
        pl.BlockSpec((TS, TS), const2),
    ]
    out_specs = [
        pl.BlockSpec((1, TS, D), lambda b, j: (b, j, 0)),
        pl.BlockSpec((TS * SUBLANES, LANES), lambda b, j: (b * nj + j, 0)),
        pl.BlockSpec((TOP_K, TS), tok),
        pl.BlockSpec((TOP_K, TS), tok),
        pl.BlockSpec((TOP_K, TS), tok),
        pl.BlockSpec((N_EXPERTS, LANES), const2),
    ]
    out_shape = [
        jax.ShapeDtypeStruct((B, S, D), F32),
        jax.ShapeDtypeStruct((N * SUBLANES, LANES), F32),
        jax.ShapeDtypeStruct((TOP_K, N), jnp.int32),
        jax.ShapeDtypeStruct((TOP_K, N), F32),
        jax.ShapeDtypeStruct((TOP_K, N), jnp.int32),
        jax.ShapeDtypeStruct((N_EXPERTS, LANES), jnp.int32),
    ]
    scratch = [
        pltpu.VMEM((TS, PROJ_COLS), F32),
        pltpu.VMEM((TS, D), BF16),
        pltpu.VMEM((BLOCK, 512), F32),
        pltpu.VMEM((HG_HEADS, HG_D, HG_D), F32),
        pltpu.VMEM((N_EXPERTS, LANES), F32),
    ]
    return pl.pallas_call(
        _mix_kernel,
        grid=(B, nj),
        in_specs=in_specs,
        out_specs=out_specs,
        out_shape=out_shape,
        scratch_shapes=scratch,
        compiler_params=pltpu.CompilerParams(
            dimension_semantics=("arbitrary", "arbitrary"), vmem_limit_bytes=VMEM_LIMIT),
        name="mix_router",
    )(sinks, x, mod, gmix, win, bin_, bias, lb, hgw, wout, bout, gffn, wr, br, ltri, utri)


TM = 256
TD = 256
ROW = SUBLANES


def _row_copy(src, src_row, dst, dst_row, sem):
    return pltpu.make_async_copy(src.at[pl.ds(pl.multiple_of(src_row * ROW, ROW), ROW)],
                                 dst.at[pl.ds(pl.multiple_of(dst_row * ROW, ROW), ROW)], sem)


def _col_from_row(w_row, n):
    rr = lax.broadcasted_iota(jnp.int32, (n, n), 0)
    cc = lax.broadcasted_iota(jnp.int32, (n, n), 1)
    return jnp.sum(jnp.where(rr == cc, w_row, 0.0), axis=1, keepdims=True)


def _rows_to_matrix(ref, base, n):
    return jnp.concatenate([ref[pl.ds(base + s, n, stride=ROW), :] for s in range(ROW)], axis=1)


def _dispatch_kernel(tend_ref, pos_ref, h2_hbm, xs_hbm, zbuf, sem, zsem):
    i = pl.program_id(0)
    n = pl.num_programs(0)
    batch = TOP_K * TD * ROW

    n_tiles = xs_hbm.shape[0] // (TM * ROW)

    def fill_tile(tile):
        return pltpu.make_async_copy(
            zbuf, xs_hbm.at[pl.ds(pl.multiple_of(tile * (TM * ROW), TM * ROW), TM * ROW)], zsem.at[0])

    def padded(e):
        return tend_ref[e] > (tend_ref[e - 1] if e > 0 else 0), tend_ref[e] - 1

    def unused(u):
        tile = tend_ref[N_EXPERTS - 1] + u
        return tile < n_tiles, tile

    @pl.when(i == 0)
    def _():
        zbuf[...] = jnp.zeros_like(zbuf)
        sites = [padded(e) for e in range(N_EXPERTS)] + [unused(u) for u in range(N_EXPERTS)]
        for cond, tile in sites:
            pl.when(cond)(lambda tile=tile: fill_tile(tile).start())
        for cond, tile in sites:
            pl.when(cond)(lambda tile=tile: fill_tile(tile).wait())

    for q in range(TOP_K * TD):
        _row_copy(h2_hbm, i * TD + q % TD, xs_hbm, pos_ref[0, 0, q], sem.at[0]).start()

    def wait_batch():
        pltpu.make_async_copy(h2_hbm.at[pl.ds(0, batch)], xs_hbm.at[pl.ds(0, batch)], sem.at[0]).wait()

    pl.when(i > 0)(wait_batch)
    pl.when(i == n - 1)(wait_batch)


def _dispatch_call(tile_end, pos_t, h2rows, n_tiles):
    nblk = pos_t.shape[0]
    grid_spec = pltpu.PrefetchScalarGridSpec(
        num_scalar_prefetch=1,
        grid=(nblk,),
        in_specs=[
            pl.BlockSpec((1, 1, TOP_K * TD), lambda i, tend: (i, 0, 0), memory_space=pltpu.SMEM),
            pl.BlockSpec(memory_space=pl.ANY),
        ],
        out_specs=pl.BlockSpec(memory_space=pl.ANY),
        scratch_shapes=[
            pltpu.VMEM((TM * ROW, LANES), F32),
            pltpu.SemaphoreType.DMA((1,)),
            pltpu.SemaphoreType.DMA((1,)),
        ],
    )
    return pl.pallas_call(
        _dispatch_kernel,
        grid_spec=grid_spec,
        out_shape=jax.ShapeDtypeStruct((n_tiles * TM * ROW, LANES), F32),
        compiler_params=pltpu.CompilerParams(dimension_semantics=("arbitrary",)),
        name="moe_dispatch",
    )(tile_end, pos_t, h2rows)


def _moe_kernel(te_ref, nt_ref, xs_ref, wgu_ref, bgu_ref, wd_ref, bd_ref, perm_ref, ys_ref, wgu_bf, wd_bf):
    j = pl.program_id(0)
    nt = nt_ref[0]

    @pl.when(j < nt)
    def _():
        e = te_ref[j]
        first = (j == 0) | (te_ref[jnp.maximum(j - 1, 0)] != e)

        @pl.when(first)
        def _():
            perm = perm_ref[...]
            for g in range(2 * D_FF // 256):
                w = wgu_ref[0, :, g * 256:(g + 1) * 256].astype(BF16)
                wgu_bf[:, g * 256:(g + 1) * 256] = _dot(w, perm).astype(BF16)
            wd_bf[...] = wd_ref[0].astype(BF16)

        h = _rows_to_matrix(xs_ref, 0, TM).astype(BF16)
        gu = _dot(h, wgu_bf[...]) + bgu_ref[0]
        acts = []
        for g in range(D_FF // LANES):
            gate = jnp.minimum(gu[:, g * 256:g * 256 + 128], SWIGLU_LIMIT)
            up = jnp.clip(gu[:, g * 256 + 128:(g + 1) * 256], -SWIGLU_LIMIT, SWIGLU_LIMIT)
            acts.append(((up + 1.0) * gate * _sigmoid(SWIGLU_ALPHA * gate)).astype(BF16))
        y = _dot(jnp.concatenate(acts, axis=1), wd_bf[...]) + bd_ref[0]
        for s in range(ROW):
            ys_ref[pl.ds(s, TM, stride=ROW), :] = y[:, s * LANES:(s + 1) * LANES]

    @pl.when(j >= nt)
    def _():
        ys_ref[...] = jnp.zeros_like(ys_ref)


def _moe_call(te, nt, xs, wgu, bgu, wd, bd, perm):
    n_tiles = xs.shape[0] // (TM * ROW)
    tile = lambda j, te, nt: (jnp.minimum(j, nt[0] - 1), 0)
    exp = lambda j, te, nt: (te[jnp.minimum(j, nt[0] - 1)], 0, 0)
    grid_spec = pltpu.PrefetchScalarGridSpec(
        num_scalar_prefetch=2,
        grid=(n_tiles,),
        in_specs=[
            pl.BlockSpec((TM * ROW, LANES), tile),
            pl.BlockSpec((1, D_MODEL, 2 * D_FF), exp),
            pl.BlockSpec((1, 1, 2 * D_FF), exp),
            pl.BlockSpec((1, D_FF, D_MODEL), exp),
            pl.BlockSpec((1, 1, D_MODEL), exp),
            pl.BlockSpec((256, 256), lambda j, te, nt: (0, 0)),
        ],
        out_specs=pl.BlockSpec((TM * ROW, LANES), lambda j, te, nt: (j, 0)),
        scratch_shapes=[
            pltpu.VMEM((D_MODEL, 2 * D_FF), BF16),
            pltpu.VMEM((D_FF, D_MODEL), BF16),
        ],
    )
    return pl.pallas_call(
        _moe_kernel,
        grid_spec=grid_spec,
        out_shape=jax.ShapeDtypeStruct(xs.shape, F32),
        compiler_params=pltpu.CompilerParams(
            dimension_semantics=("arbitrary",), vmem_limit_bytes=VMEM_LIMIT),
        name="moe_experts",
    )(te, nt, xs, wgu, bgu, wd, bd, perm)


def _final_kernel(pnext_ref, pcur_ref, wts_ref, x1_ref, mod_ref, gfin_ref, ys_hbm, out_ref, gbuf, sem):
    i = pl.program_id(0)
    n = pl.num_programs(0)

    def issue(pref, slot):
        for q in range(TOP_K * TD):
            _row_copy(ys_hbm, pref[0, 0, q], gbuf.at[slot], q, sem.at[slot]).start()

    pl.when(i == 0)(lambda: issue(pcur_ref, 0))
    pl.when(i + 1 < n)(lambda: issue(pnext_ref, (i + 1) % 2))
    slot = i % 2
    pltpu.make_async_copy(ys_hbm.at[pl.ds(0, TOP_K * TD * ROW)], gbuf.at[slot], sem.at[slot]).wait()
    gb = gbuf.at[slot]
    acc = jnp.zeros((TD, D_MODEL), F32)
    for kq in range(TOP_K):
        v = _rows_to_matrix(gb, kq * TD * ROW, TD)
        acc = acc + _col_from_row(wts_ref[kq:kq + 1, :], TD) * v
    xf = x1_ref[...] + mod_ref[0][5:6] * acc
    out_ref[...] = _rms(xf) * gfin_ref[...]


def _final_call(pos_t, wts, x1, mod, gfin, ys, S):
    N, D = x1.shape
    nblk = N // TD
    per_b = S // TD
    return pl.pallas_call(
        _final_kernel,
        grid=(nblk,),
        in_specs=[
            pl.BlockSpec((1, 1, TOP_K * TD), lambda i: (jnp.minimum(i + 1, nblk - 1), 0, 0),
                         memory_space=pltpu.SMEM),
            pl.BlockSpec((1, 1, TOP_K * TD), lambda i: (i, 0, 0), memory_space=pltpu.SMEM),
            pl.BlockSpec((TOP_K, TD), lambda i: (0, i)),
            pl.BlockSpec((TD, D), lambda i: (i, 0)),
            pl.BlockSpec((1, 8, D), lambda i: (i // per_b, 0, 0)),
            pl.BlockSpec((1, D), lambda i: (0, 0)),
            pl.BlockSpec(memory_space=pl.ANY),
        ],
        out_specs=pl.BlockSpec((TD, D), lambda i: (i, 0)),
        out_shape=jax.ShapeDtypeStruct((N, D), F32),
        scratch_shapes=[
            pltpu.VMEM((2, TOP_K * TD * ROW, LANES), F32),
            pltpu.SemaphoreType.DMA((2,)),
        ],
        compiler_params=pltpu.CompilerParams(
            dimension_semantics=("arbitrary",), vmem_limit_bytes=VMEM_LIMIT),
        name="combine_norm",
    )(pos_t, pos_t, wts, x1, mod, gfin, ys)


def _route_plan(idx, rank, cnt, N):
    n_tiles = N * TOP_K // TM + N_EXPERTS
    counts = cnt[:, 0]
    ntile = (counts + TM - 1) // TM
    tile_end = jnp.cumsum(ntile).astype(jnp.int32)
    row_start = (tile_end - ntile) * TM
    nt = tile_end[-1:]
    experts = jnp.arange(N_EXPERTS, dtype=jnp.int32)
    pos = jnp.sum(jnp.where(idx[..., None] == experts, row_start, 0), axis=-1) + rank
    te = jnp.minimum(jnp.sum(jnp.arange(n_tiles)[:, None] >= tile_end[None, :], axis=1),
                     N_EXPERTS - 1).astype(jnp.int32)
    pos_t = jnp.transpose(pos.reshape(TOP_K, N // TD, TD), (1, 0, 2)).reshape(N // TD, 1, TOP_K * TD)
    return te, nt, tile_end, pos_t, n_tiles


def _tri_constants():
    r = np.arange(TS)[:, None]
    c = np.arange(TS)[None, :]
    ltri = ((r // CHUNK) == (c // CHUNK)) & (c <= r)
    utri = r < c
    return jnp.asarray(ltri, BF16), jnp.asarray(utri, BF16)


def kernel(x, c, w_ada, b_ada, g_mix, w_in, b_in, attn_sinks, rel_bias, hg_lb, hg_norm_w, w_out, b_out, g_ffn, w_router, b_router, w_gate_up, b_gate_up, w_down, b_down, g_final):
    B, S, D = x.shape
    N = B * S
    mod6 = _prep_call(c, w_ada[0], b_ada)
    mod = jnp.pad(jnp.transpose(mod6, (1, 0, 2)), ((0, 0), (0, 2), (0, 0)))
    lb, bias = _tables_call(rel_bias, hg_lb, _bucket_table())

    wi, bi_ = w_in[0], b_in[0]
    def cols(a):
        aq, ak, av, rest = a[..., :512], a[..., 512:640], a[..., 640:768], a[..., 768:]
        k0, k1, v0, v1 = ak[..., :64], ak[..., 64:], av[..., :64], av[..., 64:]
        return jnp.concatenate([aq, k0, k0, k1, k1, v0, v0, v1, v1, rest], axis=-1)
    win = cols(wi).astype(BF16)
    bin_ = cols(bi_)[None, :]
    wrt = jnp.transpose(w_router[0])
    wr_hi = wrt.astype(BF16)
    wr_lo = (wrt - wr_hi.astype(F32)).astype(BF16)
    wr = jnp.concatenate([wr_hi, wr_lo], axis=0)
    br = jnp.broadcast_to(b_router[0][:, None], (N_EXPERTS, LANES))
    ltri, utri = _tri_constants()

    x1, h2rows, idx, wts, rank, cnt = _mix_call(
        attn_sinks[0], x, mod, g_mix, win, bin_, bias, lb, hg_norm_w, w_out[0].astype(BF16),
        b_out, g_ffn, wr, br, ltri, utri)

    te, nt, tile_end, pos_t, n_tiles = _route_plan(idx, rank, cnt, N)
    ii = np.arange(128)
    perm_np = np.zeros((256, 256), np.float32)
    perm_np[2 * ii, ii] = 1.0
    perm_np[2 * ii + 1, 128 + ii] = 1.0
    perm = jnp.asarray(perm_np, BF16)
    bgu = b_gate_up[0].reshape(N_EXPERTS, D_FF // LANES, LANES, 2)
    bgu = jnp.transpose(bgu, (0, 1, 3, 2)).reshape(N_EXPERTS, 1, 2 * D_FF)
    xs = _dispatch_call(tile_end, pos_t, h2rows, n_tiles)
    ys = _moe_call(te, nt, xs, w_gate_up[0], bgu, w_down[0], b_down[0][:, None, :], perm)
    out = _final_call(pos_t, wts, x1.reshape(N, D), mod, g_final[None, :], ys, S)
    return out.reshape(B, S, D)
```

```python
import functools
import math

import numpy as np
import jax
import jax.numpy as jnp
from jax import lax
from jax.experimental import pallas as pl
from jax.experimental.pallas import tpu as pltpu

D_MODEL = 1024
ATT_HEAD_DIM = 64
ATT_HEADS = 8
ATT_KV_HEADS = 2
ATT_SCALE = ATT_HEAD_DIM ** -0.5
BLOCK = 128
N_BUCKETS = 32
MAX_DISTANCE = 128
HG_HEADS = 4
HG_D = 128
CHUNK = 64
N_EXPERTS = 32
TOP_K = 4
D_FF = 1024
SWIGLU_LIMIT = 7.0
SWIGLU_ALPHA = 1.702
EPS = 1e-5

ATT_Q = ATT_HEADS * ATT_HEAD_DIM
HG_W = HG_HEADS * HG_D
COL_Q = 0
COL_KV = ATT_Q
COL_HQ = COL_KV + 4 * 128
COL_HF = COL_HQ + HG_W
COL_HI = COL_HF + HG_W
COL_HG = COL_HI + HG_W
PROJ_COLS = COL_HG + HG_W

LANES = 128
SUBLANES = 8
TS = 256
VMEM_LIMIT = 56 * 1024 * 1024
HG_SAFE_LEVELS = (32, 16, 8)
HG_DIAG = 8
HG_EXP_CAP = 80.0

F32 = jnp.float32
BF16 = jnp.bfloat16
NEG_INF = float("-inf")


def _dot(a, b):
    return jnp.dot(a, b, preferred_element_type=F32)


def _dot_nt(a, b):
    return lax.dot_general(a, b, (((1,), (1,)), ((), ())), preferred_element_type=F32)


def _dot_tn(a, b):
    return lax.dot_general(a, b, (((0,), (0,)), ((), ())), preferred_element_type=F32)


def _sigmoid(x):
    return 1.0 / (1.0 + jnp.exp(-x))


def _prep_kernel(c_ref, w_ref, b_ref, o_ref):
    c = c_ref[...]
    cond = c * _sigmoid(c)
    o_ref[0] = jnp.dot(cond, w_ref[...], precision=lax.Precision.HIGHEST,
                       preferred_element_type=F32) + b_ref[...]


def _prep_call(c, w_ada, b_ada):
    B = c.shape[0]
    return pl.pallas_call(
        _prep_kernel,
        grid=(6,),
        in_specs=[
            pl.BlockSpec((B, D_MODEL), lambda j: (0, 0)),
            pl.BlockSpec((D_MODEL, D_MODEL), lambda j: (0, j)),
            pl.BlockSpec((1, D_MODEL), lambda j: (0, j)),
        ],
        out_specs=pl.BlockSpec((1, B, D_MODEL), lambda j: (j, 0, 0)),
        out_shape=jax.ShapeDtypeStruct((6, B, D_MODEL), F32),
        compiler_params=pltpu.CompilerParams(dimension_semantics=("arbitrary",)),
        name="adaln_mod",
    )(c, w_ada, b_ada)


def _tables_kernel(rb_ref, lbp_ref, bucket_ref, lb_ref, bias_ref):
    p = lbp_ref[...]
    e = jnp.exp(p - jnp.max(p, axis=0, keepdims=True))
    sm = e / jnp.sum(e, axis=0, keepdims=True)
    lb_ref[...] = sm[0:1]
    bucket = bucket_ref[...]
    for h in range(ATT_HEADS):
        acc = jnp.full(bucket.shape, NEG_INF, F32)
        for bk in range(N_BUCKETS):
            acc = jnp.where(bucket == bk, rb_ref[bk, h], acc)
        bias_ref[h] = acc


def _tables_call(rel_bias, hg_lb, bucket):
    return pl.pallas_call(
        _tables_kernel,
        in_specs=[
            pl.BlockSpec(memory_space=pltpu.SMEM),
            pl.BlockSpec(memory_space=pltpu.VMEM),
            pl.BlockSpec(memory_space=pltpu.VMEM),
        ],
        out_specs=[pl.BlockSpec(memory_space=pltpu.VMEM), pl.BlockSpec(memory_space=pltpu.VMEM)],
        out_shape=[jax.ShapeDtypeStruct((1, HG_W), F32),
                   jax.ShapeDtypeStruct((ATT_HEADS, BLOCK, 2 * BLOCK), F32)],
        name="tables",
    )(rel_bias, hg_lb, bucket)


def _bucket_table():
    i = np.arange(BLOCK, dtype=np.int32)[:, None]
    m = np.arange(2 * BLOCK, dtype=np.int32)[None, :]
    dist = i + BLOCK - m
    n = np.maximum(dist, 0)
    max_exact = N_BUCKETS // 2
    nf = np.maximum(n, 1).astype(np.float32)
    large = max_exact + (np.log(nf / np.float32(max_exact)) / np.float32(math.log(MAX_DISTANCE / max_exact))
                         * np.float32(N_BUCKETS - max_exact)).astype(np.int32)
    large = np.minimum(large, N_BUCKETS - 1)
    bucket = np.where(n < max_exact, n, large)
    return jnp.asarray(np.where((dist >= 0) & (dist < BLOCK), bucket, -1), jnp.int32)


def _rms(x):
    return x * lax.rsqrt(jnp.mean(x * x, axis=-1, keepdims=True) + EPS)


def _attention_block(proj_ref, kvprev_ref, bias_ref, sink_ref, mixed_ref, blk, first_tile):
    r0 = blk * BLOCK
    cur = proj_ref[r0:r0 + BLOCK, COL_KV:COL_KV + 512]
    if blk == 0:
        prev = kvprev_ref[...]
    else:
        prev = proj_ref[r0 - BLOCK:r0, COL_KV:COL_KV + 512]
    band = jnp.concatenate([prev, cur], axis=0)
    lane = lax.broadcasted_iota(jnp.int32, (2 * BLOCK, LANES), 1)
    lo = lane < ATT_HEAD_DIM
    col = lax.broadcasted_iota(jnp.int32, (1, 2 * BLOCK), 1)
    if blk == 0:
        pen = jnp.where((col < BLOCK) & first_tile, NEG_INF, 0.0)
    for g in range(ATT_KV_HEADS):
        kd = band[:, g * 128:(g + 1) * 128]
        vd = band[:, 256 + g * 128:256 + (g + 1) * 128]
        kb = jnp.concatenate([jnp.where(lo, kd, 0.0), jnp.where(lo, 0.0, kd)], axis=0).astype(BF16)
        vb = jnp.concatenate([jnp.where(lo, vd, 0.0), jnp.where(lo, 0.0, vd)], axis=0).astype(BF16)
        for pp in range(2):
            pair = g * 2 + pp
            qp = (proj_ref[r0:r0 + BLOCK, pair * 128:(pair + 1) * 128] * ATT_SCALE).astype(BF16)
            s = _dot_nt(qp, kb)
            ps = []
            inv = []
            for hh in range(2):
                h = pair * 2 + hh
                sh = s[:, hh * 256:(hh + 1) * 256] + bias_ref[h]
                if blk == 0:
                    sh = sh + pen
                sink = sink_ref[h]
                mx = jnp.maximum(jnp.max(sh, axis=-1, keepdims=True), sink)
                p = jnp.exp(sh - mx)
                den = jnp.sum(p, axis=-1, keepdims=True) + jnp.exp(sink - mx)
                ps.append(p.astype(BF16))
                inv.append(1.0 / den)
            o = _dot(jnp.concatenate(ps, axis=1), vb)
            lane_o = lax.broadcasted_iota(jnp.int32, (BLOCK, LANES), 1)
            o = o * jnp.where(lane_o < ATT_HEAD_DIM, inv[0], inv[1])
            mixed_ref[r0:r0 + BLOCK, pair * 128:(pair + 1) * 128] = o.astype(BF16)


def _hgrn_head(proj_ref, state_ref, b_all, kk, hgw_ref, mixed_ref, hd, masks):
    c0 = hd * HG_D
    kk_all = kk[:, c0:c0 + HG_D]
    qr = proj_ref[:, COL_HQ + c0:COL_HQ + c0 + HG_D]
    qf_all = qr * _sigmoid(qr)
    st = state_ref[hd]
    for ck in range(TS // CHUNK):
        r0 = ck * CHUNK
        b = b_all[r0:r0 + CHUNK, c0:c0 + HG_D]
        q = qf_all[r0:r0 + CHUNK]
        k = kk_all[r0:r0 + CHUNK]
        v = proj_ref[r0:r0 + CHUNK, COL_HI + c0:COL_HI + c0 + HG_D].astype(BF16)
        scores = jnp.zeros((CHUNK, CHUNK), F32)
        for m in HG_SAFE_LEVELS:
            qs, ks = [], []
            for bi in range(CHUNK // m):
                rows = slice(bi * m, (bi + 1) * m)
                if bi % 2 == 1:
                    ref = b[bi * m - 1:bi * m]
                    qs.append(q[rows] * jnp.exp(b[rows] - ref))
                    ks.append(jnp.zeros((m, HG_D), F32))
                else:
                    ref = b[(bi + 1) * m - 1:(bi + 1) * m]
                    qs.append(jnp.zeros((m, HG_D), F32))
                    ks.append(k[rows] * jnp.exp(ref - b[rows]))
            ql = jnp.concatenate(qs, axis=0).astype(BF16)
            kl = jnp.concatenate(ks, axis=0).astype(BF16)
            scores = scores + jnp.where(masks[m], _dot_nt(ql, kl), 0.0)
        qs, ks = [], []
        for bi in range(CHUNK // HG_DIAG):
            rows = slice(bi * HG_DIAG, (bi + 1) * HG_DIAG)
            c = b[rows] if bi == 0 else b[rows] - b[bi * HG_DIAG - 1:bi * HG_DIAG]
            qs.append(q[rows] * jnp.exp(c))
            ks.append(k[rows] * jnp.exp(jnp.minimum(-c, HG_EXP_CAP)))
        qd = jnp.concatenate(qs, axis=0).astype(BF16)
        kd = jnp.concatenate(ks, axis=0).astype(BF16)
        scores = scores + jnp.where(masks[0], _dot_nt(qd, kd), 0.0)
        o = _dot(scores.astype(BF16), v) + _dot_nt((q * jnp.exp(b)).astype(BF16), st.astype(BF16))
        bend = b[CHUNK - 1:CHUNK]
        kdec = (k * jnp.exp(bend - b)).astype(BF16)
        st = st * jnp.exp(bend) + _dot_tn(v, kdec)
        gate = proj_ref[r0:r0 + CHUNK, COL_HG + c0:COL_HG + c0 + HG_D]
        on = _rms(o) * hgw_ref[...] * (gate * _sigmoid(gate))
        mixed_ref[r0:r0 + CHUNK, ATT_Q + c0:ATT_Q + c0 + HG_D] = on.astype(BF16)
    state_ref[hd] = st


def _hgrn_masks():
    r = lax.broadcasted_iota(jnp.int32, (CHUNK, CHUNK), 0)
    c = lax.broadcasted_iota(jnp.int32, (CHUNK, CHUNK), 1)
    masks = {}
    for m in HG_SAFE_LEVELS:
        masks[m] = ((r // (2 * m)) == (c // (2 * m))) & (((r // m) % 2) == 1) & (((c // m) % 2) == 0)
    masks[0] = ((r // HG_DIAG) == (c // HG_DIAG)) & (c <= r)
    return masks


def _mix_kernel(sink_ref, x_ref, mod_ref, gmix_ref, win_ref, bin_ref, bias_ref, lb_ref, hgw_ref,
                wout_ref, bout_ref, gffn_ref, wr_ref, br_ref, ltri_ref, utri_ref,
                x1_ref, h2_ref, idx_ref, wts_ref, rank_ref, cnt_ref,
                proj_ref, mixed_ref, kvprev_ref, state_ref, run_ref):
    bi = pl.program_id(0)
    j = pl.program_id(1)

    @pl.when(j == 0)
    def _():
        kvprev_ref[...] = jnp.zeros_like(kvprev_ref)
        state_ref[...] = jnp.zeros_like(state_ref)

    @pl.when((j == 0) & (bi == 0))
    def _():
        run_ref[...] = jnp.zeros_like(run_ref)

    x = x_ref[0]
    mod = mod_ref[0]
    h = _rms(x) * (gmix_ref[...] * (1.0 + mod[1:2])) + mod[0:1]
    proj_ref[...] = _dot(h.astype(BF16), win_ref[...]) + bin_ref[...]

    for blk in range(TS // BLOCK):
        _attention_block(proj_ref, kvprev_ref, bias_ref, sink_ref, mixed_ref, blk, j == 0)
    kvprev_ref[...] = proj_ref[TS - BLOCK:TS, COL_KV:COL_KV + 512]

    lbv = lb_ref[...]
    fl = lbv + (1.0 - lbv) * _sigmoid(proj_ref[:, COL_HF:COL_HF + HG_W])
    g = jnp.log(fl)
    g_hi = g.astype(BF16)
    r1 = g - g_hi.astype(F32)
    g_mid = r1.astype(BF16)
    g_lo = (r1 - g_mid.astype(F32)).astype(BF16)
    ltri = ltri_ref[...]
    b_all = _dot(ltri, g_hi) + _dot(ltri, g_mid) + _dot(ltri, g_lo)

    masks = _hgrn_masks()
    for hd in range(HG_HEADS):
        _hgrn_head(proj_ref, state_ref, b_all, 1.0 - fl, hgw_ref, mixed_ref, hd, masks)

    y = _dot(mixed_ref[...], wout_ref[...]) + bout_ref[...]
    x1 = x + mod[2:3] * y
    x1_ref[0] = x1
    h2 = _rms(x1) * (gffn_ref[...] * (1.0 + mod[4:5])) + mod[3:4]
    for s in range(SUBLANES):
        h2_ref[pl.ds(s, TS, stride=SUBLANES), :] = h2[:, s * LANES:(s + 1) * LANES]

    h_hi = h2.astype(BF16)
    h_lo = (h2 - h_hi.astype(F32)).astype(BF16)
    wr = wr_ref[...]
    t1 = _dot_nt(wr, h_hi)
    t2 = _dot_nt(wr[0:N_EXPERTS], h_lo)
    logits = t1[0:N_EXPERTS] + t1[N_EXPERTS:2 * N_EXPERTS] + t2 + br_ref[:, 0:1]

    eidx = lax.broadcasted_iota(jnp.int32, (N_EXPERTS, TS), 0)
    l = logits
    vals, ohs, ids = [], [], []
    for _ in range(TOP_K):
        mx = jnp.max(l, axis=0, keepdims=True)
        ik = jnp.min(jnp.where(l == mx, eidx, N_EXPERTS), axis=0, keepdims=True)
        oh = eidx == ik
        vals.append(mx)
        ids.append(ik)
        ohs.append(oh)
        l = jnp.where(oh, NEG_INF, l)
    es = [jnp.exp(v - vals[0]) for v in vals]
    den = es[0] + es[1] + es[2] + es[3]
    run = run_ref[...]
    base = run[:, 0:1]
    utri = utri_ref[...]
    ranks = []
    for kq in range(TOP_K):
        ohf = jnp.where(ohs[kq], 1.0, 0.0)
        pref = _dot(ohf.astype(BF16), utri)
        ranks.append(jnp.sum(jnp.where(ohs[kq], base + pref, 0.0), axis=0, keepdims=True))
        base = base + jnp.sum(ohf, axis=1, keepdims=True)
    run_new = jnp.broadcast_to(base, run.shape)
    run_ref[...] = run_new
    cnt_ref[...] = run_new.astype(jnp.int32)
    idx_ref[...] = jnp.concatenate(ids, axis=0)
    wts_ref[...] = jnp.concatenate([e / den for e in es], axis=0)
    rank_ref[...] = jnp.concatenate(ranks, axis=0).astype(jnp.int32)


def _mix_call(sinks, x, mod, gmix, win, bin_, bias, lb, hgw, wout, bout, gffn, wr, br, ltri, utri):
    B, S, D = x.shape
    N = B * S
    nj = S // TS
    const2 = lambda b, j: (0, 0)
    tok = lambda b, j: (0, b * nj + j)
    in_specs = [
        pl.BlockSpec(memory_space=pltpu.SMEM),
        pl.BlockSpec((1, TS, D), lambda b, j: (b, j, 0)),
        pl.BlockSpec((1, 8, D), lambda b, j: (b, 0, 0)),
        pl.BlockSpec((1, D), const2),
        pl.BlockSpec((D, PROJ_COLS), const2),
        pl.BlockSpec((1, PROJ_COLS), const2),
        pl.BlockSpec((ATT_HEADS, BLOCK, 2 * BLOCK), lambda b, j: (0, 0, 0)),
        pl.BlockSpec((1, HG_W), const2),
        pl.BlockSpec((1, HG_D), const2),
        pl.BlockSpec((D, D), const2),
        pl.BlockSpec((1, D), const2),
        pl.BlockSpec((1, D), const2),
        pl.BlockSpec((2 * N_EXPERTS, D), const2),
        pl.BlockSpec((N_EXPERTS, LANES), const2),
        pl.BlockSpec((TS, TS), const2),
        pl.BlockSpec((TS, TS), const2),
    ]
    out_specs = [
        pl.BlockSpec((1, TS, D), lambda b, j: (b, j, 0)),
        pl.BlockSpec((TS * SUBLANES, LANES), lambda b, j: (b * nj + j, 0)),
        pl.BlockSpec((TOP_K, TS), tok),
        pl.BlockSpec((TOP_K, TS), tok),
        pl.BlockSpec((TOP_K, TS), tok),
        pl.BlockSpec((N_EXPERTS, LANES), const2),
    ]
    out_shape = [
        jax.ShapeDtypeStruct((B, S, D), F32),
        jax.ShapeDtypeStruct((N * SUBLANES, LANES), F32),
        jax.ShapeDtypeStruct((TOP_K, N), jnp.int32),
        jax.ShapeDtypeStruct((TOP_K, N), F32),
        jax.ShapeDtypeStruct((TOP_K, N), jnp.int32),
        jax.ShapeDtypeStruct((N_EXPERTS, LANES), jnp.int32),
    ]
    scratch = [
        pltpu.VMEM((TS, PROJ_COLS), F32),
        pltpu.VMEM((TS, D), BF16),
        pltpu.VMEM((BLOCK, 512), F32),
        pltpu.VMEM((HG_HEADS, HG_D, HG_D), F32),
        pltpu.VMEM((N_EXPERTS, LANES), F32),
    ]
    return pl.pallas_call(
        _mix_kernel,
        grid=(B, nj),
        in_specs=in_specs,
        out_specs=out_specs,
        out_shape=out_shape,
        scratch_shapes=scratch,
        compiler_params=pltpu.CompilerParams(
            dimension_semantics=("arbitrary", "arbitrary"), vmem_limit_bytes=VMEM_LIMIT),
        name="mix_router",
    )(sinks, x, mod, gmix, win, bin_, bias, lb, hgw, wout, bout, gffn, wr, br, ltri, utri)


TM = 256
TD = 256
ROW = SUBLANES


def _row_copy(src, src_row, dst, dst_row, sem):
    return pltpu.make_async_copy(src.at[pl.ds(pl.multiple_of(src_row * ROW, ROW), ROW)],
                                 dst.at[pl.ds(pl.multiple_of(dst_row * ROW, ROW), ROW)], sem)


def _col_from_row(w_row, n):
    rr = lax.broadcasted_iota(jnp.int32, (n, n), 0)
    cc = lax.broadcasted_iota(jnp.int32, (n, n), 1)
    return jnp.sum(jnp.where(rr == cc, w_row, 0.0), axis=1, keepdims=True)


def _rows_to_matrix(ref, base, n):
    return jnp.concatenate([ref[pl.ds(base + s, n, stride=ROW), :] for s in range(ROW)], axis=1)


def _dispatch_kernel(tend_ref, pos_ref, h2_hbm, xs_hbm, zbuf, hbuf, sem, lsem, zsem):
    i = pl.program_id(0)
    n = pl.num_programs(0)
    n_tiles = xs_hbm.shape[0] // (TM * ROW)

    def fill_tile(tile):
        return pltpu.make_async_copy(
            zbuf, xs_hbm.at[pl.ds(pl.multiple_of(tile * (TM * ROW), TM * ROW), TM * ROW)], zsem.at[0])

    def padded(e):
        return tend_ref[e] > (tend_ref[e - 1] if e > 0 else 0), tend_ref[e] - 1

    def unused(u):
        tile = tend_ref[N_EXPERTS - 1] + u
        return tile < n_tiles, tile

    @pl.when(i == 0)
    def _():
        zbuf[...] = jnp.zeros_like(zbuf)
        sites = [padded(e) for e in range(N_EXPERTS)] + [unused(u) for u in range(N_EXPERTS)]
        for cond, tile in sites:
            pl.when(cond)(lambda tile=tile: fill_tile(tile).start())
        for cond, tile in sites:
            pl.when(cond)(lambda tile=tile: fill_tile(tile).wait())

    def load(step):
        s = step % 3
        return pltpu.make_async_copy(
            h2_hbm.at[pl.ds(pl.multiple_of(step * (TD * ROW), TD * ROW), TD * ROW)], hbuf.at[s], lsem.at[s])

    def wait_rows(step):
        s = step % 3
        for _ in range(TOP_K):
            pltpu.make_async_copy(hbuf.at[s], xs_hbm.at[pl.ds(0, TD * ROW)], sem.at[s]).wait()

    pl.when(i == 0)(lambda: load(i).start())
    pl.when(i >= 2)(lambda: wait_rows(i - 2))
    pl.when(i + 1 < n)(lambda: load(i + 1).start())
    load(i).wait()
    slot = i % 3
    for q in range(TOP_K * TD):
        _row_copy(hbuf.at[slot], q % TD, xs_hbm, pos_ref[0, 0, q], sem.at[slot]).start()

    @pl.when(i == n - 1)
    def _():
        pl.when(i >= 1)(lambda: wait_rows(i - 1))
        wait_rows(i)


def _dispatch_call(tile_end, pos_t, h2rows, n_tiles):
    nblk = pos_t.shape[0]
    grid_spec = pltpu.PrefetchScalarGridSpec(
        num_scalar_prefetch=1,
        grid=(nblk,),
        in_specs=[
            pl.BlockSpec((1, 1, TOP_K * TD), lambda i, tend: (i, 0, 0), memory_space=pltpu.SMEM),
            pl.BlockSpec(memory_space=pl.ANY),
        ],
        out_specs=pl.BlockSpec(memory_space=pl.ANY),
        scratch_shapes=[
            pltpu.VMEM((TM * ROW, LANES), F32),
            pltpu.VMEM((3, TD * ROW, LANES), F32),
            pltpu.SemaphoreType.DMA((3,)),
            pltpu.SemaphoreType.DMA((3,)),
            pltpu.SemaphoreType.DMA((1,)),
        ],
    )
    return pl.pallas_call(
        _dispatch_kernel,
        grid_spec=grid_spec,
        out_shape=jax.ShapeDtypeStruct((n_tiles * TM * ROW, LANES), F32),
        compiler_params=pltpu.CompilerParams(dimension_semantics=("arbitrary",)),
        name="moe_dispatch",
    )(tile_end, pos_t, h2rows)


def _moe_kernel(te_ref, nt_ref, xs_ref, wgu_ref, bgu_ref, wd_ref, bd_ref, perm_ref, ys_ref, wgu_bf, wd_bf):
    j = pl.program_id(0)
    nt = nt_ref[0]

    @pl.when(j < nt)
    def _():
        e = te_ref[j]
        first = (j == 0) | (te_ref[jnp.maximum(j - 1, 0)] != e)

        @pl.when(first)
        def _():
            perm = perm_ref[...]
            for g in range(2 * D_FF // 256):
                w = wgu_ref[0, :, g * 256:(g + 1) * 256].astype(BF16)
                wgu_bf[:, g * 256:(g + 1) * 256] = _dot(w, perm).astype(BF16)
            wd_bf[...] = wd_ref[0].astype(BF16)

        h = _rows_to_matrix(xs_ref, 0, TM).astype(BF16)
        gu = _dot(h, wgu_bf[...]) + bgu_ref[0]
        acts = []
        for g in range(D_FF // LANES):
            gate = jnp.minimum(gu[:, g * 256:g * 256 + 128], SWIGLU_LIMIT)
            up = jnp.clip(gu[:, g * 256 + 128:(g + 1) * 256], -SWIGLU_LIMIT, SWIGLU_LIMIT)
            acts.append(((up + 1.0) * gate * _sigmoid(SWIGLU_ALPHA * gate)).astype(BF16))
        y = _dot(jnp.concatenate(acts, axis=1), wd_bf[...]) + bd_ref[0]
        for s in range(ROW):
            ys_ref[pl.ds(s, TM, stride=ROW), :] = y[:, s * LANES:(s + 1) * LANES]

    @pl.when(j >= nt)
    def _():
        ys_ref[...] = jnp.zeros_like(ys_ref)


def _moe_call(te, nt, xs, wgu, bgu, wd, bd, perm):
    n_tiles = xs.shape[0] // (TM * ROW)
    tile = lambda j, te, nt: (jnp.minimum(j, nt[0] - 1), 0)
    exp = lambda j, te, nt: (te[jnp.minimum(j, nt[0] - 1)], 0, 0)
    grid_spec = pltpu.PrefetchScalarGridSpec(
        num_scalar_prefetch=2,
        grid=(n_tiles,),
        in_specs=[
            pl.BlockSpec((TM * ROW, LANES), tile),
            pl.BlockSpec((1, D_MODEL, 2 * D_FF), exp),
            pl.BlockSpec((1, 1, 2 * D_FF), exp),
            pl.BlockSpec((1, D_FF, D_MODEL), exp),
            pl.BlockSpec((1, 1, D_MODEL), exp),
            pl.BlockSpec((256, 256), lambda j, te, nt: (0, 0)),
        ],
        out_specs=pl.BlockSpec((TM * ROW, LANES), lambda j, te, nt: (j, 0)),
        scratch_shapes=[
            pltpu.VMEM((D_MODEL, 2 * D_FF), BF16),
            pltpu.VMEM((D_FF, D_MODEL), BF16),
        ],
    )
    return pl.pallas_call(
        _moe_kernel,
        grid_spec=grid_spec,
        out_shape=jax.ShapeDtypeStruct(xs.shape, F32),
        compiler_params=pltpu.CompilerParams(
            dimension_semantics=("arbitrary",), vmem_limit_bytes=VMEM_LIMIT),
        name="moe_experts",
    )(te, nt, xs, wgu, bgu, wd, bd, perm)


def _final_kernel(pnext_ref, pcur_ref, wts_ref, x1_ref, mod_ref, gfin_ref, ys_hbm, out_ref, gbuf, sem):
    i = pl.program_id(0)
    n = pl.num_programs(0)

    def issue(pref, slot):
        for q in range(TOP_K * TD):
            _row_copy(ys_hbm, pref[0, 0, q], gbuf.at[slot], q, sem.at[slot]).start()

    pl.when(i == 0)(lambda: issue(pcur_ref, 0))
    pl.when(i + 1 < n)(lambda: issue(pnext_ref, (i + 1) % 2))
    slot = i % 2
    pltpu.make_async_copy(ys_hbm.at[pl.ds(0, TOP_K * TD * ROW)], gbuf.at[slot], sem.at[slot]).wait()
    gb = gbuf.at[slot]
    acc = jnp.zeros((TD, D_MODEL), F32)
    for kq in range(TOP_K):
        v = _rows_to_matrix(gb, kq * TD * ROW, TD)
        acc = acc + _col_from_row(wts_ref[kq:kq + 1, :], TD) * v
    xf = x1_ref[...] + mod_ref[0][5:6] * acc
    out_ref[...] = _rms(xf) * gfin_ref[...]


def _final_call(pos_t, wts, x1, mod, gfin, ys, S):
    N, D = x1.shape
    nblk = N // TD
    per_b = S // TD
    return pl.pallas_call(
        _final_kernel,
        grid=(nblk,),
        in_specs=[
            pl.BlockSpec((1, 1, TOP_K * TD), lambda i: (jnp.minimum(i + 1, nblk - 1), 0, 0),
                         memory_space=pltpu.SMEM),
            pl.BlockSpec((1, 1, TOP_K * TD), lambda i: (i, 0, 0), memory_space=pltpu.SMEM),
            pl.BlockSpec((TOP_K, TD), lambda i: (0, i)),
            pl.BlockSpec((TD, D), lambda i: (i, 0)),
            pl.BlockSpec((1, 8, D), lambda i: (i // per_b, 0, 0)),
            pl.BlockSpec((1, D), lambda i: (0, 0)),
            pl.BlockSpec(memory_space=pl.ANY),
        ],
        out_specs=pl.BlockSpec((TD, D), lambda i: (i, 0)),
        out_shape=jax.ShapeDtypeStruct((N, D), F32),
        scratch_shapes=[
            pltpu.VMEM((2, TOP_K * TD * ROW, LANES), F32),
            pltpu.SemaphoreType.DMA((2,)),
        ],
        compiler_params=pltpu.CompilerParams(
            dimension_semantics=("arbitrary",), vmem_limit_bytes=VMEM_LIMIT),
        name="combine_norm",
    )(pos_t, pos_t, wts, x1, mod, gfin, ys)


def _route_plan(idx, rank, cnt, N):
    n_tiles = N * TOP_K // TM + N_EXPERTS
    counts = cnt[:, 0]
    ntile = (counts + TM - 1) // TM
    tile_end = jnp.cumsum(ntile).astype(jnp.int32)
    row_start = (tile_end - ntile) * TM
    nt = tile_end[-1:]
    experts = jnp.arange(N_EXPERTS, dtype=jnp.int32)
    pos = jnp.sum(jnp.where(idx[..., None] == experts, row_start, 0), axis=-1) + rank
    te = jnp.minimum(jnp.sum(jnp.arange(n_tiles)[:, None] >= tile_end[None, :], axis=1),
                     N_EXPERTS - 1).astype(jnp.int32)
    pos_t = jnp.transpose(pos.reshape(TOP_K, N // TD, TD), (1, 0, 2)).reshape(N // TD, 1, TOP_K * TD)
    return te, nt, tile_end, pos_t, n_tiles


def _tri_constants():
    r = np.arange(TS)[:, None]
    c = np.arange(TS)[None, :]
    ltri = ((r // CHUNK) == (c // CHUNK)) & (c <= r)
    utri = r < c
    return jnp.asarray(ltri, BF16), jnp.asarray(utri, BF16)


def kernel(x, c, w_ada, b_ada, g_mix, w_in, b_in, attn_sinks, rel_bias, hg_lb, hg_norm_w, w_out, b_out, g_ffn, w_router, b_router, w_gate_up, b_gate_up, w_down, b_down, g_final):
    B, S, D = x.shape
    N = B * S
    mod6 = _prep_call(c, w_ada[0], b_ada)
    mod = jnp.pad(jnp.transpose(mod6, (1, 0, 2)), ((0, 0), (0, 2), (0, 0)))
    lb, bias = _tables_call(rel_bias, hg_lb, _bucket_table())

    wi, bi_ = w_in[0], b_in[0]
    def cols(a):
        aq, ak, av, rest = a[..., :512], a[..., 512:640], a[..., 640:768], a[..., 768:]
        k0, k1, v0, v1 = ak[..., :64], ak[..., 64:], av[..., :64], av[..., 64:]
        return jnp.concatenate([aq, k0, k0, k1, k1, v0, v0, v1, v1, rest], axis=-1)
    win = cols(wi).astype(BF16)
    bin_ = cols(bi_)[None, :]
    wrt = jnp.transpose(w_router[0])
    wr_hi = wrt.astype(BF16)
    wr_lo = (wrt - wr_hi.astype(F32)).astype(BF16)
    wr = jnp.concatenate([wr_hi, wr_lo], axis=0)
    br = jnp.broadcast_to(b_router[0][:, None], (N_EXPERTS, LANES))
    ltri, utri = _tri_constants()

    x1, h2rows, idx, wts, rank, cnt = _mix_call(
        attn_sinks[0], x, mod, g_mix, win, bin_, bias, lb, hg_norm_w, w_out[0].astype(BF16),
        b_out, g_ffn, wr, br, ltri, utri)

    te, nt, tile_end, pos_t, n_tiles = _route_plan(idx, rank, cnt, N)
    ii = np.arange(128)
    perm_np = np.zeros((256, 256), np.float32)
    perm_np[2 * ii, ii] = 1.0
    perm_np[2 * ii + 1, 128 + ii] = 1.0
    perm = jnp.asarray(perm_np, BF16)
    bgu = b_gate_up[0].reshape(N_EXPERTS, D_FF // LANES, LANES, 2)
    bgu = jnp.transpose(bgu, (0, 1, 3, 2)).reshape(N_EXPERTS, 1, 2 * D_FF)
    xs = _dispatch_call(tile_end, pos_t, h2rows, n_tiles)
    ys = _moe_call(te, nt, xs, w_gate_up[0], bgu, w_down[0], b_down[0][:, None, :], perm)
    out = _final_call(pos_t, wts, x1.reshape(N, D), mod, g_final[None, :], ys, S)
    return out.reshape(B, S, D)
```

```python
import functools
import math

import numpy as np
import jax
import jax.numpy as jnp
from jax import lax
from jax.experimental import pallas as pl
from jax.experimental.pallas import tpu as pltpu

D_MODEL = 1024
ATT_HEAD_DIM = 64
ATT_HEADS = 8
ATT_KV_HEADS = 2
ATT_SCALE = ATT_HEAD_DIM ** -0.5
BLOCK = 128
N_BUCKETS = 32
MAX_DISTANCE = 128
HG_HEADS = 4
HG_D = 128
CHUNK = 64
N_EXPERTS = 32
TOP_K = 4
D_FF = 1024
SWIGLU_LIMIT = 7.0
SWIGLU_ALPHA = 1.702
EPS = 1e-5

ATT_Q = ATT_HEADS * ATT_HEAD_DIM
HG_W = HG_HEADS * HG_D
COL_Q = 0
COL_KV = ATT_Q
COL_HQ = COL_KV + 4 * 128
COL_HF = COL_HQ + HG_W
COL_HI = COL_HF + HG_W
COL_HG = COL_HI + HG_W
PROJ_COLS = COL_HG + HG_W

LANES = 128
SUBLANES = 8
TS = 256
VMEM_LIMIT = 56 * 1024 * 1024
HG_SAFE_LEVELS = (32, 16, 8)
HG_DIAG = 8
HG_EXP_CAP = 80.0

F32 = jnp.float32
BF16 = jnp.bfloat16
NEG_INF = float("-inf")


def _dot(a, b):
    return jnp.dot(a, b, preferred_element_type=F32)


def _dot_nt(a, b):
    return lax.dot_general(a, b, (((1,), (1,)), ((), ())), preferred_element_type=F32)


def _dot_tn(a, b):
    return lax.dot_general(a, b, (((0,), (0,)), ((), ())), preferred_element_type=F32)


def _sigmoid(x):
    return 1.0 / (1.0 + jnp.exp(-x))


def _prep_kernel(c_ref, w_ref, b_ref, o_ref):
    c = c_ref[...]
    cond = c * _sigmoid(c)
    o_ref[0] = jnp.dot(cond, w_ref[...], precision=lax.Precision.HIGHEST,
                       preferred_element_type=F32) + b_ref[...]


def _prep_call(c, w_ada, b_ada):
    B = c.shape[0]
    return pl.pallas_call(
        _prep_kernel,
        grid=(6,),
        in_specs=[
            pl.BlockSpec((B, D_MODEL), lambda j: (0, 0)),
            pl.BlockSpec((D_MODEL, D_MODEL), lambda j: (0, j)),
            pl.BlockSpec((1, D_MODEL), lambda j: (0, j)),
        ],
        out_specs=pl.BlockSpec((1, B, D_MODEL), lambda j: (j, 0, 0)),
        out_shape=jax.ShapeDtypeStruct((6, B, D_MODEL), F32),
        compiler_params=pltpu.CompilerParams(dimension_semantics=("arbitrary",)),
        name="adaln_mod",
    )(c, w_ada, b_ada)


def _tables_kernel(rb_ref, lbp_ref, bucket_ref, lb_ref, bias_ref):
    p = lbp_ref[...]
    e = jnp.exp(p - jnp.max(p, axis=0, keepdims=True))
    sm = e / jnp.sum(e, axis=0, keepdims=True)
    lb_ref[...] = sm[0:1]
    bucket = bucket_ref[...]
    for h in range(ATT_HEADS):
        acc = jnp.full(bucket.shape, NEG_INF, F32)
        for bk in range(N_BUCKETS):
            acc = jnp.where(bucket == bk, rb_ref[bk, h], acc)
        bias_ref[h] = acc


def _tables_call(rel_bias, hg_lb, bucket):
    return pl.pallas_call(
        _tables_kernel,
        in_specs=[
            pl.BlockSpec(memory_space=pltpu.SMEM),
            pl.BlockSpec(memory_space=pltpu.VMEM),
            pl.BlockSpec(memory_space=pltpu.VMEM),
        ],
        out_specs=[pl.BlockSpec(memory_space=pltpu.VMEM), pl.BlockSpec(memory_space=pltpu.VMEM)],
        out_shape=[jax.ShapeDtypeStruct((1, HG_W), F32),
                   jax.ShapeDtypeStruct((ATT_HEADS, BLOCK, 2 * BLOCK), F32)],
        name="tables",
    )(rel_bias, hg_lb, bucket)


def _bucket_table():
    i = np.arange(BLOCK, dtype=np.int32)[:, None]
    m = np.arange(2 * BLOCK, dtype=np.int32)[None, :]
    dist = i + BLOCK - m
    n = np.maximum(dist, 0)
    max_exact = N_BUCKETS // 2
    nf = np.maximum(n, 1).astype(np.float32)
    large = max_exact + (np.log(nf / np.float32(max_exact)) / np.float32(math.log(MAX_DISTANCE / max_exact))
                         * np.float32(N_BUCKETS - max_exact)).astype(np.int32)
    large = np.minimum(large, N_BUCKETS - 1)
    bucket = np.where(n < max_exact, n, large)
    return jnp.asarray(np.where((dist >= 0) & (dist < BLOCK), bucket, -1), jnp.int32)


def _rms(x):
    return x * lax.rsqrt(jnp.mean(x * x, axis=-1, keepdims=True) + EPS)


def _attention_block(proj_ref, kvprev_ref, bias_ref, sink_ref, mixed_ref, blk, first_tile):
    r0 = blk * BLOCK
    cur = proj_ref[r0:r0 + BLOCK, COL_KV:COL_KV + 512]
    if blk == 0:
        prev = kvprev_ref[...]
    else:
        prev = proj_ref[r0 - BLOCK:r0, COL_KV:COL_KV + 512]
    band = jnp.concatenate([prev, cur], axis=0)
    lane = lax.broadcasted_iota(jnp.int32, (2 * BLOCK, LANES), 1)
    lo = lane < ATT_HEAD_DIM
    col = lax.broadcasted_iota(jnp.int32, (1, 2 * BLOCK), 1)
    if blk == 0:
        pen = jnp.where((col < BLOCK) & first_tile, NEG_INF, 0.0)
    for g in range(ATT_KV_HEADS):
        kd = band[:, g * 128:(g + 1) * 128]
        vd = band[:, 256 + g * 128:256 + (g + 1) * 128]
        kb = jnp.concatenate([jnp.where(lo, kd, 0.0), jnp.where(lo, 0.0, kd)], axis=0).astype(BF16)
        vb = jnp.concatenate([jnp.where(lo, vd, 0.0), jnp.where(lo, 0.0, vd)], axis=0).astype(BF16)
        for pp in range(2):
            pair = g * 2 + pp
            qp = (proj_ref[r0:r0 + BLOCK, pair * 128:(pair + 1) * 128] * ATT_SCALE).astype(BF16)
            s = _dot_nt(qp, kb)
            ps = []
            inv = []
            for hh in range(2):
                h = pair * 2 + hh
                sh = s[:, hh * 256:(hh + 1) * 256] + bias_ref[h]
                if blk == 0:
                    sh = sh + pen
                sink = sink_ref[h]
                mx = jnp.maximum(jnp.max(sh, axis=-1, keepdims=True), sink)
                p = jnp.exp(sh - mx)
                den = jnp.sum(p, axis=-1, keepdims=True) + jnp.exp(sink - mx)
                ps.append(p.astype(BF16))
                inv.append(1.0 / den)
            o = _dot(jnp.concatenate(ps, axis=1), vb)
            lane_o = lax.broadcasted_iota(jnp.int32, (BLOCK, LANES), 1)
            o = o * jnp.where(lane_o < ATT_HEAD_DIM, inv[0], inv[1])
            mixed_ref[r0:r0 + BLOCK, pair * 128:(pair + 1) * 128] = o.astype(BF16)


def _hgrn_head(proj_ref, state_ref, b_all, kk, hgw_ref, mixed_ref, hd, masks):
    c0 = hd * HG_D
    kk_all = kk[:, c0:c0 + HG_D]
    qr = proj_ref[:, COL_HQ + c0:COL_HQ + c0 + HG_D]
    qf_all = qr * _sigmoid(qr)
    st = state_ref[hd]
    for ck in range(TS // CHUNK):
        r0 = ck * CHUNK
        b = b_all[r0:r0 + CHUNK, c0:c0 + HG_D]
        q = qf_all[r0:r0 + CHUNK]
        k = kk_all[r0:r0 + CHUNK]
        v = proj_ref[r0:r0 + CHUNK, COL_HI + c0:COL_HI + c0 + HG_D].astype(BF16)
        scores = jnp.zeros((CHUNK, CHUNK), F32)
        for m in HG_SAFE_LEVELS:
            qs, ks = [], []
            for bi in range(CHUNK // m):
                rows = slice(bi * m, (bi + 1) * m)
                if bi % 2 == 1:
                    ref = b[bi * m - 1:bi * m]
                    qs.append(q[rows] * jnp.exp(b[rows] - ref))
                    ks.append(jnp.zeros((m, HG_D), F32))
                else:
                    ref = b[(bi + 1) * m - 1:(bi + 1) * m]
                    qs.append(jnp.zeros((m, HG_D), F32))
                    ks.append(k[rows] * jnp.exp(ref - b[rows]))
            ql = jnp.concatenate(qs, axis=0).astype(BF16)
            kl = jnp.concatenate(ks, axis=0).astype(BF16)
            scores = scores + jnp.where(masks[m], _dot_nt(ql, kl), 0.0)
        qs, ks = [], []
        for bi in range(CHUNK // HG_DIAG):
            rows = slice(bi * HG_DIAG, (bi + 1) * HG_DIAG)
            c = b[rows] if bi == 0 else b[rows] - b[bi * HG_DIAG - 1:bi * HG_DIAG]
            qs.append(q[rows] * jnp.exp(c))
            ks.append(k[rows] * jnp.exp(jnp.minimum(-c, HG_EXP_CAP)))
        qd = jnp.concatenate(qs, axis=0).astype(BF16)
        kd = jnp.concatenate(ks, axis=0).astype(BF16)
        scores = scores + jnp.where(masks[0], _dot_nt(qd, kd), 0.0)
        o = _dot(scores.astype(BF16), v) + _dot_nt((q * jnp.exp(b)).astype(BF16), st.astype(BF16))
        bend = b[CHUNK - 1:CHUNK]
        kdec = (k * jnp.exp(bend - b)).astype(BF16)
        st = st * jnp.exp(bend) + _dot_tn(v, kdec)
        gate = proj_ref[r0:r0 + CHUNK, COL_HG + c0:COL_HG + c0 + HG_D]
        on = _rms(o) * hgw_ref[...] * (gate * _sigmoid(gate))
        mixed_ref[r0:r0 + CHUNK, ATT_Q + c0:ATT_Q + c0 + HG_D] = on.astype(BF16)
    state_ref[hd] = st


def _hgrn_masks():
    r = lax.broadcasted_iota(jnp.int32, (CHUNK, CHUNK), 0)
    c = lax.broadcasted_iota(jnp.int32, (CHUNK, CHUNK), 1)
    masks = {}
    for m in HG_SAFE_LEVELS:
        masks[m] = ((r // (2 * m)) == (c // (2 * m))) & (((r // m) % 2) == 1) & (((c // m) % 2) == 0)
    masks[0] = ((r // HG_DIAG) == (c // HG_DIAG)) & (c <= r)
    return masks


def _mix_kernel(sink_ref, x_ref, mod_ref, gmix_ref, win_ref, bin_ref, bias_ref, lb_ref, hgw_ref,
                wout_ref, bout_ref, gffn_ref, wr_ref, br_ref, ltri_ref, utri_ref,
                x1_ref, h2_ref, idx_ref, wts_ref, rank_ref, cnt_ref,
                proj_ref, mixed_ref, kvprev_ref, state_ref, run_ref):
    bi = pl.program_id(0)
    j = pl.program_id(1)

    @pl.when(j == 0)
    def _():
        kvprev_ref[...] = jnp.zeros_like(kvprev_ref)
        state_ref[...] = jnp.zeros_like(state_ref)

    @pl.when((j == 0) & (bi == 0))
    def _():
        run_ref[...] = jnp.zeros_like(run_ref)

    x = x_ref[0]
    mod = mod_ref[0]
    h = _rms(x) * (gmix_ref[...] * (1.0 + mod[1:2])) + mod[0:1]
    proj_ref[...] = _dot(h.astype(BF16), win_ref[...]) + bin_ref[...]

    for blk in range(TS // BLOCK):
        _attention_block(proj_ref, kvprev_ref, bias_ref, sink_ref, mixed_ref, blk, j == 0)
    kvprev_ref[...] = proj_ref[TS - BLOCK:TS, COL_KV:COL_KV + 512]

    lbv = lb_ref[...]
    fl = lbv + (1.0 - lbv) * _sigmoid(proj_ref[:, COL_HF:COL_HF + HG_W])
    g = jnp.log(fl)
    g_hi = g.astype(BF16)
    r1 = g - g_hi.astype(F32)
    g_mid = r1.astype(BF16)
    g_lo = (r1 - g_mid.astype(F32)).astype(BF16)
    ltri = ltri_ref[...]
    b_all = _dot(ltri, g_hi) + _dot(ltri, g_mid) + _dot(ltri, g_lo)

    masks = _hgrn_masks()
    for hd in range(HG_HEADS):
        _hgrn_head(proj_ref, state_ref, b_all, 1.0 - fl, hgw_ref, mixed_ref, hd, masks)

    y = _dot(mixed_ref[...], wout_ref[...]) + bout_ref[...]
    x1 = x + mod[2:3] * y
    x1_ref[0] = x1
    h2 = _rms(x1) * (gffn_ref[...] * (1.0 + mod[4:5])) + mod[3:4]
    for s in range(SUBLANES):
        h2_ref[pl.ds(s, TS, stride=SUBLANES), :] = h2[:, s * LANES:(s + 1) * LANES]

    h_hi = h2.astype(BF16)
    h_lo = (h2 - h_hi.astype(F32)).astype(BF16)
    wr = wr_ref[...]
    t1 = _dot_nt(wr, h_hi)
    t2 = _dot_nt(wr[0:N_EXPERTS], h_lo)
    logits = t1[0:N_EXPERTS] + t1[N_EXPERTS:2 * N_EXPERTS] + t2 + br_ref[:, 0:1]

    eidx = lax.broadcasted_iota(jnp.int32, (N_EXPERTS, TS), 0)
    l = logits
    vals, ohs, ids = [], [], []
    for _ in range(TOP_K):
        mx = jnp.max(l, axis=0, keepdims=True)
        ik = jnp.min(jnp.where(l == mx, eidx, N_EXPERTS), axis=0, keepdims=True)
        oh = eidx == ik
        vals.append(mx)
        ids.append(ik)
        ohs.append(oh)
        l = jnp.where(oh, NEG_INF, l)
    es = [jnp.exp(v - vals[0]) for v in vals]
    den = es[0] + es[1] + es[2] + es[3]
    run = run_ref[...]
    base = run[:, 0:1]
    utri = utri_ref[...]
    ranks = []
    for kq in range(TOP_K):
        ohf = jnp.where(ohs[kq], 1.0, 0.0)
        pref = _dot(ohf.astype(BF16), utri)
        ranks.append(jnp.sum(jnp.where(ohs[kq], base + pref, 0.0), axis=0, keepdims=True))
        base = base + jnp.sum(ohf, axis=1, keepdims=True)
    run_new = jnp.broadcast_to(base, run.shape)
    run_ref[...] = run_new
    cnt_ref[...] = run_new.astype(jnp.int32)
    idx_ref[...] = jnp.concatenate(ids, axis=0)
    wts_ref[...] = jnp.concatenate([e / den for e in es], axis=0)
    rank_ref[...] = jnp.concatenate(ranks, axis=0).astype(jnp.int32)


def _mix_call(sinks, x, mod, gmix, win, bin_, bias, lb, hgw, wout, bout, gffn, wr, br, ltri, utri):
    B, S, D = x.shape
    N = B * S
    nj = S // TS
    const2 = lambda b, j: (0, 0)
    tok = lambda b, j: (0, b * nj + j)
    in_specs = [
        pl.BlockSpec(memory_space=pltpu.SMEM),
        pl.BlockSpec((1, TS, D), lambda b, j: (b, j, 0)),
        pl.BlockSpec((1, 8, D), lambda b, j: (b, 0, 0)),
        pl.BlockSpec((1, D), const2),
        pl.BlockSpec((D, PROJ_COLS), const2),
        pl.BlockSpec((1, PROJ_COLS), const2),
        pl.BlockSpec((ATT_HEADS, BLOCK, 2 * BLOCK), lambda b, j: (0, 0, 0)),
        pl.BlockSpec((1, HG_W), const2),
        pl.BlockSpec((1, HG_D), const2),
        pl.BlockSpec((D, D), const2),
        pl.BlockSpec((1, D), const2),
        pl.BlockSpec((1, D), const2),
        pl.BlockSpec((2 * N_EXPERTS, D), const2),
        pl.BlockSpec((N_EXPERTS, LANES), const2),
        pl.BlockSpec((TS, TS), const2),
        pl.BlockSpec((TS, TS), const2),
    ]
    out_specs = [
        pl.BlockSpec((1, TS, D), lambda b, j: (b, j, 0)),
        pl.BlockSpec((TS * SUBLANES, LANES), lambda b, j: (b * nj + j, 0)),
        pl.BlockSpec((TOP_K, TS), tok),
        pl.BlockSpec((TOP_K, TS), tok),
        pl.BlockSpec((TOP_K, TS), tok),
        pl.BlockSpec((N_EXPERTS, LANES), const2),
    ]
    out_shape = [
        jax.ShapeDtypeStruct((B, S, D), F32),
        jax.ShapeDtypeStruct((N * SUBLANES, LANES), F32),
        jax.ShapeDtypeStruct((TOP_K, N), jnp.int32),
        jax.ShapeDtypeStruct((TOP_K, N), F32),
        jax.ShapeDtypeStruct((TOP_K, N), jnp.int32),
        jax.ShapeDtypeStruct((N_EXPERTS, LANES), jnp.int32),
    ]
    scratch = [
        pltpu.VMEM((TS, PROJ_COLS), F32),
        pltpu.VMEM((TS, D), BF16),
        pltpu.VMEM((BLOCK, 512), F32),
        pltpu.VMEM((HG_HEADS, HG_D, HG_D), F32),
        pltpu.VMEM((N_EXPERTS, LANES), F32),
    ]
    return pl.pallas_call(
        _mix_kernel,
        grid=(B, nj),
        in_specs=in_specs,
        out_specs=out_specs,
        out_shape=out_shape,
        scratch_shapes=scratch,
        compiler_params=pltpu.CompilerParams(
            dimension_semantics=("arbitrary", "arbitrary"), vmem_limit_bytes=VMEM_LIMIT),
        name="mix_router",
    )(sinks, x, mod, gmix, win, bin_, bias, lb, hgw, wout, bout, gffn, wr, br, ltri, utri)


TM = 256
TD = 256
ROW = SUBLANES


def _row_copy(src, src_row, dst, dst_row, sem):
    return pltpu.make_async_copy(src.at[pl.ds(pl.multiple_of(src_row * ROW, ROW), ROW)],
                                 dst.at[pl.ds(pl.multiple_of(dst_row * ROW, ROW), ROW)], sem)


def _col_from_row(w_row, n):
    rr = lax.broadcasted_iota(jnp.int32, (n, n), 0)
    cc = lax.broadcasted_iota(jnp.int32, (n, n), 1)
    return jnp.sum(jnp.where(rr == cc, w_row, 0.0), axis=1, keepdims=True)


def _rows_to_matrix(ref, base, n):
    return jnp.concatenate([ref[pl.ds(base + s, n, stride=ROW), :] for s in range(ROW)], axis=1)


def _dispatch_kernel(tend_ref, pos_ref, h2_hbm, xs_hbm, zbuf, hbuf, sem, lsem, zsem):
    i = pl.program_id(0)
    n = pl.num_programs(0)
    n_tiles = xs_hbm.shape[0] // (TM * ROW)

    def fill_tile(tile):
        return pltpu.make_async_copy(
            zbuf, xs_hbm.at[pl.ds(pl.multiple_of(tile * (TM * ROW), TM * ROW), TM * ROW)], zsem.at[0])

    def padded(e):
        return tend_ref[e] > (tend_ref[e - 1] if e > 0 else 0), tend_ref[e] - 1

    def unused(u):
        tile = tend_ref[N_EXPERTS - 1] + u
        return tile < n_tiles, tile

    @pl.when(i == 0)
    def _():
        zbuf[...] = jnp.zeros_like(zbuf)
        sites = [padded(e) for e in range(N_EXPERTS)] + [unused(u) for u in range(N_EXPERTS)]
        for cond, tile in sites:
            pl.when(cond)(lambda tile=tile: fill_tile(tile).start())
        for cond, tile in sites:
            pl.when(cond)(lambda tile=tile: fill_tile(tile).wait())

    def load(step):
        s = step % 3
        return pltpu.make_async_copy(
            h2_hbm.at[pl.ds(pl.multiple_of(step * (TD * ROW), TD * ROW), TD * ROW)], hbuf.at[s], lsem.at[s])

    def wait_rows(step):
        s = step % 3
        for _ in range(TOP_K):
            pltpu.make_async_copy(hbuf.at[s], xs_hbm.at[pl.ds(0, TD * ROW)], sem.at[s]).wait()

    pl.when(i == 0)(lambda: load(i).start())
    pl.when(i >= 2)(lambda: wait_rows(i - 2))
    pl.when(i + 1 < n)(lambda: load(i + 1).start())
    load(i).wait()
    slot = i % 3
    for q in range(TOP_K * TD):
        _row_copy(hbuf.at[slot], q % TD, xs_hbm, pos_ref[0, 0, q], sem.at[slot]).start(priority=q % 2)

    @pl.when(i == n - 1)
    def _():
        pl.when(i >= 1)(lambda: wait_rows(i - 1))
        wait_rows(i)


def _dispatch_call(tile_end, pos_t, h2rows, n_tiles):
    nblk = pos_t.shape[0]
    grid_spec = pltpu.PrefetchScalarGridSpec(
        num_scalar_prefetch=1,
        grid=(nblk,),
        in_specs=[
            pl.BlockSpec((1, 1, TOP_K * TD), lambda i, tend: (i, 0, 0), memory_space=pltpu.SMEM),
            pl.BlockSpec(memory_space=pl.ANY),
        ],
        out_specs=pl.BlockSpec(memory_space=pl.ANY),
        scratch_shapes=[
            pltpu.VMEM((TM * ROW, LANES), F32),
            pltpu.VMEM((3, TD * ROW, LANES), F32),
            pltpu.SemaphoreType.DMA((3,)),
            pltpu.SemaphoreType.DMA((3,)),
            pltpu.SemaphoreType.DMA((1,)),
        ],
    )
    return pl.pallas_call(
        _dispatch_kernel,
        grid_spec=grid_spec,
        out_shape=jax.ShapeDtypeStruct((n_tiles * TM * ROW, LANES), F32),
        compiler_params=pltpu.CompilerParams(dimension_semantics=("arbitrary",)),
        name="moe_dispatch",
    )(tile_end, pos_t, h2rows)


def _moe_kernel(te_ref, nt_ref, nxt_ref, par_ref, xs_ref, wgu_hbm, bgu_ref, wd_hbm, bd_ref, perm_ref, ys_ref,
                wgu_st, wd_st, wgu_bf, wd_bf, wsem):
    j = pl.program_id(0)
    nt = nt_ref[0]

    def weight_copies(e):
        s = par_ref[e]
        return (pltpu.make_async_copy(wgu_hbm.at[e], wgu_st.at[s], wsem.at[0, s]),
                pltpu.make_async_copy(wd_hbm.at[e], wd_st.at[s], wsem.at[1, s]))

    @pl.when(j == 0)
    def _():
        for cp in weight_copies(te_ref[0]):
            cp.start()

    @pl.when(j < nt)
    def _():
        e = te_ref[j]
        first = (j == 0) | (te_ref[jnp.maximum(j - 1, 0)] != e)

        @pl.when(first)
        def _():
            for cp in weight_copies(e):
                cp.wait()
            s = par_ref[e]
            perm = perm_ref[...]
            for g in range(2 * D_FF // 256):
                w = wgu_st[s, :, g * 256:(g + 1) * 256].astype(BF16)
                wgu_bf[:, g * 256:(g + 1) * 256] = _dot(w, perm).astype(BF16)
            wd_bf[...] = wd_st[s].astype(BF16)
            nxt = nxt_ref[e]

            @pl.when(nxt >= 0)
            def _():
                for cp in weight_copies(nxt):
                    cp.start()

        h = _rows_to_matrix(xs_ref, 0, TM).astype(BF16)
        gu = _dot(h, wgu_bf[...]) + bgu_ref[0]
        acts = []
        for g in range(D_FF // LANES):
            gate = jnp.minimum(gu[:, g * 256:g * 256 + 128], SWIGLU_LIMIT)
            up = jnp.clip(gu[:, g * 256 + 128:(g + 1) * 256], -SWIGLU_LIMIT, SWIGLU_LIMIT)
            acts.append(((up + 1.0) * gate * _sigmoid(SWIGLU_ALPHA * gate)).astype(BF16))
        y = _dot(jnp.concatenate(acts, axis=1), wd_bf[...]) + bd_ref[0]
        for s in range(ROW):
            ys_ref[pl.ds(s, TM, stride=ROW), :] = y[:, s * LANES:(s + 1) * LANES]

    @pl.when(j >= nt)
    def _():
        ys_ref[...] = jnp.zeros_like(ys_ref)


def _moe_call(te, nt, nxt, par, xs, wgu, bgu, wd, bd, perm):
    n_tiles = xs.shape[0] // (TM * ROW)
    tile = lambda j, te, nt, nxt, par: (jnp.minimum(j, nt[0] - 1), 0)
    exp = lambda j, te, nt, nxt, par: (te[jnp.minimum(j, nt[0] - 1)], 0, 0)
    grid_spec = pltpu.PrefetchScalarGridSpec(
        num_scalar_prefetch=4,
        grid=(n_tiles,),
        in_specs=[
            pl.BlockSpec((TM * ROW, LANES), tile),
            pl.BlockSpec(memory_space=pl.ANY),
            pl.BlockSpec((1, 1, 2 * D_FF), exp),
            pl.BlockSpec(memory_space=pl.ANY),
            pl.BlockSpec((1, 1, D_MODEL), exp),
            pl.BlockSpec((256, 256), lambda j, te, nt, nxt, par: (0, 0)),
        ],
        out_specs=pl.BlockSpec((TM * ROW, LANES), lambda j, te, nt, nxt, par: (j, 0)),
        scratch_shapes=[
            pltpu.VMEM((2, D_MODEL, 2 * D_FF), F32),
            pltpu.VMEM((2, D_FF, D_MODEL), F32),
            pltpu.VMEM((D_MODEL, 2 * D_FF), BF16),
            pltpu.VMEM((D_FF, D_MODEL), BF16),
            pltpu.SemaphoreType.DMA((2, 2)),
        ],
    )
    return pl.pallas_call(
        _moe_kernel,
        grid_spec=grid_spec,
        out_shape=jax.ShapeDtypeStruct(xs.shape, F32),
        compiler_params=pltpu.CompilerParams(
            dimension_semantics=("arbitrary",), vmem_limit_bytes=VMEM_LIMIT),
        name="moe_experts",
    )(te, nt, nxt, par, xs, wgu, bgu, wd, bd, perm)


def _final_kernel(pnext_ref, pcur_ref, wts_ref, x1_ref, mod_ref, gfin_ref, ys_hbm, out_ref, gbuf, sem):
    i = pl.program_id(0)
    n = pl.num_programs(0)

    def issue(pref, slot):
        for q in range(TOP_K * TD):
            _row_copy(ys_hbm, pref[0, 0, q], gbuf.at[slot], q, sem.at[slot]).start(priority=q % 2)

    pl.when(i == 0)(lambda: issue(pcur_ref, 0))
    pl.when(i + 1 < n)(lambda: issue(pnext_ref, (i + 1) % 2))
    slot = i % 2
    pltpu.make_async_copy(ys_hbm.at[pl.ds(0, TOP_K * TD * ROW)], gbuf.at[slot], sem.at[slot]).wait()
    gb = gbuf.at[slot]
    acc = jnp.zeros((TD, D_MODEL), F32)
    for kq in range(TOP_K):
        v = _rows_to_matrix(gb, kq * TD * ROW, TD)
        acc = acc + _col_from_row(wts_ref[kq:kq + 1, :], TD) * v
    xf = x1_ref[...] + mod_ref[0][5:6] * acc
    out_ref[...] = _rms(xf) * gfin_ref[...]


def _final_call(pos_t, wts, x1, mod, gfin, ys, S):
    N, D = x1.shape
    nblk = N // TD
    per_b = S // TD
    return pl.pallas_call(
        _final_kernel,
        grid=(nblk,),
        in_specs=[
            pl.BlockSpec((1, 1, TOP_K * TD), lambda i: (jnp.minimum(i + 1, nblk - 1), 0, 0),
                         memory_space=pltpu.SMEM),
            pl.BlockSpec((1, 1, TOP_K * TD), lambda i: (i, 0, 0), memory_space=pltpu.SMEM),
            pl.BlockSpec((TOP_K, TD), lambda i: (0, i)),
            pl.BlockSpec((TD, D), lambda i: (i, 0)),
            pl.BlockSpec((1, 8, D), lambda i: (i // per_b, 0, 0)),
            pl.BlockSpec((1, D), lambda i: (0, 0)),
            pl.BlockSpec(memory_space=pl.ANY),
        ],
        out_specs=pl.BlockSpec((TD, D), lambda i: (i, 0)),
        out_shape=jax.ShapeDtypeStruct((N, D), F32),
        scratch_shapes=[
            pltpu.VMEM((2, TOP_K * TD * ROW, LANES), F32),
            pltpu.SemaphoreType.DMA((2,)),
        ],
        compiler_params=pltpu.CompilerParams(
            dimension_semantics=("arbitrary",), vmem_limit_bytes=VMEM_LIMIT),
        name="combine_norm",
    )(pos_t, pos_t, wts, x1, mod, gfin, ys)


def _route_plan(idx, rank, cnt, N):
    n_tiles = N * TOP_K // TM + N_EXPERTS
    counts = cnt[:, 0]
    ntile = (counts + TM - 1) // TM
    tile_end = jnp.cumsum(ntile).astype(jnp.int32)
    row_start = (tile_end - ntile) * TM
    nt = tile_end[-1:]
    experts = jnp.arange(N_EXPERTS, dtype=jnp.int32)
    pos = jnp.sum(jnp.where(idx[..., None] == experts, row_start, 0), axis=-1) + rank
    te = jnp.minimum(jnp.sum(jnp.arange(n_tiles)[:, None] >= tile_end[None, :], axis=1),
                     N_EXPERTS - 1).astype(jnp.int32)
    pos_t = jnp.transpose(pos.reshape(TOP_K, N // TD, TD), (1, 0, 2)).reshape(N // TD, 1, TOP_K * TD)
    nonempty = ntile > 0
    par = ((jnp.cumsum(nonempty) - 1) % 2).astype(jnp.int32)
    later = nonempty[None, :] & (experts[None, :] > experts[:, None])
    nxt = jnp.min(jnp.where(later, experts[None, :], N_EXPERTS), axis=1)
    nxt = jnp.where(nxt == N_EXPERTS, -1, nxt).astype(jnp.int32)
    return te, nt, nxt, par, tile_end, pos_t, n_tiles


def _tri_constants():
    r = np.arange(TS)[:, None]
    c = np.arange(TS)[None, :]
    ltri = ((r // CHUNK) == (c // CHUNK)) & (c <= r)
    utri = r < c
    return jnp.asarray(ltri, BF16), jnp.asarray(utri, BF16)


def kernel(x, c, w_ada, b_ada, g_mix, w_in, b_in, attn_sinks, rel_bias, hg_lb, hg_norm_w, w_out, b_out, g_ffn, w_router, b_router, w_gate_up, b_gate_up, w_down, b_down, g_final):
    B, S, D = x.shape
    N = B * S
    mod6 = _prep_call(c, w_ada[0], b_ada)
    mod = jnp.pad(jnp.transpose(mod6, (1, 0, 2)), ((0, 0), (0, 2), (0, 0)))
    lb, bias = _tables_call(rel_bias, hg_lb, _bucket_table())

    wi, bi_ = w_in[0], b_in[0]
    def cols(a):
        aq, ak, av, rest = a[..., :512], a[..., 512:640], a[..., 640:768], a[..., 768:]
        k0, k1, v0, v1 = ak[..., :64], ak[..., 64:], av[..., :64], av[..., 64:]
        return jnp.concatenate([aq, k0, k0, k1, k1, v0, v0, v1, v1, rest], axis=-1)
    win = cols(wi).astype(BF16)
    bin_ = cols(bi_)[None, :]
    wrt = jnp.transpose(w_router[0])
    wr_hi = wrt.astype(BF16)
    wr_lo = (wrt - wr_hi.astype(F32)).astype(BF16)
    wr = jnp.concatenate([wr_hi, wr_lo], axis=0)
    br = jnp.broadcast_to(b_router[0][:, None], (N_EXPERTS, LANES))
    ltri, utri = _tri_constants()

    x1, h2rows, idx, wts, rank, cnt = _mix_call(
        attn_sinks[0], x, mod, g_mix, win, bin_, bias, lb, hg_norm_w, w_out[0].astype(BF16),
        b_out, g_ffn, wr, br, ltri, utri)

    te, nt, nxt, par, tile_end, pos_t, n_tiles = _route_plan(idx, rank, cnt, N)
    ii = np.arange(128)
    perm_np = np.zeros((256, 256), np.float32)
    perm_np[2 * ii, ii] = 1.0
    perm_np[2 * ii + 1, 128 + ii] = 1.0
    perm = jnp.asarray(perm_np, BF16)
    bgu = b_gate_up[0].reshape(N_EXPERTS, D_FF // LANES, LANES, 2)
    bgu = jnp.transpose(bgu, (0, 1, 3, 2)).reshape(N_EXPERTS, 1, 2 * D_FF)
    xs = _dispatch_call(tile_end, pos_t, h2rows, n_tiles)
    ys = _moe_call(te, nt, nxt, par, xs, w_gate_up[0], bgu, w_down[0], b_down[0][:, None, :], perm)
    out = _final_call(pos_t, wts, x1.reshape(N, D), mod, g_final[None, :], ys, S)
    return out.reshape(B, S, D)
```

```python
import functools
import math

import numpy as np
import jax
import jax.numpy as jnp
from jax import lax
from jax.experimental import pallas as pl
from jax.experimental.pallas import tpu as pltpu

D_MODEL = 1024
ATT_HEAD_DIM = 64
ATT_HEADS = 8
ATT_KV_HEADS = 2
ATT_SCALE = ATT_HEAD_DIM ** -0.5
BLOCK = 128
N_BUCKETS = 32
MAX_DISTANCE = 128
HG_HEADS = 4
HG_D = 128
CHUNK = 64
N_EXPERTS = 32
TOP_K = 4
D_FF = 1024
SWIGLU_LIMIT = 7.0
SWIGLU_ALPHA = 1.702
EPS = 1e-5

ATT_Q = ATT_HEADS * ATT_HEAD_DIM
HG_W = HG_HEADS * HG_D
COL_Q = 0
COL_KV = ATT_Q
COL_HQ = COL_KV + 4 * 128
COL_HF = COL_HQ + HG_W
COL_HI = COL_HF + HG_W
COL_HG = COL_HI + HG_W
PROJ_COLS = COL_HG + HG_W

LANES = 128
SUBLANES = 8
TS = 256
VMEM_LIMIT = 56 * 1024 * 1024
HG_SAFE_LEVELS = (32, 16, 8)
HG_DIAG = 8
HG_CHUNKS_PER_GROUP = 2
HG_EXP_CAP = 80.0

F32 = jnp.float32
BF16 = jnp.bfloat16
NEG_INF = float("-inf")


def _dot(a, b):
    return jnp.dot(a, b, preferred_element_type=F32)


def _dot_nt(a, b):
    return lax.dot_general(a, b, (((1,), (1,)), ((), ())), preferred_element_type=F32)


def _dot_tn(a, b):
    return lax.dot_general(a, b, (((0,), (0,)), ((), ())), preferred_element_type=F32)


def _sigmoid(x):
    return 1.0 / (1.0 + jnp.exp(-x))


def _prep_kernel(c_ref, w_ref, b_ref, o_ref):
    c = c_ref[...]
    cond = c * _sigmoid(c)
    o_ref[0] = jnp.dot(cond, w_ref[...], precision=lax.Precision.HIGHEST,
                       preferred_element_type=F32) + b_ref[...]


def _prep_call(c, w_ada, b_ada):
    B = c.shape[0]
    return pl.pallas_call(
        _prep_kernel,
        grid=(6,),
        in_specs=[
            pl.BlockSpec((B, D_MODEL), lambda j: (0, 0)),
            pl.BlockSpec((D_MODEL, D_MODEL), lambda j: (0, j)),
            pl.BlockSpec((1, D_MODEL), lambda j: (0, j)),
        ],
        out_specs=pl.BlockSpec((1, B, D_MODEL), lambda j: (j, 0, 0)),
        out_shape=jax.ShapeDtypeStruct((6, B, D_MODEL), F32),
        compiler_params=pltpu.CompilerParams(dimension_semantics=("arbitrary",)),
        name="adaln_mod",
    )(c, w_ada, b_ada)


def _tables_kernel(rb_ref, lbp_ref, bucket_ref, lb_ref, bias_ref):
    p = lbp_ref[...]
    e = jnp.exp(p - jnp.max(p, axis=0, keepdims=True))
    sm = e / jnp.sum(e, axis=0, keepdims=True)
    lb_ref[...] = sm[0:1]
    bucket = bucket_ref[...]
    for h in range(ATT_HEADS):
        acc = jnp.full(bucket.shape, NEG_INF, F32)
        for bk in range(N_BUCKETS):
            acc = jnp.where(bucket == bk, rb_ref[bk, h], acc)
        bias_ref[h] = acc


def _tables_call(rel_bias, hg_lb, bucket):
    return pl.pallas_call(
        _tables_kernel,
        in_specs=[
            pl.BlockSpec(memory_space=pltpu.SMEM),
            pl.BlockSpec(memory_space=pltpu.VMEM),
            pl.BlockSpec(memory_space=pltpu.VMEM),
        ],
        out_specs=[pl.BlockSpec(memory_space=pltpu.VMEM), pl.BlockSpec(memory_space=pltpu.VMEM)],
        out_shape=[jax.ShapeDtypeStruct((1, HG_W), F32),
                   jax.ShapeDtypeStruct((ATT_HEADS, BLOCK, 2 * BLOCK), F32)],
        name="tables",
    )(rel_bias, hg_lb, bucket)


def _bucket_table():
    i = np.arange(BLOCK, dtype=np.int32)[:, None]
    m = np.arange(2 * BLOCK, dtype=np.int32)[None, :]
    dist = i + BLOCK - m
    n = np.maximum(dist, 0)
    max_exact = N_BUCKETS // 2
    nf = np.maximum(n, 1).astype(np.float32)
    large = max_exact + (np.log(nf / np.float32(max_exact)) / np.float32(math.log(MAX_DISTANCE / max_exact))
                         * np.float32(N_BUCKETS - max_exact)).astype(np.int32)
    large = np.minimum(large, N_BUCKETS - 1)
    bucket = np.where(n < max_exact, n, large)
    return jnp.asarray(np.where((dist >= 0) & (dist < BLOCK), bucket, -1), jnp.int32)


def _rms(x):
    return x * lax.rsqrt(jnp.mean(x * x, axis=-1, keepdims=True) + EPS)


def _attention_block(proj_ref, kvprev_ref, bias_ref, sink_ref, mixed_ref, blk, first_tile):
    r0 = blk * BLOCK
    cur = proj_ref[r0:r0 + BLOCK, COL_KV:COL_KV + 512]
    if blk == 0:
        prev = kvprev_ref[...]
    else:
        prev = proj_ref[r0 - BLOCK:r0, COL_KV:COL_KV + 512]
    band = jnp.concatenate([prev, cur], axis=0)
    lane = lax.broadcasted_iota(jnp.int32, (2 * BLOCK, LANES), 1)
    lo = lane < ATT_HEAD_DIM
    col = lax.broadcasted_iota(jnp.int32, (1, 2 * BLOCK), 1)
    if blk == 0:
        pen = jnp.where((col < BLOCK) & first_tile, NEG_INF, 0.0)
    kbs, vbs = [], []
    for g in range(ATT_KV_HEADS):
        kd = band[:, g * 128:(g + 1) * 128]
        vd = band[:, 256 + g * 128:256 + (g + 1) * 128]
        kbs.append(jnp.concatenate([jnp.where(lo, kd, 0.0), jnp.where(lo, 0.0, kd)], axis=0).astype(BF16))
        vbs.append(jnp.concatenate([jnp.where(lo, vd, 0.0), jnp.where(lo, 0.0, vd)], axis=0).astype(BF16))
    n_pairs = ATT_HEADS // 2
    ss = []
    for g in range(ATT_KV_HEADS):
        qg = jnp.concatenate(
            [(proj_ref[r0:r0 + BLOCK, pair * 128:(pair + 1) * 128] * ATT_SCALE).astype(BF16)
             for pair in (2 * g, 2 * g + 1)], axis=0)
        sg = _dot_nt(qg, kbs[g])
        ss += [sg[0:BLOCK], sg[BLOCK:2 * BLOCK]]
    ps, invs = [], []
    for h in range(ATT_HEADS):
        sh = ss[h // 2][:, (h % 2) * 256:(h % 2 + 1) * 256] + bias_ref[h]
        if blk == 0:
            sh = sh + pen
        sink = sink_ref[h]
        mx = jnp.maximum(jnp.max(sh, axis=-1, keepdims=True), sink)
        p = jnp.exp(sh - mx)
        den = jnp.sum(p, axis=-1, keepdims=True) + jnp.exp(sink - mx)
        ps.append(p.astype(BF16))
        invs.append(1.0 / den)
    os_ = []
    for g in range(ATT_KV_HEADS):
        pg = jnp.concatenate([jnp.concatenate(ps[2 * pair:2 * pair + 2], axis=1)
                              for pair in (2 * g, 2 * g + 1)], axis=0)
        og = _dot(pg, vbs[g])
        os_ += [og[0:BLOCK], og[BLOCK:2 * BLOCK]]
    lane_o = lax.broadcasted_iota(jnp.int32, (BLOCK, LANES), 1)
    for pair in range(n_pairs):
        o = os_[pair] * jnp.where(lane_o < ATT_HEAD_DIM, invs[2 * pair], invs[2 * pair + 1])
        mixed_ref[r0:r0 + BLOCK, pair * 128:(pair + 1) * 128] = o.astype(BF16)


def _hgrn_scaled_operands(b, q, k):
    pairs = []
    for m in HG_SAFE_LEVELS:
        qs, ks = [], []
        for bi in range(CHUNK // m):
            rows = slice(bi * m, (bi + 1) * m)
            if bi % 2 == 1:
                ref = b[bi * m - 1:bi * m]
                qs.append(q[rows] * jnp.exp(b[rows] - ref))
                ks.append(jnp.zeros((m, HG_D), F32))
            else:
                ref = b[(bi + 1) * m - 1:(bi + 1) * m]
                qs.append(jnp.zeros((m, HG_D), F32))
                ks.append(k[rows] * jnp.exp(ref - b[rows]))
        pairs.append((m, jnp.concatenate(qs, axis=0).astype(BF16), jnp.concatenate(ks, axis=0).astype(BF16)))
    qs, ks = [], []
    for bi in range(CHUNK // HG_DIAG):
        rows = slice(bi * HG_DIAG, (bi + 1) * HG_DIAG)
        c = b[rows] if bi == 0 else b[rows] - b[bi * HG_DIAG - 1:bi * HG_DIAG]
        qs.append(q[rows] * jnp.exp(c))
        ks.append(k[rows] * jnp.exp(jnp.minimum(-c, HG_EXP_CAP)))
    pairs.append((0, jnp.concatenate(qs, axis=0).astype(BF16), jnp.concatenate(ks, axis=0).astype(BF16)))
    return pairs


def _hgrn_chunks(proj_ref, states, b_all, kk_all, qf_all, hgw_ref, mixed_ref, chunks, masks):
    units = [(ck, hd) for ck in chunks for hd in range(HG_HEADS)]
    rows = {u: slice(u[0] * CHUNK, (u[0] + 1) * CHUNK) for u in units}
    cols = {u: slice(u[1] * HG_D, (u[1] + 1) * HG_D) for u in units}
    bs = {u: b_all[rows[u], cols[u]] for u in units}
    qs = {u: qf_all[rows[u], cols[u]] for u in units}
    ks = {u: kk_all[rows[u], cols[u]] for u in units}
    vs = {u: proj_ref[rows[u], COL_HI + u[1] * HG_D:COL_HI + (u[1] + 1) * HG_D].astype(BF16) for u in units}
    operands = {u: _hgrn_scaled_operands(bs[u], qs[u], ks[u]) for u in units}
    qbs = {u: (qs[u] * jnp.exp(bs[u])).astype(BF16) for u in units}
    bends = {u: bs[u][CHUNK - 1:CHUNK] for u in units}
    kdecs = {u: (ks[u] * jnp.exp(bends[u] - bs[u])).astype(BF16) for u in units}
    prods = {u: [(m, _dot_nt(ql, kl)) for m, ql, kl in operands[u]] for u in units}
    kvs = {u: _dot_tn(vs[u], kdecs[u]) for u in units}
    intras = {}
    for u in units:
        sc = jnp.zeros((CHUNK, CHUNK), F32)
        for m, pr in prods[u]:
            sc = sc + jnp.where(masks[m], pr, 0.0)
        intras[u] = _dot(sc.astype(BF16), vs[u])
    for ck in chunks:
        inters = [_dot_nt(qbs[(ck, hd)], states[hd].astype(BF16)) for hd in range(HG_HEADS)]
        for hd in range(HG_HEADS):
            u = (ck, hd)
            states[hd] = states[hd] * jnp.exp(bends[u]) + kvs[u]
            gate = proj_ref[rows[u], COL_HG + hd * HG_D:COL_HG + (hd + 1) * HG_D]
            on = _rms(intras[u] + inters[hd]) * hgw_ref[...] * (gate * _sigmoid(gate))
            mixed_ref[rows[u], ATT_Q + hd * HG_D:ATT_Q + (hd + 1) * HG_D] = on.astype(BF16)


def _hgrn_masks():
    r = lax.broadcasted_iota(jnp.int32, (CHUNK, CHUNK), 0)
    c = lax.broadcasted_iota(jnp.int32, (CHUNK, CHUNK), 1)
    masks = {}
    for m in HG_SAFE_LEVELS:
        masks[m] = ((r // (2 * m)) == (c // (2 * m))) & (((r // m) % 2) == 1) & (((c // m) % 2) == 0)
    masks[0] = ((r // HG_DIAG) == (c // HG_DIAG)) & (c <= r)
    return masks


def _mix_kernel(sink_ref, x_ref, mod_ref, gmix_ref, win_ref, bin_ref, bias_ref, lb_ref, hgw_ref,
                wout_ref, bout_ref, gffn_ref, wr_ref, br_ref, ltri_ref, utri_ref,
                x1_ref, h2_ref, idx_ref, wts_ref, rank_ref, cnt_ref,
                proj_ref, mixed_ref, kvprev_ref, state_ref, run_ref, win_sc, wout_sc):
    bi = pl.program_id(0)
    j = pl.program_id(1)

    @pl.when(j == 0)
    def _():
        kvprev_ref[...] = jnp.zeros_like(kvprev_ref)
        state_ref[...] = jnp.zeros_like(state_ref)

    @pl.when((j == 0) & (bi == 0))
    def _():
        run_ref[...] = jnp.zeros_like(run_ref)
        win_sc[...] = win_ref[...]
        wout_sc[...] = wout_ref[...]

    x = x_ref[0]
    mod = mod_ref[0]
    h = _rms(x) * (gmix_ref[...] * (1.0 + mod[1:2])) + mod[0:1]
    hb = h.astype(BF16)

    def project(c0, c1):
        proj_ref[:, c0:c1] = _dot(hb, win_sc[:, c0:c1]) + bin_ref[:, c0:c1]

    project(0, COL_HQ)
    for blk in range(TS // BLOCK):
        _attention_block(proj_ref, kvprev_ref, bias_ref, sink_ref, mixed_ref, blk, j == 0)
        project(COL_HQ + blk * 1024, COL_HQ + (blk + 1) * 1024)
    kvprev_ref[...] = proj_ref[TS - BLOCK:TS, COL_KV:COL_KV + 512]

    lbv = lb_ref[...]
    fl = lbv + (1.0 - lbv) * _sigmoid(proj_ref[:, COL_HF:COL_HF + HG_W])
    g = jnp.log(fl)
    g_hi = g.astype(BF16)
    r1 = g - g_hi.astype(F32)
    g_mid = r1.astype(BF16)
    g_lo = (r1 - g_mid.astype(F32)).astype(BF16)
    ltri = ltri_ref[...]
    b_all = _dot(ltri, g_hi) + _dot(ltri, g_mid) + _dot(ltri, g_lo)

    masks = _hgrn_masks()
    qr = proj_ref[:, COL_HQ:COL_HQ + HG_W]
    qf_all = qr * _sigmoid(qr)
    kk_all = 1.0 - fl
    states = [state_ref[hd] for hd in range(HG_HEADS)]
    n_chunks = TS // CHUNK
    for c0 in range(0, n_chunks, HG_CHUNKS_PER_GROUP):
        _hgrn_chunks(proj_ref, states, b_all, kk_all, qf_all, hgw_ref, mixed_ref,
                     range(c0, c0 + HG_CHUNKS_PER_GROUP), masks)
    for hd in range(HG_HEADS):
        state_ref[hd] = states[hd]

    y = _dot(mixed_ref[...], wout_sc[...]) + bout_ref[...]
    x1 = x + mod[2:3] * y
    x1_ref[0] = x1
    h2 = _rms(x1) * (gffn_ref[...] * (1.0 + mod[4:5])) + mod[3:4]
    for s in range(SUBLANES):
        h2_ref[pl.ds(s, TS, stride=SUBLANES), :] = h2[:, s * LANES:(s + 1) * LANES]

    h_hi = h2.astype(BF16)
    h_lo = (h2 - h_hi.astype(F32)).astype(BF16)
    wr = wr_ref[...]
    t1 = _dot_nt(wr, h_hi)
    t2 = _dot_nt(wr[0:N_EXPERTS], h_lo)
    logits = t1[0:N_EXPERTS] + t1[N_EXPERTS:2 * N_EXPERTS] + t2 + br_ref[:, 0:1]

    eidx = lax.broadcasted_iota(jnp.int32, (N_EXPERTS, TS), 0)
    l = logits
    vals, ohs, ids = [], [], []
    for _ in range(TOP_K):
        mx = jnp.max(l, axis=0, keepdims=True)
        ik = jnp.min(jnp.where(l == mx, eidx, N_EXPERTS), axis=0, keepdims=True)
        oh = eidx == ik
        vals.append(mx)
        ids.append(ik)
        ohs.append(oh)
        l = jnp.where(oh, NEG_INF, l)
    es = [jnp.exp(v - vals[0]) for v in vals]
    den = es[0] + es[1] + es[2] + es[3]
    run = run_ref[...]
    base = run[:, 0:1]
    utri = utri_ref[...]
    ranks = []
    for kq in range(TOP_K):
        ohf = jnp.where(ohs[kq], 1.0, 0.0)
        pref = _dot(ohf.astype(BF16), utri)
        ranks.append(jnp.sum(jnp.where(ohs[kq], base + pref, 0.0), axis=0, keepdims=True))
        base = base + jnp.sum(ohf, axis=1, keepdims=True)
    run_new = jnp.broadcast_to(base, run.shape)
    run_ref[...] = run_new
    cnt_ref[...] = run_new.astype(jnp.int32)
    idx_ref[...] = jnp.concatenate(ids, axis=0)
    wts_ref[...] = jnp.concatenate([e / den for e in es], axis=0)
    rank_ref[...] = jnp.concatenate(ranks, axis=0).astype(jnp.int32)


def _mix_call(sinks, x, mod, gmix, win, bin_, bias, lb, hgw, wout, bout, gffn, wr, br, ltri, utri):
    B, S, D = x.shape
    N = B * S
    nj = S // TS
    const2 = lambda b, j: (0, 0)
    tok = lambda b, j: (0, b * nj + j)
    in_specs = [
        pl.BlockSpec(memory_space=pltpu.SMEM),
        pl.BlockSpec((1, TS, D), lambda b, j: (b, j, 0)),
        pl.BlockSpec((1, 8, D), lambda b, j: (b, 0, 0)),
        pl.BlockSpec((1, D), const2),
        pl.BlockSpec((D, PROJ_COLS), const2),
        pl.BlockSpec((1, PROJ_COLS), const2),
        pl.BlockSpec((ATT_HEADS, BLOCK, 2 * BLOCK), lambda b, j: (0, 0, 0)),
        pl.BlockSpec((1, HG_W), const2),
        pl.BlockSpec((1, HG_D), const2),
        pl.BlockSpec((D, D), const2),
        pl.BlockSpec((1, D), const2),
        pl.BlockSpec((1, D), const2),
        pl.BlockSpec((2 * N_EXPERTS, D), const2),
        pl.BlockSpec((N_EXPERTS, LANES), const2),
        pl.BlockSpec((TS, TS), const2),
        pl.BlockSpec((TS, TS), const2),
    ]
    out_specs = [
        pl.BlockSpec((1, TS, D), lambda b, j: (b, j, 0)),
        pl.BlockSpec((TS * SUBLANES, LANES), lambda b, j: (b * nj + j, 0)),
        pl.BlockSpec((TOP_K, TS), tok),
        pl.BlockSpec((TOP_K, TS), tok),
        pl.BlockSpec((TOP_K, TS), tok),
        pl.BlockSpec((N_EXPERTS, LANES), const2),
    ]
    out_shape = [
        jax.ShapeDtypeStruct((B, S, D), F32),
        jax.ShapeDtypeStruct((N * SUBLANES, LANES), F32),
        jax.ShapeDtypeStruct((TOP_K, N), jnp.int32),
        jax.ShapeDtypeStruct((TOP_K, N), F32),
        jax.ShapeDtypeStruct((TOP_K, N), jnp.int32),
        jax.ShapeDtypeStruct((N_EXPERTS, LANES), jnp.int32),
    ]
    scratch = [
        pltpu.VMEM((TS, PROJ_COLS), F32),
        pltpu.VMEM((TS, D), BF16),
        pltpu.VMEM((BLOCK, 512), F32),
        pltpu.VMEM((HG_HEADS, HG_D, HG_D), F32),
        pltpu.VMEM((N_EXPERTS, LANES), F32),
        pltpu.VMEM((D, PROJ_COLS), BF16),
        pltpu.VMEM((D, D), BF16),
    ]
    return pl.pallas_call(
        _mix_kernel,
        grid=(B, nj),
        in_specs=in_specs,
        out_specs=out_specs,
        out_shape=out_shape,
        scratch_shapes=scratch,
        compiler_params=pltpu.CompilerParams(
            dimension_semantics=("arbitrary", "arbitrary"), vmem_limit_bytes=VMEM_LIMIT),
        name="mix_router",
    )(sinks, x, mod, gmix, win, bin_, bias, lb, hgw, wout, bout, gffn, wr, br, ltri, utri)


TM = 256
TD = 256
ROW = SUBLANES


def _row_copy(src, src_row, dst, dst_row, sem):
    return pltpu.make_async_copy(src.at[pl.ds(pl.multiple_of(src_row * ROW, ROW), ROW)],
                                 dst.at[pl.ds(pl.multiple_of(dst_row * ROW, ROW), ROW)], sem)


def _col_from_row(w_row, n):
    rr = lax.broadcasted_iota(jnp.int32, (n, n), 0)
    cc = lax.broadcasted_iota(jnp.int32, (n, n), 1)
    return jnp.sum(jnp.where(rr == cc, w_row, 0.0), axis=1, keepdims=True)


def _rows_to_matrix(ref, base, n):
    return jnp.concatenate([ref[pl.ds(base + s, n, stride=ROW), :] for s in range(ROW)], axis=1)


def _dispatch_kernel(tend_ref, pos_ref, h2_hbm, xs_hbm, zbuf, hbuf, sem, lsem, zsem):
    i = pl.program_id(0)
    n = pl.num_programs(0)
    n_tiles = xs_hbm.shape[0] // (TM * ROW)

    def fill_tile(tile):
        return pltpu.make_async_copy(
            zbuf, xs_hbm.at[pl.ds(pl.multiple_of(tile * (TM * ROW), TM * ROW), TM * ROW)], zsem.at[0])

    def padded(e):
        return tend_ref[e] > (tend_ref[e - 1] if e > 0 else 0), tend_ref[e] - 1

    def unused(u):
        tile = tend_ref[N_EXPERTS - 1] + u
        return tile < n_tiles, tile

    @pl.when(i == 0)
    def _():
        zbuf[...] = jnp.zeros_like(zbuf)
        sites = [padded(e) for e in range(N_EXPERTS)] + [unused(u) for u in range(N_EXPERTS)]
        for cond, tile in sites:
            pl.when(cond)(lambda tile=tile: fill_tile(tile).start())
        for cond, tile in sites:
            pl.when(cond)(lambda tile=tile: fill_tile(tile).wait())

    def load(step):
        s = step % 3
        return pltpu.make_async_copy(
            h2_hbm.at[pl.ds(pl.multiple_of(step * (TD * ROW), TD * ROW), TD * ROW)], hbuf.at[s], lsem.at[s])

    def wait_rows(step):
        s = step % 3
        for _ in range(TOP_K):
            pltpu.make_async_copy(hbuf.at[s], xs_hbm.at[pl.ds(0, TD * ROW)], sem.at[s]).wait()

    pl.when(i == 0)(lambda: load(i).start())
    pl.when(i >= 2)(lambda: wait_rows(i - 2))
    pl.when(i + 1 < n)(lambda: load(i + 1).start())
    load(i).wait()
    slot = i % 3
    for q in range(TOP_K * TD):
        _row_copy(hbuf.at[slot], q % TD, xs_hbm, pos_ref[0, 0, q], sem.at[slot]).start(priority=q % 2)

    @pl.when(i == n - 1)
    def _():
        pl.when(i >= 1)(lambda: wait_rows(i - 1))
        wait_rows(i)


def _dispatch_call(tile_end, pos_t, h2rows, n_tiles):
    nblk = pos_t.shape[0]
    grid_spec = pltpu.PrefetchScalarGridSpec(
        num_scalar_prefetch=1,
        grid=(nblk,),
        in_specs=[
            pl.BlockSpec((1, 1, TOP_K * TD), lambda i, tend: (i, 0, 0), memory_space=pltpu.SMEM),
            pl.BlockSpec(memory_space=pl.ANY),
        ],
        out_specs=pl.BlockSpec(memory_space=pl.ANY),
        scratch_shapes=[
            pltpu.VMEM((TM * ROW, LANES), F32),
            pltpu.VMEM((3, TD * ROW, LANES), F32),
            pltpu.SemaphoreType.DMA((3,)),
            pltpu.SemaphoreType.DMA((3,)),
            pltpu.SemaphoreType.DMA((1,)),
        ],
    )
    return pl.pallas_call(
        _dispatch_kernel,
        grid_spec=grid_spec,
        out_shape=jax.ShapeDtypeStruct((n_tiles * TM * ROW, LANES), F32),
        compiler_params=pltpu.CompilerParams(dimension_semantics=("arbitrary",)),
        name="moe_dispatch",
    )(tile_end, pos_t, h2rows)


def _moe_kernel(te_ref, nt_ref, nxt_ref, par_ref, xs_ref, wgu_hbm, bgu_ref, wd_hbm, bd_ref, perm_ref, ys_ref,
                wgu_st, wd_st, wgu_bf, wd_bf, wsem):
    j = pl.program_id(0)
    nt = nt_ref[0]

    def weight_copies(e):
        s = par_ref[e]
        return (pltpu.make_async_copy(wgu_hbm.at[e], wgu_st.at[s], wsem.at[0, s]),
                pltpu.make_async_copy(wd_hbm.at[e], wd_st.at[s], wsem.at[1, s]))

    @pl.when(j == 0)
    def _():
        for cp in weight_copies(te_ref[0]):
            cp.start()

    @pl.when(j < nt)
    def _():
        e = te_ref[j]
        first = (j == 0) | (te_ref[jnp.maximum(j - 1, 0)] != e)

        @pl.when(first)
        def _():
            for cp in weight_copies(e):
                cp.wait()
            s = par_ref[e]
            perm = perm_ref[...]
            for g in range(2 * D_FF // 256):
                w = wgu_st[s, :, g * 256:(g + 1) * 256].astype(BF16)
                wgu_bf[:, g * 256:(g + 1) * 256] = _dot(w, perm).astype(BF16)
            wd_bf[...] = wd_st[s].astype(BF16)
            nxt = nxt_ref[e]

            @pl.when(nxt >= 0)
            def _():
                for cp in weight_copies(nxt):
                    cp.start()

        h = _rows_to_matrix(xs_ref, 0, TM).astype(BF16)
        gu = _dot(h, wgu_bf[...]) + bgu_ref[0]
        acts = []
        for g in range(D_FF // LANES):
            gate = jnp.minimum(gu[:, g * 256:g * 256 + 128], SWIGLU_LIMIT)
            up = jnp.clip(gu[:, g * 256 + 128:(g + 1) * 256], -SWIGLU_LIMIT, SWIGLU_LIMIT)
            acts.append(((up + 1.0) * gate * _sigmoid(SWIGLU_ALPHA * gate)).astype(BF16))
        y = _dot(jnp.concatenate(acts, axis=1), wd_bf[...]) + bd_ref[0]
        for s in range(ROW):
            ys_ref[pl.ds(s, TM, stride=ROW), :] = y[:, s * LANES:(s + 1) * LANES]

    @pl.when(j >= nt)
    def _():
        ys_ref[...] = jnp.zeros_like(ys_ref)


def _moe_call(te, nt, nxt, par, xs, wgu, bgu, wd, bd, perm):
    n_tiles = xs.shape[0] // (TM * ROW)
    tile = lambda j, te, nt, nxt, par: (jnp.minimum(j, nt[0] - 1), 0)
    exp = lambda j, te, nt, nxt, par: (te[jnp.minimum(j, nt[0] - 1)], 0, 0)
    grid_spec = pltpu.PrefetchScalarGridSpec(
        num_scalar_prefetch=4,
        grid=(n_tiles,),
        in_specs=[
            pl.BlockSpec((TM * ROW, LANES), tile),
            pl.BlockSpec(memory_space=pl.ANY),
            pl.BlockSpec((1, 1, 2 * D_FF), exp),
            pl.BlockSpec(memory_space=pl.ANY),
            pl.BlockSpec((1, 1, D_MODEL), exp),
            pl.BlockSpec((256, 256), lambda j, te, nt, nxt, par: (0, 0)),
        ],
        out_specs=pl.BlockSpec((TM * ROW, LANES), lambda j, te, nt, nxt, par: (j, 0)),
        scratch_shapes=[
            pltpu.VMEM((2, D_MODEL, 2 * D_FF), F32),
            pltpu.VMEM((2, D_FF, D_MODEL), F32),
            pltpu.VMEM((D_MODEL, 2 * D_FF), BF16),
            pltpu.VMEM((D_FF, D_MODEL), BF16),
            pltpu.SemaphoreType.DMA((2, 2)),
        ],
    )
    return pl.pallas_call(
        _moe_kernel,
        grid_spec=grid_spec,
        out_shape=jax.ShapeDtypeStruct(xs.shape, F32),
        compiler_params=pltpu.CompilerParams(
            dimension_semantics=("arbitrary",), vmem_limit_bytes=VMEM_LIMIT),
        name="moe_experts",
    )(te, nt, nxt, par, xs, wgu, bgu, wd, bd, perm)


def _final_kernel(pnext_ref, pcur_ref, wts_ref, x1_ref, mod_ref, gfin_ref, ys_hbm, out_ref, gbuf, sem):
    i = pl.program_id(0)
    n = pl.num_programs(0)

    def issue(pref, slot):
        for q in range(TOP_K * TD):
            _row_copy(ys_hbm, pref[0, 0, q], gbuf.at[slot], q, sem.at[slot]).start(priority=q % 2)

    pl.when(i == 0)(lambda: issue(pcur_ref, 0))
    pl.when(i + 1 < n)(lambda: issue(pnext_ref, (i + 1) % 2))
    slot = i % 2
    pltpu.make_async_copy(ys_hbm.at[pl.ds(0, TOP_K * TD * ROW)], gbuf.at[slot], sem.at[slot]).wait()
    gb = gbuf.at[slot]
    acc = jnp.zeros((TD, D_MODEL), F32)
    for kq in range(TOP_K):
        v = _rows_to_matrix(gb, kq * TD * ROW, TD)
        acc = acc + _col_from_row(wts_ref[kq:kq + 1, :], TD) * v
    xf = x1_ref[...] + mod_ref[0][5:6] * acc
    out_ref[...] = _rms(xf) * gfin_ref[...]


def _final_call(pos_t, wts, x1, mod, gfin, ys, S):
    N, D = x1.shape
    nblk = N // TD
    per_b = S // TD
    return pl.pallas_call(
        _final_kernel,
        grid=(nblk,),
        in_specs=[
            pl.BlockSpec((1, 1, TOP_K * TD), lambda i: (jnp.minimum(i + 1, nblk - 1), 0, 0),
                         memory_space=pltpu.SMEM),
            pl.BlockSpec((1, 1, TOP_K * TD), lambda i: (i, 0, 0), memory_space=pltpu.SMEM),
            pl.BlockSpec((TOP_K, TD), lambda i: (0, i)),
            pl.BlockSpec((TD, D), lambda i: (i, 0)),
            pl.BlockSpec((1, 8, D), lambda i: (i // per_b, 0, 0)),
            pl.BlockSpec((1, D), lambda i: (0, 0)),
            pl.BlockSpec(memory_space=pl.ANY),
        ],
        out_specs=pl.BlockSpec((TD, D), lambda i: (i, 0)),
        out_shape=jax.ShapeDtypeStruct((N, D), F32),
        scratch_shapes=[
            pltpu.VMEM((2, TOP_K * TD * ROW, LANES), F32),
            pltpu.SemaphoreType.DMA((2,)),
        ],
        compiler_params=pltpu.CompilerParams(
            dimension_semantics=("arbitrary",), vmem_limit_bytes=VMEM_LIMIT),
        name="combine_norm",
    )(pos_t, pos_t, wts, x1, mod, gfin, ys)


def _route_plan(idx, rank, cnt, N):
    n_tiles = N * TOP_K // TM + N_EXPERTS
    counts = cnt[:, 0]
    ntile = (counts + TM - 1) // TM
    tile_end = jnp.cumsum(ntile).astype(jnp.int32)
    row_start = (tile_end - ntile) * TM
    nt = tile_end[-1:]
    experts = jnp.arange(N_EXPERTS, dtype=jnp.int32)
    pos = jnp.sum(jnp.where(idx[..., None] == experts, row_start, 0), axis=-1) + rank
    te = jnp.minimum(jnp.sum(jnp.arange(n_tiles)[:, None] >= tile_end[None, :], axis=1),
                     N_EXPERTS - 1).astype(jnp.int32)
    pos_t = jnp.transpose(pos.reshape(TOP_K, N // TD, TD), (1, 0, 2)).reshape(N // TD, 1, TOP_K * TD)
    nonempty = ntile > 0
    par = ((jnp.cumsum(nonempty) - 1) % 2).astype(jnp.int32)
    later = nonempty[None, :] & (experts[None, :] > experts[:, None])
    nxt = jnp.min(jnp.where(later, experts[None, :], N_EXPERTS), axis=1)
    nxt = jnp.where(nxt == N_EXPERTS, -1, nxt).astype(jnp.int32)
    return te, nt, nxt, par, tile_end, pos_t, n_tiles


def _tri_constants():
    r = np.arange(TS)[:, None]
    c = np.arange(TS)[None, :]
    ltri = ((r // CHUNK) == (c // CHUNK)) & (c <= r)
    utri = r < c
    return jnp.asarray(ltri, BF16), jnp.asarray(utri, BF16)


def kernel(x, c, w_ada, b_ada, g_mix, w_in, b_in, attn_sinks, rel_bias, hg_lb, hg_norm_w, w_out, b_out, g_ffn, w_router, b_router, w_gate_up, b_gate_up, w_down, b_down, g_final):
    B, S, D = x.shape
    N = B * S
    mod6 = _prep_call(c, w_ada[0], b_ada)
    mod = jnp.pad(jnp.transpose(mod6, (1, 0, 2)), ((0, 0), (0, 2), (0, 0)))
    lb, bias = _tables_call(rel_bias, hg_lb, _bucket_table())

    wi, bi_ = w_in[0], b_in[0]
    def cols(a):
        aq, ak, av, rest = a[..., :512], a[..., 512:640], a[..., 640:768], a[..., 768:]
        k0, k1, v0, v1 = ak[..., :64], ak[..., 64:], av[..., :64], av[..., 64:]
        return jnp.concatenate([aq, k0, k0, k1, k1, v0, v0, v1, v1, rest], axis=-1)
    win = cols(wi).astype(BF16)
    bin_ = cols(bi_)[None, :]
    wrt = jnp.transpose(w_router[0])
    wr_hi = wrt.astype(BF16)
    wr_lo = (wrt - wr_hi.astype(F32)).astype(BF16)
    wr = jnp.concatenate([wr_hi, wr_lo], axis=0)
    br = jnp.broadcast_to(b_router[0][:, None], (N_EXPERTS, LANES))
    ltri, utri = _tri_constants()

    x1, h2rows, idx, wts, rank, cnt = _mix_call(
        attn_sinks[0], x, mod, g_mix, win, bin_, bias, lb, hg_norm_w, w_out[0].astype(BF16),
        b_out, g_ffn, wr, br, ltri, utri)

    te, nt, nxt, par, tile_end, pos_t, n_tiles = _route_plan(idx, rank, cnt, N)
    ii = np.arange(128)
    perm_np = np.zeros((256, 256), np.float32)
    perm_np[2 * ii, ii] = 1.0
    perm_np[2 * ii + 1, 128 + ii] = 1.0
    perm = jnp.asarray(perm_np, BF16)
    bgu = b_gate_up[0].reshape(N_EXPERTS, D_FF // LANES, LANES, 2)
    bgu = jnp.transpose(bgu, (0, 1, 3, 2)).reshape(N_EXPERTS, 1, 2 * D_FF)
    xs = _dispatch_call(tile_end, pos_t, h2rows, n_tiles)
    ys = _moe_call(te, nt, nxt, par, xs, w_gate_up[0], bgu, w_down[0], b_down[0][:, None, :], perm)
    out = _final_call(pos_t, wts, x1.reshape(N, D), mod, g_final[None, :], ys, S)
    return out.reshape(B, S, D)
```

```python
import functools
import math

import numpy as np
import jax
import jax.numpy as jnp
from jax import lax
from jax.experimental import pallas as pl
from jax.experimental.pallas import tpu as pltpu

D_MODEL = 1024
ATT_HEAD_DIM = 64
ATT_HEADS = 8
ATT_KV_HEADS = 2
ATT_SCALE = ATT_HEAD_DIM ** -0.5
BLOCK = 128
N_BUCKETS = 32
MAX_DISTANCE = 128
HG_HEADS = 4
HG_D = 128
CHUNK = 64
N_EXPERTS = 32
TOP_K = 4
D_FF = 1024
SWIGLU_LIMIT = 7.0
SWIGLU_ALPHA = 1.702
EPS = 1e-5

ATT_Q = ATT_HEADS * ATT_HEAD_DIM
HG_W = HG_HEADS * HG_D
COL_Q = 0
COL_KV = ATT_Q
COL_HQ = COL_KV + 4 * 128
COL_HF = COL_HQ + HG_W
COL_HI = COL_HF + HG_W
COL_HG = COL_HI + HG_W
PROJ_COLS = COL_HG + HG_W

LANES = 128
SUBLANES = 8
TS = 256
VMEM_LIMIT = 56 * 1024 * 1024
HG_SAFE_LEVELS = (32, 16, 8)
HG_DIAG = 8
HG_CHUNKS_PER_GROUP = 2
HG_EXP_CAP = 80.0

F32 = jnp.float32
BF16 = jnp.bfloat16
NEG_INF = float("-inf")


def _dot(a, b):
    return jnp.dot(a, b, preferred_element_type=F32)


def _dot_nt(a, b):
    return lax.dot_general(a, b, (((1,), (1,)), ((), ())), preferred_element_type=F32)


def _dot_tn(a, b):
    return lax.dot_general(a, b, (((0,), (0,)), ((), ())), preferred_element_type=F32)


def _sigmoid(x):
    return 1.0 / (1.0 + jnp.exp(-x))


def _prep_kernel(c_ref, w_ref, b_ref, o_ref):
    c = c_ref[...]
    cond = c * _sigmoid(c)
    o_ref[0] = jnp.dot(cond, w_ref[...], precision=lax.Precision.HIGHEST,
                       preferred_element_type=F32) + b_ref[...]


def _prep_call(c, w_ada, b_ada):
    B = c.shape[0]
    return pl.pallas_call(
        _prep_kernel,
        grid=(6,),
        in_specs=[
            pl.BlockSpec((B, D_MODEL), lambda j: (0, 0)),
            pl.BlockSpec((D_MODEL, D_MODEL), lambda j: (0, j)),
            pl.BlockSpec((1, D_MODEL), lambda j: (0, j)),
        ],
        out_specs=pl.BlockSpec((1, B, D_MODEL), lambda j: (j, 0, 0)),
        out_shape=jax.ShapeDtypeStruct((6, B, D_MODEL), F32),
        compiler_params=pltpu.CompilerParams(dimension_semantics=("arbitrary",)),
        name="adaln_mod",
    )(c, w_ada, b_ada)


def _tables_kernel(rb_ref, lbp_ref, bucket_ref, lb_ref, bias_ref):
    p = lbp_ref[...]
    e = jnp.exp(p - jnp.max(p, axis=0, keepdims=True))
    sm = e / jnp.sum(e, axis=0, keepdims=True)
    lb_ref[...] = sm[0:1]
    bucket = bucket_ref[...]
    for h in range(ATT_HEADS):
        acc = jnp.full(bucket.shape, NEG_INF, F32)
        for bk in range(N_BUCKETS):
            acc = jnp.where(bucket == bk, rb_ref[bk, h], acc)
        bias_ref[h] = acc


def _tables_call(rel_bias, hg_lb, bucket):
    return pl.pallas_call(
        _tables_kernel,
        in_specs=[
            pl.BlockSpec(memory_space=pltpu.SMEM),
            pl.BlockSpec(memory_space=pltpu.VMEM),
            pl.BlockSpec(memory_space=pltpu.VMEM),
        ],
        out_specs=[pl.BlockSpec(memory_space=pltpu.VMEM), pl.BlockSpec(memory_space=pltpu.VMEM)],
        out_shape=[jax.ShapeDtypeStruct((1, HG_W), F32),
                   jax.ShapeDtypeStruct((ATT_HEADS, BLOCK, 2 * BLOCK), F32)],
        name="tables",
    )(rel_bias, hg_lb, bucket)


def _bucket_table():
    i = np.arange(BLOCK, dtype=np.int32)[:, None]
    m = np.arange(2 * BLOCK, dtype=np.int32)[None, :]
    dist = i + BLOCK - m
    n = np.maximum(dist, 0)
    max_exact = N_BUCKETS // 2
    nf = np.maximum(n, 1).astype(np.float32)
    large = max_exact + (np.log(nf / np.float32(max_exact)) / np.float32(math.log(MAX_DISTANCE / max_exact))
                         * np.float32(N_BUCKETS - max_exact)).astype(np.int32)
    large = np.minimum(large, N_BUCKETS - 1)
    bucket = np.where(n < max_exact, n, large)
    return jnp.asarray(np.where((dist >= 0) & (dist < BLOCK), bucket, -1), jnp.int32)


def _rms(x):
    return x * lax.rsqrt(jnp.mean(x * x, axis=-1, keepdims=True) + EPS)


def _attention_block(proj_ref, kvprev_ref, bias_ref, sink_ref, mixed_ref, blk, first_tile):
    r0 = blk * BLOCK
    cur = proj_ref[r0:r0 + BLOCK, COL_KV:COL_KV + 512]
    if blk == 0:
        prev = kvprev_ref[...]
    else:
        prev = proj_ref[r0 - BLOCK:r0, COL_KV:COL_KV + 512]
    band = jnp.concatenate([prev, cur], axis=0)
    lane = lax.broadcasted_iota(jnp.int32, (2 * BLOCK, LANES), 1)
    lo = lane < ATT_HEAD_DIM
    col = lax.broadcasted_iota(jnp.int32, (1, 2 * BLOCK), 1)
    if blk == 0:
        pen = jnp.where((col < BLOCK) & first_tile, NEG_INF, 0.0)
    kbs, vbs = [], []
    for g in range(ATT_KV_HEADS):
        kd = band[:, g * 128:(g + 1) * 128]
        vd = band[:, 256 + g * 128:256 + (g + 1) * 128]
        kbs.append(jnp.concatenate([jnp.where(lo, kd, 0.0), jnp.where(lo, 0.0, kd)], axis=0).astype(BF16))
        vbs.append(jnp.concatenate([jnp.where(lo, vd, 0.0), jnp.where(lo, 0.0, vd)], axis=0).astype(BF16))
    n_pairs = ATT_HEADS // 2
    ss = []
    for g in range(ATT_KV_HEADS):
        qg = jnp.concatenate(
            [(proj_ref[r0:r0 + BLOCK, pair * 128:(pair + 1) * 128] * ATT_SCALE).astype(BF16)
             for pair in (2 * g, 2 * g + 1)], axis=0)
        sg = _dot_nt(qg, kbs[g])
        ss += [sg[0:BLOCK], sg[BLOCK:2 * BLOCK]]
    ps, invs = [], []
    for h in range(ATT_HEADS):
        sh = ss[h // 2][:, (h % 2) * 256:(h % 2 + 1) * 256] + bias_ref[h]
        if blk == 0:
            sh = sh + pen
        sink = sink_ref[h]
        mx = jnp.maximum(jnp.max(sh, axis=-1, keepdims=True), sink)
        p = jnp.exp(sh - mx)
        den = jnp.sum(p, axis=-1, keepdims=True) + jnp.exp(sink - mx)
        ps.append(p.astype(BF16))
        invs.append(1.0 / den)
    os_ = []
    for g in range(ATT_KV_HEADS):
        pg = jnp.concatenate([jnp.concatenate(ps[2 * pair:2 * pair + 2], axis=1)
                              for pair in (2 * g, 2 * g + 1)], axis=0)
        og = _dot(pg, vbs[g])
        os_ += [og[0:BLOCK], og[BLOCK:2 * BLOCK]]
    lane_o = lax.broadcasted_iota(jnp.int32, (BLOCK, LANES), 1)
    for pair in range(n_pairs):
        o = os_[pair] * jnp.where(lane_o < ATT_HEAD_DIM, invs[2 * pair], invs[2 * pair + 1])
        mixed_ref[r0:r0 + BLOCK, pair * 128:(pair + 1) * 128] = o.astype(BF16)


def _hgrn_scaled_operands(b, q, k):
    pairs = []
    for m in HG_SAFE_LEVELS:
        qs, ks = [], []
        for bi in range(CHUNK // m):
            rows = slice(bi * m, (bi + 1) * m)
            if bi % 2 == 1:
                ref = b[bi * m - 1:bi * m]
                qs.append(q[rows] * jnp.exp(b[rows] - ref))
                ks.append(jnp.zeros((m, HG_D), F32))
            else:
                ref = b[(bi + 1) * m - 1:(bi + 1) * m]
                qs.append(jnp.zeros((m, HG_D), F32))
                ks.append(k[rows] * jnp.exp(ref - b[rows]))
        pairs.append((m, jnp.concatenate(qs, axis=0).astype(BF16), jnp.concatenate(ks, axis=0).astype(BF16)))
    qs, ks = [], []
    for bi in range(CHUNK // HG_DIAG):
        rows = slice(bi * HG_DIAG, (bi + 1) * HG_DIAG)
        c = b[rows] if bi == 0 else b[rows] - b[bi * HG_DIAG - 1:bi * HG_DIAG]
        qs.append(q[rows] * jnp.exp(c))
        ks.append(k[rows] * jnp.exp(jnp.minimum(-c, HG_EXP_CAP)))
    pairs.append((0, jnp.concatenate(qs, axis=0).astype(BF16), jnp.concatenate(ks, axis=0).astype(BF16)))
    return pairs


def _hgrn_chunks(proj_ref, states, b_all, kk_all, qf_all, hgw_ref, mixed_ref, chunks, masks):
    units = [(ck, hd) for ck in chunks for hd in range(HG_HEADS)]
    rows = {u: slice(u[0] * CHUNK, (u[0] + 1) * CHUNK) for u in units}
    cols = {u: slice(u[1] * HG_D, (u[1] + 1) * HG_D) for u in units}
    bs = {u: b_all[rows[u], cols[u]] for u in units}
    qs = {u: qf_all[rows[u], cols[u]] for u in units}
    ks = {u: kk_all[rows[u], cols[u]] for u in units}
    vs = {u: proj_ref[rows[u], COL_HI + u[1] * HG_D:COL_HI + (u[1] + 1) * HG_D].astype(BF16) for u in units}
    operands = {u: _hgrn_scaled_operands(bs[u], qs[u], ks[u]) for u in units}
    qbs = {u: (qs[u] * jnp.exp(bs[u])).astype(BF16) for u in units}
    bends = {u: bs[u][CHUNK - 1:CHUNK] for u in units}
    kdecs = {u: (ks[u] * jnp.exp(bends[u] - bs[u])).astype(BF16) for u in units}
    prods = {u: [(m, _dot_nt(ql, kl)) for m, ql, kl in operands[u]] for u in units}
    kvs = {u: _dot_tn(vs[u], kdecs[u]) for u in units}
    intras = {}
    for u in units:
        sc = jnp.zeros((CHUNK, CHUNK), F32)
        for m, pr in prods[u]:
            sc = sc + jnp.where(masks[m], pr, 0.0)
        intras[u] = _dot(sc.astype(BF16), vs[u])
    for ck in chunks:
        inters = [_dot_nt(qbs[(ck, hd)], states[hd].astype(BF16)) for hd in range(HG_HEADS)]
        for hd in range(HG_HEADS):
            u = (ck, hd)
            states[hd] = states[hd] * jnp.exp(bends[u]) + kvs[u]
            gate = proj_ref[rows[u], COL_HG + hd * HG_D:COL_HG + (hd + 1) * HG_D]
            on = _rms(intras[u] + inters[hd]) * hgw_ref[...] * (gate * _sigmoid(gate))
            mixed_ref[rows[u], ATT_Q + hd * HG_D:ATT_Q + (hd + 1) * HG_D] = on.astype(BF16)


def _hgrn_masks():
    r = lax.broadcasted_iota(jnp.int32, (CHUNK, CHUNK), 0)
    c = lax.broadcasted_iota(jnp.int32, (CHUNK, CHUNK), 1)
    masks = {}
    for m in HG_SAFE_LEVELS:
        masks[m] = ((r // (2 * m)) == (c // (2 * m))) & (((r // m) % 2) == 1) & (((c // m) % 2) == 0)
    masks[0] = ((r // HG_DIAG) == (c // HG_DIAG)) & (c <= r)
    return masks


def _mix_kernel(sink_ref, x_ref, mod_ref, gmix_ref, win_ref, bin_ref, bias_ref, lb_ref, hgw_ref,
                wout_ref, bout_ref, gffn_ref, wr_ref, br_ref, ltri_ref, utri_ref,
                x1_ref, h2_ref, idx_ref, wts_ref, rank_ref, cnt_ref,
                proj_ref, mixed_ref, kvprev_ref, state_ref, run_ref, win_sc, wout_sc):
    bi = pl.program_id(0)
    j = pl.program_id(1)

    @pl.when(j == 0)
    def _():
        kvprev_ref[...] = jnp.zeros_like(kvprev_ref)
        state_ref[...] = jnp.zeros_like(state_ref)

    @pl.when((j == 0) & (bi == 0))
    def _():
        run_ref[...] = jnp.zeros_like(run_ref)
        win_sc[...] = win_ref[...]
        wout_sc[...] = wout_ref[...]

    x = x_ref[0]
    mod = mod_ref[0]
    h = _rms(x) * (gmix_ref[...] * (1.0 + mod[1:2])) + mod[0:1]
    hb = h.astype(BF16)

    def project(c0, c1):
        proj_ref[:, c0:c1] = _dot(hb, win_sc[:, c0:c1]) + bin_ref[:, c0:c1]

    project(0, COL_HQ)
    for blk in range(TS // BLOCK):
        _attention_block(proj_ref, kvprev_ref, bias_ref, sink_ref, mixed_ref, blk, j == 0)
        project(COL_HQ + blk * 1024, COL_HQ + (blk + 1) * 1024)
    kvprev_ref[...] = proj_ref[TS - BLOCK:TS, COL_KV:COL_KV + 512]

    lbv = lb_ref[...]
    fl = lbv + (1.0 - lbv) * _sigmoid(proj_ref[:, COL_HF:COL_HF + HG_W])
    g = jnp.log(fl)
    g_hi = g.astype(BF16)
    r1 = g - g_hi.astype(F32)
    g_mid = r1.astype(BF16)
    g_lo = (r1 - g_mid.astype(F32)).astype(BF16)
    ltri = ltri_ref[...]
    b_all = _dot(ltri, g_hi) + _dot(ltri, g_mid) + _dot(ltri, g_lo)

    masks = _hgrn_masks()
    qr = proj_ref[:, COL_HQ:COL_HQ + HG_W]
    qf_all = qr * _sigmoid(qr)
    kk_all = 1.0 - fl
    states = [state_ref[hd] for hd in range(HG_HEADS)]
    n_chunks = TS // CHUNK
    for c0 in range(0, n_chunks, HG_CHUNKS_PER_GROUP):
        _hgrn_chunks(proj_ref, states, b_all, kk_all, qf_all, hgw_ref, mixed_ref,
                     range(c0, c0 + HG_CHUNKS_PER_GROUP), masks)
    for hd in range(HG_HEADS):
        state_ref[hd] = states[hd]

    y = _dot(mixed_ref[...], wout_sc[...]) + bout_ref[...]
    x1 = x + mod[2:3] * y
    x1_ref[0] = x1
    h2 = _rms(x1) * (gffn_ref[...] * (1.0 + mod[4:5])) + mod[3:4]
    for s in range(SUBLANES):
        h2_ref[pl.ds(s, TS, stride=SUBLANES), :] = h2[:, s * LANES:(s + 1) * LANES]

    h_hi = h2.astype(BF16)
    h_lo = (h2 - h_hi.astype(F32)).astype(BF16)
    wr = wr_ref[...]
    t1 = _dot_nt(wr, h_hi)
    t2 = _dot_nt(wr[0:N_EXPERTS], h_lo)
    logits = t1[0:N_EXPERTS] + t1[N_EXPERTS:2 * N_EXPERTS] + t2 + br_ref[:, 0:1]

    eidx = lax.broadcasted_iota(jnp.int32, (N_EXPERTS, TS), 0)
    l = logits
    vals, ohs, ids = [], [], []
    for _ in range(TOP_K):
        mx = jnp.max(l, axis=0, keepdims=True)
        ik = jnp.min(jnp.where(l == mx, eidx, N_EXPERTS), axis=0, keepdims=True)
        oh = eidx == ik
        vals.append(mx)
        ids.append(ik)
        ohs.append(oh)
        l = jnp.where(oh, NEG_INF, l)
    es = [jnp.exp(v - vals[0]) for v in vals]
    den = es[0] + es[1] + es[2] + es[3]
    run = run_ref[...]
    base = run[:, 0:1]
    utri = utri_ref[...]
    ranks = []
    for kq in range(TOP_K):
        ohf = jnp.where(ohs[kq], 1.0, 0.0)
        pref = _dot(ohf.astype(BF16), utri)
        ranks.append(jnp.sum(jnp.where(ohs[kq], base + pref, 0.0), axis=0, keepdims=True))
        base = base + jnp.sum(ohf, axis=1, keepdims=True)
    run_new = jnp.broadcast_to(base, run.shape)
    run_ref[...] = run_new
    cnt_ref[...] = run_new.astype(jnp.int32)
    idx_ref[...] = jnp.concatenate(ids, axis=0)
    wts_ref[...] = jnp.concatenate([e / den for e in es], axis=0)
    rank_ref[...] = jnp.concatenate(ranks, axis=0).astype(jnp.int32)


def _mix_call(sinks, x, mod, gmix, win, bin_, bias, lb, hgw, wout, bout, gffn, wr, br, ltri, utri):
    B, S, D = x.shape
    N = B * S
    nj = S // TS
    const2 = lambda b, j: (0, 0)
    tok = lambda b, j: (0, b * nj + j)
    in_specs = [
        pl.BlockSpec(memory_space=pltpu.SMEM),
        pl.BlockSpec((1, TS, D), lambda b, j: (b, j, 0)),
        pl.BlockSpec((1, 8, D), lambda b, j: (b, 0, 0)),
        pl.BlockSpec((1, D), const2),
        pl.BlockSpec((D, PROJ_COLS), const2),
        pl.BlockSpec((1, PROJ_COLS), const2),
        pl.BlockSpec((ATT_HEADS, BLOCK, 2 * BLOCK), lambda b, j: (0, 0, 0)),
        pl.BlockSpec((1, HG_W), const2),
        pl.BlockSpec((1, HG_D), const2),
        pl.BlockSpec((D, D), const2),
        pl.BlockSpec((1, D), const2),
        pl.BlockSpec((1, D), const2),
        pl.BlockSpec((2 * N_EXPERTS, D), const2),
        pl.BlockSpec((N_EXPERTS, LANES), const2),
        pl.BlockSpec((TS, TS), const2),
        pl.BlockSpec((TS, TS), const2),
    ]
    out_specs = [
        pl.BlockSpec((1, TS, D), lambda b, j: (b, j, 0)),
        pl.BlockSpec((TS * SUBLANES, LANES), lambda b, j: (b * nj + j, 0)),
        pl.BlockSpec((TOP_K, TS), tok),
        pl.BlockSpec((TOP_K, TS), tok),
        pl.BlockSpec((TOP_K, TS), tok),
        pl.BlockSpec((N_EXPERTS, LANES), const2),
    ]
    out_shape = [
        jax.ShapeDtypeStruct((B, S, D), F32),
        jax.ShapeDtypeStruct((N * SUBLANES, LANES), F32),
        jax.ShapeDtypeStruct((TOP_K, N), jnp.int32),
        jax.ShapeDtypeStruct((TOP_K, N), F32),
        jax.ShapeDtypeStruct((TOP_K, N), jnp.int32),
        jax.ShapeDtypeStruct((N_EXPERTS, LANES), jnp.int32),
    ]
    scratch = [
        pltpu.VMEM((TS, PROJ_COLS), F32),
        pltpu.VMEM((TS, D), BF16),
        pltpu.VMEM((BLOCK, 512), F32),
        pltpu.VMEM((HG_HEADS, HG_D, HG_D), F32),
        pltpu.VMEM((N_EXPERTS, LANES), F32),
        pltpu.VMEM((D, PROJ_COLS), BF16),
        pltpu.VMEM((D, D), BF16),
    ]
    return pl.pallas_call(
        _mix_kernel,
        grid=(B, nj),
        in_specs=in_specs,
        out_specs=out_specs,
        out_shape=out_shape,
        scratch_shapes=scratch,
        compiler_params=pltpu.CompilerParams(
            dimension_semantics=("arbitrary", "arbitrary"), vmem_limit_bytes=VMEM_LIMIT),
        name="mix_router",
    )(sinks, x, mod, gmix, win, bin_, bias, lb, hgw, wout, bout, gffn, wr, br, ltri, utri)


TM = 512
FF_CHUNKS = 4
TD = 256
ROW = SUBLANES


def _row_copy(src, src_row, dst, dst_row, sem):
    return pltpu.make_async_copy(src.at[pl.ds(pl.multiple_of(src_row * ROW, ROW), ROW)],
                                 dst.at[pl.ds(pl.multiple_of(dst_row * ROW, ROW), ROW)], sem)


def _col_from_row(w_row, n):
    rr = lax.broadcasted_iota(jnp.int32, (n, n), 0)
    cc = lax.broadcasted_iota(jnp.int32, (n, n), 1)
    return jnp.sum(jnp.where(rr == cc, w_row, 0.0), axis=1, keepdims=True)


def _rows_to_matrix(ref, base, n):
    return jnp.concatenate([ref[pl.ds(base + s, n, stride=ROW), :] for s in range(ROW)], axis=1)


def _dispatch_kernel(tend_ref, pos_ref, h2_hbm, xs_hbm, zbuf, hbuf, sem, lsem, zsem):
    i = pl.program_id(0)
    n = pl.num_programs(0)
    n_tiles = xs_hbm.shape[0] // (TM * ROW)

    def fill_tile(tile):
        return pltpu.make_async_copy(
            zbuf, xs_hbm.at[pl.ds(pl.multiple_of(tile * (TM * ROW), TM * ROW), TM * ROW)], zsem.at[0])

    def padded(e):
        return tend_ref[e] > (tend_ref[e - 1] if e > 0 else 0), tend_ref[e] - 1

    def unused(u):
        tile = tend_ref[N_EXPERTS - 1] + u
        return tile < n_tiles, tile

    @pl.when(i == 0)
    def _():
        zbuf[...] = jnp.zeros_like(zbuf)
        sites = [padded(e) for e in range(N_EXPERTS)] + [unused(u) for u in range(N_EXPERTS)]
        for cond, tile in sites:
            pl.when(cond)(lambda tile=tile: fill_tile(tile).start())
        for cond, tile in sites:
            pl.when(cond)(lambda tile=tile: fill_tile(tile).wait())

    def load(step):
        s = step % 3
        return pltpu.make_async_copy(
            h2_hbm.at[pl.ds(pl.multiple_of(step * (TD * ROW), TD * ROW), TD * ROW)], hbuf.at[s], lsem.at[s])

    def wait_rows(step):
        s = step % 3
        for _ in range(TOP_K):
            pltpu.make_async_copy(hbuf.at[s], xs_hbm.at[pl.ds(0, TD * ROW)], sem.at[s]).wait()

    pl.when(i == 0)(lambda: load(i).start())
    pl.when(i >= 2)(lambda: wait_rows(i - 2))
    pl.when(i + 1 < n)(lambda: load(i + 1).start())
    load(i).wait()
    slot = i % 3
    for q in range(TOP_K * TD):
        _row_copy(hbuf.at[slot], q % TD, xs_hbm, pos_ref[0, 0, q], sem.at[slot]).start(priority=q % 2)

    @pl.when(i == n - 1)
    def _():
        pl.when(i >= 1)(lambda: wait_rows(i - 1))
        wait_rows(i)


def _dispatch_call(tile_end, pos_t, h2rows, n_tiles):
    nblk = pos_t.shape[0]
    grid_spec = pltpu.PrefetchScalarGridSpec(
        num_scalar_prefetch=1,
        grid=(nblk,),
        in_specs=[
            pl.BlockSpec((1, 1, TOP_K * TD), lambda i, tend: (i, 0, 0), memory_space=pltpu.SMEM),
            pl.BlockSpec(memory_space=pl.ANY),
        ],
        out_specs=pl.BlockSpec(memory_space=pl.ANY),
        scratch_shapes=[
            pltpu.VMEM((TM * ROW, LANES), F32),
            pltpu.VMEM((3, TD * ROW, LANES), F32),
            pltpu.SemaphoreType.DMA((3,)),
            pltpu.SemaphoreType.DMA((3,)),
            pltpu.SemaphoreType.DMA((1,)),
        ],
    )
    return pl.pallas_call(
        _dispatch_kernel,
        grid_spec=grid_spec,
        out_shape=jax.ShapeDtypeStruct((n_tiles * TM * ROW, LANES), F32),
        compiler_params=pltpu.CompilerParams(dimension_semantics=("arbitrary",)),
        name="moe_dispatch",
    )(tile_end, pos_t, h2rows)


def _moe_kernel(te_ref, nt_ref, nxt_ref, par_ref, xs_ref, wgu_hbm, bgu_ref, wd_hbm, bd_ref, perm_ref, ys_ref,
                wgu_st, wd_st, wgu_bf, wd_bf, wsem):
    j = pl.program_id(0)
    nt = nt_ref[0]

    def weight_copies(e):
        s = par_ref[e]
        return (pltpu.make_async_copy(wgu_hbm.at[e], wgu_st.at[s], wsem.at[0, s]),
                pltpu.make_async_copy(wd_hbm.at[e], wd_st.at[s], wsem.at[1, s]))

    @pl.when(j == 0)
    def _():
        for cp in weight_copies(te_ref[0]):
            cp.start()

    @pl.when(j < nt)
    def _():
        e = te_ref[j]
        first = (j == 0) | (te_ref[jnp.maximum(j - 1, 0)] != e)

        @pl.when(first)
        def _():
            for cp in weight_copies(e):
                cp.wait()
            s = par_ref[e]
            perm = perm_ref[...]
            for g in range(2 * D_FF // 256):
                w = wgu_st[s, :, g * 256:(g + 1) * 256].astype(BF16)
                wgu_bf[:, g * 256:(g + 1) * 256] = _dot(w, perm).astype(BF16)
            wd_bf[...] = wd_st[s].astype(BF16)
            nxt = nxt_ref[e]

            @pl.when(nxt >= 0)
            def _():
                for cp in weight_copies(nxt):
                    cp.start()

        h = _rows_to_matrix(xs_ref, 0, TM).astype(BF16)
        gw = 2 * D_FF // FF_CHUNKS
        hw = D_FF // FF_CHUNKS

        def up_proj(c):
            return _dot(h, wgu_bf[:, c * gw:(c + 1) * gw]) + bgu_ref[0, :, c * gw:(c + 1) * gw]

        def activate(gu):
            parts = []
            for g in range(gw // 256):
                gate = jnp.minimum(gu[:, g * 256:g * 256 + 128], SWIGLU_LIMIT)
                up = jnp.clip(gu[:, g * 256 + 128:(g + 1) * 256], -SWIGLU_LIMIT, SWIGLU_LIMIT)
                parts.append(((up + 1.0) * gate * _sigmoid(SWIGLU_ALPHA * gate)).astype(BF16))
            return jnp.concatenate(parts, axis=1)

        y = bd_ref[0]
        gu_next = up_proj(0)
        for c in range(FF_CHUNKS):
            gu = gu_next
            if c + 1 < FF_CHUNKS:
                gu_next = up_proj(c + 1)
            y = y + _dot(activate(gu), wd_bf[c * hw:(c + 1) * hw, :])
        for s in range(ROW):
            ys_ref[pl.ds(s, TM, stride=ROW), :] = y[:, s * LANES:(s + 1) * LANES]

    @pl.when(j >= nt)
    def _():
        ys_ref[...] = jnp.zeros_like(ys_ref)


def _moe_call(te, nt, nxt, par, xs, wgu, bgu, wd, bd, perm):
    n_tiles = xs.shape[0] // (TM * ROW)
    tile = lambda j, te, nt, nxt, par: (jnp.minimum(j, nt[0] - 1), 0)
    exp = lambda j, te, nt, nxt, par: (te[jnp.minimum(j, nt[0] - 1)], 0, 0)
    grid_spec = pltpu.PrefetchScalarGridSpec(
        num_scalar_prefetch=4,
        grid=(n_tiles,),
        in_specs=[
            pl.BlockSpec((TM * ROW, LANES), tile),
            pl.BlockSpec(memory_space=pl.ANY),
            pl.BlockSpec((1, 1, 2 * D_FF), exp),
            pl.BlockSpec(memory_space=pl.ANY),
            pl.BlockSpec((1, 1, D_MODEL), exp),
            pl.BlockSpec((256, 256), lambda j, te, nt, nxt, par: (0, 0)),
        ],
        out_specs=pl.BlockSpec((TM * ROW, LANES), lambda j, te, nt, nxt, par: (j, 0)),
        scratch_shapes=[
            pltpu.VMEM((2, D_MODEL, 2 * D_FF), F32),
            pltpu.VMEM((2, D_FF, D_MODEL), F32),
            pltpu.VMEM((D_MODEL, 2 * D_FF), BF16),
            pltpu.VMEM((D_FF, D_MODEL), BF16),
            pltpu.SemaphoreType.DMA((2, 2)),
        ],
    )
    return pl.pallas_call(
        _moe_kernel,
        grid_spec=grid_spec,
        out_shape=jax.ShapeDtypeStruct(xs.shape, F32),
        compiler_params=pltpu.CompilerParams(
            dimension_semantics=("arbitrary",), vmem_limit_bytes=VMEM_LIMIT),
        name="moe_experts",
    )(te, nt, nxt, par, xs, wgu, bgu, wd, bd, perm)


def _final_kernel(pnext_ref, pcur_ref, wts_ref, x1_ref, mod_ref, gfin_ref, ys_hbm, out_ref, gbuf, sem):
    i = pl.program_id(0)
    n = pl.num_programs(0)

    def issue(pref, slot):
        for q in range(TOP_K * TD):
            _row_copy(ys_hbm, pref[0, 0, q], gbuf.at[slot], q, sem.at[slot]).start(priority=q % 2)

    pl.when(i == 0)(lambda: issue(pcur_ref, 0))
    pl.when(i + 1 < n)(lambda: issue(pnext_ref, (i + 1) % 2))
    slot = i % 2
    pltpu.make_async_copy(ys_hbm.at[pl.ds(0, TOP_K * TD * ROW)], gbuf.at[slot], sem.at[slot]).wait()
    gb = gbuf.at[slot]
    acc = jnp.zeros((TD, D_MODEL), F32)
    for kq in range(TOP_K):
        v = _rows_to_matrix(gb, kq * TD * ROW, TD)
        acc = acc + _col_from_row(wts_ref[kq:kq + 1, :], TD) * v
    xf = x1_ref[...] + mod_ref[0][5:6] * acc
    out_ref[...] = _rms(xf) * gfin_ref[...]


def _final_call(pos_t, wts, x1, mod, gfin, ys, S):
    N, D = x1.shape
    nblk = N // TD
    per_b = S // TD
    return pl.pallas_call(
        _final_kernel,
        grid=(nblk,),
        in_specs=[
            pl.BlockSpec((1, 1, TOP_K * TD), lambda i: (jnp.minimum(i + 1, nblk - 1), 0, 0),
                         memory_space=pltpu.SMEM),
            pl.BlockSpec((1, 1, TOP_K * TD), lambda i: (i, 0, 0), memory_space=pltpu.SMEM),
            pl.BlockSpec((TOP_K, TD), lambda i: (0, i)),
            pl.BlockSpec((TD, D), lambda i: (i, 0)),
            pl.BlockSpec((1, 8, D), lambda i: (i // per_b, 0, 0)),
            pl.BlockSpec((1, D), lambda i: (0, 0)),
            pl.BlockSpec(memory_space=pl.ANY),
        ],
        out_specs=pl.BlockSpec((TD, D), lambda i: (i, 0)),
        out_shape=jax.ShapeDtypeStruct((N, D), F32),
        scratch_shapes=[
            pltpu.VMEM((2, TOP_K * TD * ROW, LANES), F32),
            pltpu.SemaphoreType.DMA((2,)),
        ],
        compiler_params=pltpu.CompilerParams(
            dimension_semantics=("arbitrary",), vmem_limit_bytes=VMEM_LIMIT),
        name="combine_norm",
    )(pos_t, pos_t, wts, x1, mod, gfin, ys)


def _route_plan(idx, rank, cnt, N):
    n_tiles = N * TOP_K // TM + N_EXPERTS
    counts = cnt[:, 0]
    ntile = (counts + TM - 1) // TM
    tile_end = jnp.cumsum(ntile).astype(jnp.int32)
    row_start = (tile_end - ntile) * TM
    nt = tile_end[-1:]
    experts = jnp.arange(N_EXPERTS, dtype=jnp.int32)
    pos = jnp.sum(jnp.where(idx[..., None] == experts, row_start, 0), axis=-1) + rank
    te = jnp.minimum(jnp.sum(jnp.arange(n_tiles)[:, None] >= tile_end[None, :], axis=1),
                     N_EXPERTS - 1).astype(jnp.int32)
    pos_t = jnp.transpose(pos.reshape(TOP_K, N // TD, TD), (1, 0, 2)).reshape(N // TD, 1, TOP_K * TD)
    nonempty = ntile > 0
    par = ((jnp.cumsum(nonempty) - 1) % 2).astype(jnp.int32)
    later = nonempty[None, :] & (experts[None, :] > experts[:, None])
    nxt = jnp.min(jnp.where(later, experts[None, :], N_EXPERTS), axis=1)
    nxt = jnp.where(nxt == N_EXPERTS, -1, nxt).astype(jnp.int32)
    return te, nt, nxt, par, tile_end, pos_t, n_tiles


def _tri_constants():
    r = np.arange(TS)[:, None]
    c = np.arange(TS)[None, :]
    ltri = ((r // CHUNK) == (c // CHUNK)) & (c <= r)
    utri = r < c
    return jnp.asarray(ltri, BF16), jnp.asarray(utri, BF16)


def kernel(x, c, w_ada, b_ada, g_mix, w_in, b_in, attn_sinks, rel_bias, hg_lb, hg_norm_w, w_out, b_out, g_ffn, w_router, b_router, w_gate_up, b_gate_up, w_down, b_down, g_final):
    B, S, D = x.shape
    N = B * S
    mod6 = _prep_call(c, w_ada[0], b_ada)
    mod = jnp.pad(jnp.transpose(mod6, (1, 0, 2)), ((0, 0), (0, 2), (0, 0)))
    lb, bias = _tables_call(rel_bias, hg_lb, _bucket_table())

    wi, bi_ = w_in[0], b_in[0]
    def cols(a):
        aq, ak, av, rest = a[..., :512], a[..., 512:640], a[..., 640:768], a[..., 768:]
        k0, k1, v0, v1 = ak[..., :64], ak[..., 64:], av[..., :64], av[..., 64:]
        return jnp.concatenate([aq, k0, k0, k1, k1, v0, v0, v1, v1, rest], axis=-1)
    win = cols(wi).astype(BF16)
    bin_ = cols(bi_)[None, :]
    wrt = jnp.transpose(w_router[0])
    wr_hi = wrt.astype(BF16)
    wr_lo = (wrt - wr_hi.astype(F32)).astype(BF16)
    wr = jnp.concatenate([wr_hi, wr_lo], axis=0)
    br = jnp.broadcast_to(b_router[0][:, None], (N_EXPERTS, LANES))
    ltri, utri = _tri_constants()

    x1, h2rows, idx, wts, rank, cnt = _mix_call(
        attn_sinks[0], x, mod, g_mix, win, bin_, bias, lb, hg_norm_w, w_out[0].astype(BF16),
        b_out, g_ffn, wr, br, ltri, utri)

    te, nt, nxt, par, tile_end, pos_t, n_tiles = _route_plan(idx, rank, cnt, N)
    ii = np.arange(128)
    perm_np = np.zeros((256, 256), np.float32)
    perm_np[2 * ii, ii] = 1.0
    perm_np[2 * ii + 1, 128 + ii] = 1.0
    perm = jnp.asarray(perm_np, BF16)
    bgu = b_gate_up[0].reshape(N_EXPERTS, D_FF // LANES, LANES, 2)
    bgu = jnp.transpose(bgu, (0, 1, 3, 2)).reshape(N_EXPERTS, 1, 2 * D_FF)
    xs = _dispatch_call(tile_end, pos_t, h2rows, n_tiles)
    ys = _moe_call(te, nt, nxt, par, xs, w_gate_up[0], bgu, w_down[0], b_down[0][:, None, :], perm)
    out = _final_call(pos_t, wts, x1.reshape(N, D), mod, g_final[None, :], ys, S)
    return out.reshape(B, S, D)
```

```python
import functools
import math

import numpy as np
import jax
import jax.numpy as jnp
from jax import lax
from jax.experimental import pallas as pl
from jax.experimental.pallas import tpu as pltpu

D_MODEL = 1024
ATT_HEAD_DIM = 64
ATT_HEADS = 8
ATT_KV_HEADS = 2
ATT_SCALE = ATT_HEAD_DIM ** -0.5
BLOCK = 128
N_BUCKETS = 32
MAX_DISTANCE = 128
HG_HEADS = 4
HG_D = 128
CHUNK = 64
N_EXPERTS = 32
TOP_K = 4
D_FF = 1024
SWIGLU_LIMIT = 7.0
SWIGLU_ALPHA = 1.702
EPS = 1e-5

ATT_Q = ATT_HEADS * ATT_HEAD_DIM
HG_W = HG_HEADS * HG_D
COL_Q = 0
COL_KV = ATT_Q
COL_HQ = COL_KV + 4 * 128
COL_HF = COL_HQ + HG_W
COL_HI = COL_HF + HG_W
COL_HG = COL_HI + HG_W
PROJ_COLS = COL_HG + HG_W

LANES = 128
SUBLANES = 8
TS = 256
VMEM_LIMIT = 56 * 1024 * 1024
HG_SAFE_LEVELS = (32, 16, 8)
HG_DIAG = 8
HG_CHUNKS_PER_GROUP = 2
HG_EXP_CAP = 80.0

F32 = jnp.float32
BF16 = jnp.bfloat16
NEG_INF = float("-inf")


def _dot(a, b):
    return jnp.dot(a, b, preferred_element_type=F32)


def _dot_nt(a, b):
    return lax.dot_general(a, b, (((1,), (1,)), ((), ())), preferred_element_type=F32)


def _dot_tn(a, b):
    return lax.dot_general(a, b, (((0,), (0,)), ((), ())), preferred_element_type=F32)


def _sigmoid(x):
    return 1.0 / (1.0 + jnp.exp(-x))


def _prep_kernel(c_ref, w_ref, b_ref, o_ref):
    c = c_ref[...]
    cond = c * _sigmoid(c)
    o_ref[0] = jnp.dot(cond, w_ref[...], precision=lax.Precision.HIGHEST,
                       preferred_element_type=F32) + b_ref[...]


def _prep_call(c, w_ada, b_ada):
    B = c.shape[0]
    return pl.pallas_call(
        _prep_kernel,
        grid=(6,),
        in_specs=[
            pl.BlockSpec((B, D_MODEL), lambda j: (0, 0)),
            pl.BlockSpec((D_MODEL, D_MODEL), lambda j: (0, j)),
            pl.BlockSpec((1, D_MODEL), lambda j: (0, j)),
        ],
        out_specs=pl.BlockSpec((1, B, D_MODEL), lambda j: (j, 0, 0)),
        out_shape=jax.ShapeDtypeStruct((6, B, D_MODEL), F32),
        compiler_params=pltpu.CompilerParams(dimension_semantics=("arbitrary",)),
        name="adaln_mod",
    )(c, w_ada, b_ada)


def _tables_kernel(rb_ref, lbp_ref, bucket_ref, lb_ref, bias_ref):
    p = lbp_ref[...]
    e = jnp.exp(p - jnp.max(p, axis=0, keepdims=True))
    sm = e / jnp.sum(e, axis=0, keepdims=True)
    lb_ref[...] = sm[0:1]
    bucket = bucket_ref[...]
    for h in range(ATT_HEADS):
        acc = jnp.full(bucket.shape, NEG_INF, F32)
        for bk in range(N_BUCKETS):
            acc = jnp.where(bucket == bk, rb_ref[bk, h], acc)
        bias_ref[h] = acc


def _tables_call(rel_bias, hg_lb, bucket):
    return pl.pallas_call(
        _tables_kernel,
        in_specs=[
            pl.BlockSpec(memory_space=pltpu.SMEM),
            pl.BlockSpec(memory_space=pltpu.VMEM),
            pl.BlockSpec(memory_space=pltpu.VMEM),
        ],
        out_specs=[pl.BlockSpec(memory_space=pltpu.VMEM), pl.BlockSpec(memory_space=pltpu.VMEM)],
        out_shape=[jax.ShapeDtypeStruct((1, HG_W), F32),
                   jax.ShapeDtypeStruct((ATT_HEADS, BLOCK, 2 * BLOCK), F32)],
        name="tables",
    )(rel_bias, hg_lb, bucket)


def _bucket_table():
    i = np.arange(BLOCK, dtype=np.int32)[:, None]
    m = np.arange(2 * BLOCK, dtype=np.int32)[None, :]
    dist = i + BLOCK - m
    n = np.maximum(dist, 0)
    max_exact = N_BUCKETS // 2
    nf = np.maximum(n, 1).astype(np.float32)
    large = max_exact + (np.log(nf / np.float32(max_exact)) / np.float32(math.log(MAX_DISTANCE / max_exact))
                         * np.float32(N_BUCKETS - max_exact)).astype(np.int32)
    large = np.minimum(large, N_BUCKETS - 1)
    bucket = np.where(n < max_exact, n, large)
    return jnp.asarray(np.where((dist >= 0) & (dist < BLOCK), bucket, -1), jnp.int32)


def _rms(x):
    return x * lax.rsqrt(jnp.mean(x * x, axis=-1, keepdims=True) + EPS)


def _attention_block(proj_ref, kvprev_ref, bias_ref, sink_ref, mixed_ref, blk, first_tile):
    r0 = blk * BLOCK
    cur = proj_ref[r0:r0 + BLOCK, COL_KV:COL_KV + 512]
    if blk == 0:
        prev = kvprev_ref[...]
    else:
        prev = proj_ref[r0 - BLOCK:r0, COL_KV:COL_KV + 512]
    band = jnp.concatenate([prev, cur], axis=0)
    lane = lax.broadcasted_iota(jnp.int32, (2 * BLOCK, LANES), 1)
    lo = lane < ATT_HEAD_DIM
    col = lax.broadcasted_iota(jnp.int32, (1, 2 * BLOCK), 1)
    if blk == 0:
        pen = jnp.where((col < BLOCK) & first_tile, NEG_INF, 0.0)
    kbs, vbs = [], []
    for g in range(ATT_KV_HEADS):
        kd = band[:, g * 128:(g + 1) * 128]
        vd = band[:, 256 + g * 128:256 + (g + 1) * 128]
        kbs.append(jnp.concatenate([jnp.where(lo, kd, 0.0), jnp.where(lo, 0.0, kd)], axis=0).astype(BF16))
        vbs.append(jnp.concatenate([jnp.where(lo, vd, 0.0), jnp.where(lo, 0.0, vd)], axis=0).astype(BF16))
    n_pairs = ATT_HEADS // 2
    ss = []
    for g in range(ATT_KV_HEADS):
        qg = jnp.concatenate(
            [(proj_ref[r0:r0 + BLOCK, pair * 128:(pair + 1) * 128] * ATT_SCALE).astype(BF16)
             for pair in (2 * g, 2 * g + 1)], axis=0)
        sg = _dot_nt(qg, kbs[g])
        ss += [sg[0:BLOCK], sg[BLOCK:2 * BLOCK]]
    ps, invs = [], []
    for h in range(ATT_HEADS):
        sh = ss[h // 2][:, (h % 2) * 256:(h % 2 + 1) * 256] + bias_ref[h]
        if blk == 0:
            sh = sh + pen
        sink = sink_ref[h]
        mx = jnp.maximum(jnp.max(sh, axis=-1, keepdims=True), sink)
        p = jnp.exp(sh - mx)
        den = jnp.sum(p, axis=-1, keepdims=True) + jnp.exp(sink - mx)
        ps.append(p.astype(BF16))
        invs.append(1.0 / den)
    os_ = []
    for g in range(ATT_KV_HEADS):
        pg = jnp.concatenate([jnp.concatenate(ps[2 * pair:2 * pair + 2], axis=1)
                              for pair in (2 * g, 2 * g + 1)], axis=0)
        og = _dot(pg, vbs[g])
        os_ += [og[0:BLOCK], og[BLOCK:2 * BLOCK]]
    lane_o = lax.broadcasted_iota(jnp.int32, (BLOCK, LANES), 1)
    for pair in range(n_pairs):
        o = os_[pair] * jnp.where(lane_o < ATT_HEAD_DIM, invs[2 * pair], invs[2 * pair + 1])
        mixed_ref[r0:r0 + BLOCK, pair * 128:(pair + 1) * 128] = o.astype(BF16)


def _hgrn_scaled_operands(b, q, k):
    pairs = []
    for m in HG_SAFE_LEVELS:
        qs, ks = [], []
        for bi in range(CHUNK // m):
            rows = slice(bi * m, (bi + 1) * m)
            if bi % 2 == 1:
                ref = b[bi * m - 1:bi * m]
                qs.append(q[rows] * jnp.exp(b[rows] - ref))
                ks.append(jnp.zeros((m, HG_D), F32))
            else:
                ref = b[(bi + 1) * m - 1:(bi + 1) * m]
                qs.append(jnp.zeros((m, HG_D), F32))
                ks.append(k[rows] * jnp.exp(ref - b[rows]))
        pairs.append((m, jnp.concatenate(qs, axis=0).astype(BF16), jnp.concatenate(ks, axis=0).astype(BF16)))
    qs, ks = [], []
    for bi in range(CHUNK // HG_DIAG):
        rows = slice(bi * HG_DIAG, (bi + 1) * HG_DIAG)
        c = b[rows] if bi == 0 else b[rows] - b[bi * HG_DIAG - 1:bi * HG_DIAG]
        qs.append(q[rows] * jnp.exp(c))
        ks.append(k[rows] * jnp.exp(jnp.minimum(-c, HG_EXP_CAP)))
    pairs.append((0, jnp.concatenate(qs, axis=0).astype(BF16), jnp.concatenate(ks, axis=0).astype(BF16)))
    return pairs


def _hgrn_chunks(proj_ref, states, b_all, kk_all, qf_all, hgw_ref, mixed_ref, chunks, masks):
    units = [(ck, hd) for ck in chunks for hd in range(HG_HEADS)]
    rows = {u: slice(u[0] * CHUNK, (u[0] + 1) * CHUNK) for u in units}
    cols = {u: slice(u[1] * HG_D, (u[1] + 1) * HG_D) for u in units}
    bs = {u: b_all[rows[u], cols[u]] for u in units}
    qs = {u: qf_all[rows[u], cols[u]] for u in units}
    ks = {u: kk_all[rows[u], cols[u]] for u in units}
    vs = {u: proj_ref[rows[u], COL_HI + u[1] * HG_D:COL_HI + (u[1] + 1) * HG_D].astype(BF16) for u in units}
    operands = {u: _hgrn_scaled_operands(bs[u], qs[u], ks[u]) for u in units}
    qbs = {u: (qs[u] * jnp.exp(bs[u])).astype(BF16) for u in units}
    bends = {u: bs[u][CHUNK - 1:CHUNK] for u in units}
    kdecs = {u: (ks[u] * jnp.exp(bends[u] - bs[u])).astype(BF16) for u in units}
    prods = {u: [(m, _dot_nt(ql, kl)) for m, ql, kl in operands[u]] for u in units}
    kvs = {u: _dot_tn(vs[u], kdecs[u]) for u in units}
    intras = {}
    for u in units:
        sc = jnp.zeros((CHUNK, CHUNK), F32)
        for m, pr in prods[u]:
            sc = sc + jnp.where(masks[m], pr, 0.0)
        intras[u] = _dot(sc.astype(BF16), vs[u])
    for ck in chunks:
        inters = [_dot_nt(qbs[(ck, hd)], states[hd].astype(BF16)) for hd in range(HG_HEADS)]
        for hd in range(HG_HEADS):
            u = (ck, hd)
            states[hd] = states[hd] * jnp.exp(bends[u]) + kvs[u]
            gate = proj_ref[rows[u], COL_HG + hd * HG_D:COL_HG + (hd + 1) * HG_D]
            on = _rms(intras[u] + inters[hd]) * hgw_ref[...] * (gate * _sigmoid(gate))
            mixed_ref[rows[u], ATT_Q + hd * HG_D:ATT_Q + (hd + 1) * HG_D] = on.astype(BF16)


def _hgrn_masks():
    r = lax.broadcasted_iota(jnp.int32, (CHUNK, CHUNK), 0)
    c = lax.broadcasted_iota(jnp.int32, (CHUNK, CHUNK), 1)
    masks = {}
    for m in HG_SAFE_LEVELS:
        masks[m] = ((r // (2 * m)) == (c // (2 * m))) & (((r // m) % 2) == 1) & (((c // m) % 2) == 0)
    masks[0] = ((r // HG_DIAG) == (c // HG_DIAG)) & (c <= r)
    return masks


def _mix_kernel(sink_ref, x_ref, mod_ref, gmix_ref, win_ref, bin_ref, bias_ref, lb_ref, hgw_ref,
                wout_ref, bout_ref, gffn_ref, wr_ref, br_ref, ltri_ref, utri_ref,
                x1_ref, h2_ref, idx_ref, wts_ref, rank_ref, cnt_ref,
                proj_ref, mixed_ref, kvprev_ref, state_ref, run_ref, win_sc, wout_sc):
    bi = pl.program_id(0)
    j = pl.program_id(1)

    @pl.when(j == 0)
    def _():
        kvprev_ref[...] = jnp.zeros_like(kvprev_ref)
        state_ref[...] = jnp.zeros_like(state_ref)

    @pl.when((j == 0) & (bi == 0))
    def _():
        run_ref[...] = jnp.zeros_like(run_ref)
        win_sc[...] = win_ref[...]
        wout_sc[...] = wout_ref[...]

    x = x_ref[0]
    mod = mod_ref[0]
    h = _rms(x) * (gmix_ref[...] * (1.0 + mod[1:2])) + mod[0:1]
    hb = h.astype(BF16)

    def project(c0, c1):
        proj_ref[:, c0:c1] = _dot(hb, win_sc[:, c0:c1]) + bin_ref[:, c0:c1]

    project(0, COL_HQ)
    for blk in range(TS // BLOCK):
        _attention_block(proj_ref, kvprev_ref, bias_ref, sink_ref, mixed_ref, blk, j == 0)
        project(COL_HQ + blk * 1024, COL_HQ + (blk + 1) * 1024)
    kvprev_ref[...] = proj_ref[TS - BLOCK:TS, COL_KV:COL_KV + 512]

    lbv = lb_ref[...]
    fl = lbv + (1.0 - lbv) * _sigmoid(proj_ref[:, COL_HF:COL_HF + HG_W])
    g = jnp.log(fl)
    g_hi = g.astype(BF16)
    r1 = g - g_hi.astype(F32)
    g_mid = r1.astype(BF16)
    g_lo = (r1 - g_mid.astype(F32)).astype(BF16)
    ltri = ltri_ref[...]
    b_all = _dot(ltri, g_hi) + _dot(ltri, g_mid) + _dot(ltri, g_lo)

    masks = _hgrn_masks()
    qr = proj_ref[:, COL_HQ:COL_HQ + HG_W]
    qf_all = qr * _sigmoid(qr)
    kk_all = 1.0 - fl
    states = [state_ref[hd] for hd in range(HG_HEADS)]
    n_chunks = TS // CHUNK
    for c0 in range(0, n_chunks, HG_CHUNKS_PER_GROUP):
        _hgrn_chunks(proj_ref, states, b_all, kk_all, qf_all, hgw_ref, mixed_ref,
                     range(c0, c0 + HG_CHUNKS_PER_GROUP), masks)
    for hd in range(HG_HEADS):
        state_ref[hd] = states[hd]

    y = _dot(mixed_ref[...], wout_sc[...]) + bout_ref[...]
    x1 = x + mod[2:3] * y
    x1_ref[0] = x1
    h2 = _rms(x1) * (gffn_ref[...] * (1.0 + mod[4:5])) + mod[3:4]
    for s in range(SUBLANES):
        h2_ref[pl.ds(s, TS, stride=SUBLANES), :] = h2[:, s * LANES:(s + 1) * LANES]

    h_hi = h2.astype(BF16)
    h_lo = (h2 - h_hi.astype(F32)).astype(BF16)
    wr = wr_ref[...]
    t1 = _dot_nt(wr, h_hi)
    t2 = _dot_nt(wr[0:N_EXPERTS], h_lo)
    logits = t1[0:N_EXPERTS] + t1[N_EXPERTS:2 * N_EXPERTS] + t2 + br_ref[:, 0:1]

    eidx = lax.broadcasted_iota(jnp.int32, (N_EXPERTS, TS), 0)
    l = logits
    vals, ohs, ids = [], [], []
    for _ in range(TOP_K):
        mx = jnp.max(l, axis=0, keepdims=True)
        ik = jnp.min(jnp.where(l == mx, eidx, N_EXPERTS), axis=0, keepdims=True)
        oh = eidx == ik
        vals.append(mx)
        ids.append(ik)
        ohs.append(oh)
        l = jnp.where(oh, NEG_INF, l)
    es = [jnp.exp(v - vals[0]) for v in vals]
    den = es[0] + es[1] + es[2] + es[3]
    run = run_ref[...]
    base = run[:, 0:1]
    utri = utri_ref[...]
    ranks = []
    for kq in range(TOP_K):
        ohf = jnp.where(ohs[kq], 1.0, 0.0)
        pref = _dot(ohf.astype(BF16), utri)
        ranks.append(jnp.sum(jnp.where(ohs[kq], base + pref, 0.0), axis=0, keepdims=True))
        base = base + jnp.sum(ohf, axis=1, keepdims=True)
    run_new = jnp.broadcast_to(base, run.shape)
    run_ref[...] = run_new
    cnt_ref[...] = run_new.astype(jnp.int32)
    idx_ref[...] = jnp.concatenate(ids, axis=0)
    wts_ref[...] = jnp.concatenate([e / den for e in es], axis=0)
    rank_ref[...] = jnp.concatenate(ranks, axis=0).astype(jnp.int32)


def _mix_call(sinks, x, mod, gmix, win, bin_, bias, lb, hgw, wout, bout, gffn, wr, br, ltri, utri):
    B, S, D = x.shape
    N = B * S
    nj = S // TS
    const2 = lambda b, j: (0, 0)
    tok = lambda b, j: (0, b * nj + j)
    in_specs = [
        pl.BlockSpec(memory_space=pltpu.SMEM),
        pl.BlockSpec((1, TS, D), lambda b, j: (b, j, 0)),
        pl.BlockSpec((1, 8, D), lambda b, j: (b, 0, 0)),
        pl.BlockSpec((1, D), const2),
        pl.BlockSpec((D, PROJ_COLS), const2),
        pl.BlockSpec((1, PROJ_COLS), const2),
        pl.BlockSpec((ATT_HEADS, BLOCK, 2 * BLOCK), lambda b, j: (0, 0, 0)),
        pl.BlockSpec((1, HG_W), const2),
        pl.BlockSpec((1, HG_D), const2),
        pl.BlockSpec((D, D), const2),
        pl.BlockSpec((1, D), const2),
        pl.BlockSpec((1, D), const2),
        pl.BlockSpec((2 * N_EXPERTS, D), const2),
        pl.BlockSpec((N_EXPERTS, LANES), const2),
        pl.BlockSpec((TS, TS), const2),
        pl.BlockSpec((TS, TS), const2),
    ]
    out_specs = [
        pl.BlockSpec((1, TS, D), lambda b, j: (b, j, 0)),
        pl.BlockSpec((TS * SUBLANES, LANES), lambda b, j: (b * nj + j, 0)),
        pl.BlockSpec((TOP_K, TS), tok),
        pl.BlockSpec((TOP_K, TS), tok),
        pl.BlockSpec((TOP_K, TS), tok),
        pl.BlockSpec((N_EXPERTS, LANES), const2),
    ]
    out_shape = [
        jax.ShapeDtypeStruct((B, S, D), F32),
        jax.ShapeDtypeStruct((N * SUBLANES, LANES), F32),
        jax.ShapeDtypeStruct((TOP_K, N), jnp.int32),
        jax.ShapeDtypeStruct((TOP_K, N), F32),
        jax.ShapeDtypeStruct((TOP_K, N), jnp.int32),
        jax.ShapeDtypeStruct((N_EXPERTS, LANES), jnp.int32),
    ]
    scratch = [
        pltpu.VMEM((TS, PROJ_COLS), F32),
        pltpu.VMEM((TS, D), BF16),
        pltpu.VMEM((BLOCK, 512), F32),
        pltpu.VMEM((HG_HEADS, HG_D, HG_D), F32),
        pltpu.VMEM((N_EXPERTS, LANES), F32),
        pltpu.VMEM((D, PROJ_COLS), BF16),
        pltpu.VMEM((D, D), BF16),
    ]
    return pl.pallas_call(
        _mix_kernel,
        grid=(B, nj),
        in_specs=in_specs,
        out_specs=out_specs,
        out_shape=out_shape,
        scratch_shapes=scratch,
        compiler_params=pltpu.CompilerParams(
            dimension_semantics=("arbitrary", "arbitrary"), vmem_limit_bytes=VMEM_LIMIT),
        name="mix_router",
    )(sinks, x, mod, gmix, win, bin_, bias, lb, hgw, wout, bout, gffn, wr, br, ltri, utri)


TM = 512
FF_CHUNKS = 4
TD = 256
ROW = SUBLANES


def _row_copy(src, src_row, dst, dst_row, sem):
    return pltpu.make_async_copy(src.at[pl.ds(pl.multiple_of(src_row * ROW, ROW), ROW)],
                                 dst.at[pl.ds(pl.multiple_of(dst_row * ROW, ROW), ROW)], sem)


def _col_from_row(w_row, n):
    rr = lax.broadcasted_iota(jnp.int32, (n, n), 0)
    cc = lax.broadcasted_iota(jnp.int32, (n, n), 1)
    return jnp.sum(jnp.where(rr == cc, w_row, 0.0), axis=1, keepdims=True)


def _rows_to_matrix(ref, base, n):
    return jnp.concatenate([ref[pl.ds(base + s, n, stride=ROW), :] for s in range(ROW)], axis=1)


def _dispatch_kernel(tend_ref, pos_ref, h2_hbm, xs_hbm, zbuf, hbuf, sem, lsem, zsem):
    i = pl.program_id(0)
    n = pl.num_programs(0)
    n_tiles = xs_hbm.shape[0] // (TM * ROW)

    def fill_tile(tile):
        return pltpu.make_async_copy(
            zbuf, xs_hbm.at[pl.ds(pl.multiple_of(tile * (TM * ROW), TM * ROW), TM * ROW)], zsem.at[0])

    def padded(e):
        return tend_ref[e] > (tend_ref[e - 1] if e > 0 else 0), tend_ref[e] - 1

    def unused(u):
        tile = tend_ref[N_EXPERTS - 1] + u
        return tile < n_tiles, tile

    @pl.when(i == 0)
    def _():
        zbuf[...] = jnp.zeros_like(zbuf)
        sites = [padded(e) for e in range(N_EXPERTS)] + [unused(u) for u in range(N_EXPERTS)]
        for cond, tile in sites:
            pl.when(cond)(lambda tile=tile: fill_tile(tile).start())
        for cond, tile in sites:
            pl.when(cond)(lambda tile=tile: fill_tile(tile).wait())

    def load(step):
        s = step % 3
        return pltpu.make_async_copy(
            h2_hbm.at[pl.ds(pl.multiple_of(step * (TD * ROW), TD * ROW), TD * ROW)], hbuf.at[s], lsem.at[s])

    def wait_rows(step):
        s = step % 3
        for _ in range(TOP_K):
            pltpu.make_async_copy(hbuf.at[s], xs_hbm.at[pl.ds(0, TD * ROW)], sem.at[s]).wait()

    pl.when(i == 0)(lambda: load(i).start())
    pl.when(i >= 2)(lambda: wait_rows(i - 2))
    pl.when(i + 1 < n)(lambda: load(i + 1).start())
    load(i).wait()
    slot = i % 3
    for q in range(TOP_K * TD):
        _row_copy(hbuf.at[slot], q % TD, xs_hbm, pos_ref[0, 0, q], sem.at[slot]).start(priority=q % 2)

    @pl.when(i == n - 1)
    def _():
        pl.when(i >= 1)(lambda: wait_rows(i - 1))
        wait_rows(i)


def _dispatch_call(tile_end, pos_t, h2rows, n_tiles):
    nblk = pos_t.shape[0]
    grid_spec = pltpu.PrefetchScalarGridSpec(
        num_scalar_prefetch=1,
        grid=(nblk,),
        in_specs=[
            pl.BlockSpec((1, 1, TOP_K * TD), lambda i, tend: (i, 0, 0), memory_space=pltpu.SMEM),
            pl.BlockSpec(memory_space=pl.ANY),
        ],
        out_specs=pl.BlockSpec(memory_space=pl.ANY),
        scratch_shapes=[
            pltpu.VMEM((TM * ROW, LANES), F32),
            pltpu.VMEM((3, TD * ROW, LANES), F32),
            pltpu.SemaphoreType.DMA((3,)),
            pltpu.SemaphoreType.DMA((3,)),
            pltpu.SemaphoreType.DMA((1,)),
        ],
    )
    return pl.pallas_call(
        _dispatch_kernel,
        grid_spec=grid_spec,
        out_shape=jax.ShapeDtypeStruct((n_tiles * TM * ROW, LANES), F32),
        compiler_params=pltpu.CompilerParams(dimension_semantics=("arbitrary",)),
        name="moe_dispatch",
    )(tile_end, pos_t, h2rows)


def _moe_kernel(te_ref, nt_ref, nxt_ref, par_ref, xs_ref, wgu_hbm, bgu_ref, wd_hbm, bd_ref, perm_ref, ys_ref,
                wgu_st, wd_st, wgu_bf, wd_bf, wsem):
    j = pl.program_id(0)
    nt = nt_ref[0]

    def weight_copies(e):
        s = par_ref[e]
        return (pltpu.make_async_copy(wgu_hbm.at[e], wgu_st.at[s], wsem.at[0, s]),
                pltpu.make_async_copy(wd_hbm.at[e], wd_st.at[s], wsem.at[1, s]))

    @pl.when(j == 0)
    def _():
        for cp in weight_copies(te_ref[0]):
            cp.start()

    @pl.when(j < nt)
    def _():
        e = te_ref[j]
        first = (j == 0) | (te_ref[jnp.maximum(j - 1, 0)] != e)

        @pl.when(first)
        def _():
            for cp in weight_copies(e):
                cp.wait()
            s = par_ref[e]
            perm = perm_ref[...]
            for g in range(2 * D_FF // 256):
                w = wgu_st[s, :, g * 256:(g + 1) * 256].astype(BF16)
                wgu_bf[:, g * 256:(g + 1) * 256] = _dot(w, perm).astype(BF16)
            wd_bf[...] = wd_st[s].astype(BF16)
            nxt = nxt_ref[e]

            @pl.when(nxt >= 0)
            def _():
                for cp in weight_copies(nxt):
                    cp.start()

        h = _rows_to_matrix(xs_ref, 0, TM).astype(BF16)
        gw = 2 * D_FF // FF_CHUNKS
        hw = D_FF // FF_CHUNKS

        def up_proj(c):
            return _dot(h, wgu_bf[:, c * gw:(c + 1) * gw]) + bgu_ref[0, :, c * gw:(c + 1) * gw]

        def activate(gu):
            parts = []
            for g in range(gw // 256):
                gate = jnp.minimum(gu[:, g * 256:g * 256 + 128], SWIGLU_LIMIT)
                up = jnp.clip(gu[:, g * 256 + 128:(g + 1) * 256], -SWIGLU_LIMIT, SWIGLU_LIMIT)
                parts.append(((up + 1.0) * gate * _sigmoid(SWIGLU_ALPHA * gate)).astype(BF16))
            return jnp.concatenate(parts, axis=1)

        y = bd_ref[0]
        gu_next = up_proj(0)
        for c in range(FF_CHUNKS):
            gu = gu_next
            if c + 1 < FF_CHUNKS:
                gu_next = up_proj(c + 1)
            y = y + _dot(activate(gu), wd_bf[c * hw:(c + 1) * hw, :])
        for s in range(ROW):
            ys_ref[pl.ds(s, TM, stride=ROW), :] = y[:, s * LANES:(s + 1) * LANES]

    @pl.when(j >= nt)
    def _():
        ys_ref[...] = jnp.zeros_like(ys_ref)


def _moe_call(te, nt, nxt, par, xs, wgu, bgu, wd, bd, perm):
    n_tiles = xs.shape[0] // (TM * ROW)
    tile = lambda j, te, nt, nxt, par: (jnp.minimum(j, nt[0] - 1), 0)
    exp = lambda j, te, nt, nxt, par: (te[jnp.minimum(j, nt[0] - 1)], 0, 0)
    grid_spec = pltpu.PrefetchScalarGridSpec(
        num_scalar_prefetch=4,
        grid=(n_tiles,),
        in_specs=[
            pl.BlockSpec((TM * ROW, LANES), tile),
            pl.BlockSpec(memory_space=pl.ANY),
            pl.BlockSpec((1, 1, 2 * D_FF), exp),
            pl.BlockSpec(memory_space=pl.ANY),
            pl.BlockSpec((1, 1, D_MODEL), exp),
            pl.BlockSpec((256, 256), lambda j, te, nt, nxt, par: (0, 0)),
        ],
        out_specs=pl.BlockSpec((TM * ROW, LANES), lambda j, te, nt, nxt, par: (j, 0)),
        scratch_shapes=[
            pltpu.VMEM((2, D_MODEL, 2 * D_FF), F32),
            pltpu.VMEM((2, D_FF, D_MODEL), F32),
            pltpu.VMEM((D_MODEL, 2 * D_FF), BF16),
            pltpu.VMEM((D_FF, D_MODEL), BF16),
            pltpu.SemaphoreType.DMA((2, 2)),
        ],
    )
    return pl.pallas_call(
        _moe_kernel,
        grid_spec=grid_spec,
        out_shape=jax.ShapeDtypeStruct(xs.shape, F32),
        compiler_params=pltpu.CompilerParams(
            dimension_semantics=("arbitrary",), vmem_limit_bytes=VMEM_LIMIT),
        name="moe_experts",
    )(te, nt, nxt, par, xs, wgu, bgu, wd, bd, perm)


def _final_kernel(pa_ref, pb_ref, pc_ref, wts_ref, x1_ref, mod_ref, gfin_ref, ys_hbm, out_ref,
                  gbuf_a, gbuf_b, sem):
    i = pl.program_id(0)
    n = pl.num_programs(0)

    def issue(pref, buf, s):
        for q in range(TOP_K * TD):
            _row_copy(ys_hbm, pref[0, 0, q], buf, q, sem.at[s]).start(priority=q % 2)

    def wait(buf, s):
        pltpu.make_async_copy(ys_hbm.at[pl.ds(0, TOP_K * TD * ROW)], buf, sem.at[s]).wait()

    def combine(buf, half):
        acc = jnp.zeros((TD, D_MODEL), F32)
        for kq in range(TOP_K):
            v = _rows_to_matrix(buf, kq * TD * ROW, TD)
            acc = acc + _col_from_row(wts_ref[kq:kq + 1, half * TD:(half + 1) * TD], TD) * v
        xf = x1_ref[half * TD:(half + 1) * TD, :] + mod_ref[0][5:6] * acc
        out_ref[half * TD:(half + 1) * TD, :] = _rms(xf) * gfin_ref[...]

    pl.when(i == 0)(lambda: issue(pa_ref, gbuf_a, 0))
    wait(gbuf_a, 0)
    issue(pb_ref, gbuf_b, 1)
    combine(gbuf_a, 0)
    wait(gbuf_b, 1)
    issue(pc_ref, gbuf_a, 0)
    combine(gbuf_b, 1)
    pl.when(i == n - 1)(lambda: wait(gbuf_a, 0))


def _final_call(pos_t, wts, x1, mod, gfin, ys, S):
    N, D = x1.shape
    nstep = N // (2 * TD)
    per_b = S // (2 * TD)
    pos_spec = lambda f: pl.BlockSpec((1, 1, TOP_K * TD), lambda i: (f(i), 0, 0), memory_space=pltpu.SMEM)
    return pl.pallas_call(
        _final_kernel,
        grid=(nstep,),
        in_specs=[
            pos_spec(lambda i: 2 * i),
            pos_spec(lambda i: 2 * i + 1),
            pos_spec(lambda i: jnp.minimum(2 * i + 2, 2 * nstep - 2)),
            pl.BlockSpec((TOP_K, 2 * TD), lambda i: (0, i)),
            pl.BlockSpec((2 * TD, D), lambda i: (i, 0)),
            pl.BlockSpec((1, 8, D), lambda i: (i // per_b, 0, 0)),
            pl.BlockSpec((1, D), lambda i: (0, 0)),
            pl.BlockSpec(memory_space=pl.ANY),
        ],
        out_specs=pl.BlockSpec((2 * TD, D), lambda i: (i, 0)),
        out_shape=jax.ShapeDtypeStruct((N, D), F32),
        scratch_shapes=[
            pltpu.VMEM((TOP_K * TD * ROW, LANES), F32),
            pltpu.VMEM((TOP_K * TD * ROW, LANES), F32),
            pltpu.SemaphoreType.DMA((2,)),
        ],
        compiler_params=pltpu.CompilerParams(
            dimension_semantics=("arbitrary",), vmem_limit_bytes=VMEM_LIMIT),
        name="combine_norm",
    )(pos_t, pos_t, pos_t, wts, x1, mod, gfin, ys)


def _route_plan(idx, rank, cnt, N):
    n_tiles = N * TOP_K // TM + N_EXPERTS
    counts = cnt[:, 0]
    ntile = (counts + TM - 1) // TM
    tile_end = jnp.cumsum(ntile).astype(jnp.int32)
    row_start = (tile_end - ntile) * TM
    nt = tile_end[-1:]
    experts = jnp.arange(N_EXPERTS, dtype=jnp.int32)
    pos = jnp.sum(jnp.where(idx[..., None] == experts, row_start, 0), axis=-1) + rank
    te = jnp.minimum(jnp.sum(jnp.arange(n_tiles)[:, None] >= tile_end[None, :], axis=1),
                     N_EXPERTS - 1).astype(jnp.int32)
    pos_t = jnp.transpose(pos.reshape(TOP_K, N // TD, TD), (1, 0, 2)).reshape(N // TD, 1, TOP_K * TD)
    nonempty = ntile > 0
    par = ((jnp.cumsum(nonempty) - 1) % 2).astype(jnp.int32)
    later = nonempty[None, :] & (experts[None, :] > experts[:, None])
    nxt = jnp.min(jnp.where(later, experts[None, :], N_EXPERTS), axis=1)
    nxt = jnp.where(nxt == N_EXPERTS, -1, nxt).astype(jnp.int32)
    return te, nt, nxt, par, tile_end, pos_t, n_tiles


def _tri_constants():
    r = np.arange(TS)[:, None]
    c = np.arange(TS)[None, :]
    ltri = ((r // CHUNK) == (c // CHUNK)) & (c <= r)
    utri = r < c
    return jnp.asarray(ltri, BF16), jnp.asarray(utri, BF16)


def kernel(x, c, w_ada, b_ada, g_mix, w_in, b_in, attn_sinks, rel_bias, hg_lb, hg_norm_w, w_out, b_out, g_ffn, w_router, b_router, w_gate_up, b_gate_up, w_down, b_down, g_final):
    B, S, D = x.shape
    N = B * S
    mod6 = _prep_call(c, w_ada[0], b_ada)
    mod = jnp.pad(jnp.transpose(mod6, (1, 0, 2)), ((0, 0), (0, 2), (0, 0)))
    lb, bias = _tables_call(rel_bias, hg_lb, _bucket_table())

    wi, bi_ = w_in[0], b_in[0]
    def cols(a):
        aq, ak, av, rest = a[..., :512], a[..., 512:640], a[..., 640:768], a[..., 768:]
        k0, k1, v0, v1 = ak[..., :64], ak[..., 64:], av[..., :64], av[..., 64:]
        return jnp.concatenate([aq, k0, k0, k1, k1, v0, v0, v1, v1, rest], axis=-1)
    win = cols(wi).astype(BF16)
    bin_ = cols(bi_)[None, :]
    wrt = jnp.transpose(w_router[0])
    wr_hi = wrt.astype(BF16)
    wr_lo = (wrt - wr_hi.astype(F32)).astype(BF16)
    wr = jnp.concatenate([wr_hi, wr_lo], axis=0)
    br = jnp.broadcast_to(b_router[0][:, None], (N_EXPERTS, LANES))
    ltri, utri = _tri_constants()

    x1, h2rows, idx, wts, rank, cnt = _mix_call(
        attn_sinks[0], x, mod, g_mix, win, bin_, bias, lb, hg_norm_w, w_out[0].astype(BF16),
        b_out, g_ffn, wr, br, ltri, utri)

    te, nt, nxt, par, tile_end, pos_t, n_tiles = _route_plan(idx, rank, cnt, N)
    ii = np.arange(128)
    perm_np = np.zeros((256, 256), np.float32)
    perm_np[2 * ii, ii] = 1.0
    perm_np[2 * ii + 1, 128 + ii] = 1.0
    perm = jnp.asarray(perm_np, BF16)
    bgu = b_gate_up[0].reshape(N_EXPERTS, D_FF // LANES, LANES, 2)
    bgu = jnp.transpose(bgu, (0, 1, 3, 2)).reshape(N_EXPERTS, 1, 2 * D_FF)
    xs = _dispatch_call(tile_end, pos_t, h2rows, n_tiles)
    ys = _moe_call(te, nt, nxt, par, xs, w_gate_up[0], bgu, w_down[0], b_down[0][:, None, :], perm)
    out = _final_call(pos_t, wts, x1.reshape(N, D), mod, g_final[None, :], ys, S)
    return out.reshape(B, S, D)
```

```python
import functools
import math

import numpy as np
import jax
import jax.numpy as jnp
from jax import lax
from jax.experimental import pallas as pl
from jax.experimental.pallas import tpu as pltpu

D_MODEL = 1024
ATT_HEAD_DIM = 64
ATT_HEADS = 8
ATT_KV_HEADS = 2
ATT_SCALE = ATT_HEAD_DIM ** -0.5
BLOCK = 128
N_BUCKETS = 32
MAX_DISTANCE = 128
HG_HEADS = 4
HG_D = 128
CHUNK = 64
N_EXPERTS = 32
TOP_K = 4
D_FF = 1024
SWIGLU_LIMIT = 7.0
SWIGLU_ALPHA = 1.702
EPS = 1e-5

ATT_Q = ATT_HEADS * ATT_HEAD_DIM
HG_W = HG_HEADS * HG_D
COL_Q = 0
COL_KV = ATT_Q
COL_HQ = COL_KV + 4 * 128
COL_HF = COL_HQ + HG_W
COL_HI = COL_HF + HG_W
COL_HG = COL_HI + HG_W
PROJ_COLS = COL_HG + HG_W

LANES = 128
SUBLANES = 8
TS = 256
VMEM_LIMIT = 56 * 1024 * 1024
HG_SAFE_LEVELS = (32, 16, 8)
HG_DIAG = 8
HG_CHUNKS_PER_GROUP = 2
HG_EXP_CAP = 80.0

F32 = jnp.float32
BF16 = jnp.bfloat16
NEG_INF = float("-inf")


def _dot(a, b):
    return jnp.dot(a, b, preferred_element_type=F32)


def _dot_nt(a, b):
    return lax.dot_general(a, b, (((1,), (1,)), ((), ())), preferred_element_type=F32)


def _dot_tn(a, b):
    return lax.dot_general(a, b, (((0,), (0,)), ((), ())), preferred_element_type=F32)


def _sigmoid(x):
    return 1.0 / (1.0 + jnp.exp(-x))


def _prep_kernel(c_ref, w_ref, b_ref, o_ref):
    c = c_ref[...]
    cond = c * _sigmoid(c)
    o_ref[0] = jnp.dot(cond, w_ref[...], precision=lax.Precision.HIGHEST,
                       preferred_element_type=F32) + b_ref[...]


def _prep_call(c, w_ada, b_ada):
    B = c.shape[0]
    return pl.pallas_call(
        _prep_kernel,
        grid=(6,),
        in_specs=[
            pl.BlockSpec((B, D_MODEL), lambda j: (0, 0)),
            pl.BlockSpec((D_MODEL, D_MODEL), lambda j: (0, j)),
            pl.BlockSpec((1, D_MODEL), lambda j: (0, j)),
        ],
        out_specs=pl.BlockSpec((1, B, D_MODEL), lambda j: (j, 0, 0)),
        out_shape=jax.ShapeDtypeStruct((6, B, D_MODEL), F32),
        compiler_params=pltpu.CompilerParams(dimension_semantics=("arbitrary",)),
        name="adaln_mod",
    )(c, w_ada, b_ada)


def _tables_kernel(rb_ref, lbp_ref, bucket_ref, lb_ref, bias_ref):
    p = lbp_ref[...]
    e = jnp.exp(p - jnp.max(p, axis=0, keepdims=True))
    sm = e / jnp.sum(e, axis=0, keepdims=True)
    lb_ref[...] = sm[0:1]
    bucket = bucket_ref[...]
    for h in range(ATT_HEADS):
        acc = jnp.full(bucket.shape, NEG_INF, F32)
        for bk in range(N_BUCKETS):
            acc = jnp.where(bucket == bk, rb_ref[bk, h], acc)
        bias_ref[h] = acc


def _tables_call(rel_bias, hg_lb, bucket):
    return pl.pallas_call(
        _tables_kernel,
        in_specs=[
            pl.BlockSpec(memory_space=pltpu.SMEM),
            pl.BlockSpec(memory_space=pltpu.VMEM),
            pl.BlockSpec(memory_space=pltpu.VMEM),
        ],
        out_specs=[pl.BlockSpec(memory_space=pltpu.VMEM), pl.BlockSpec(memory_space=pltpu.VMEM)],
        out_shape=[jax.ShapeDtypeStruct((1, HG_W), F32),
                   jax.ShapeDtypeStruct((ATT_HEADS, BLOCK, 2 * BLOCK), F32)],
        name="tables",
    )(rel_bias, hg_lb, bucket)


def _bucket_table():
    i = np.arange(BLOCK, dtype=np.int32)[:, None]
    m = np.arange(2 * BLOCK, dtype=np.int32)[None, :]
    dist = i + BLOCK - m
    n = np.maximum(dist, 0)
    max_exact = N_BUCKETS // 2
    nf = np.maximum(n, 1).astype(np.float32)
    large = max_exact + (np.log(nf / np.float32(max_exact)) / np.float32(math.log(MAX_DISTANCE / max_exact))
                         * np.float32(N_BUCKETS - max_exact)).astype(np.int32)
    large = np.minimum(large, N_BUCKETS - 1)
    bucket = np.where(n < max_exact, n, large)
    return jnp.asarray(np.where((dist >= 0) & (dist < BLOCK), bucket, -1), jnp.int32)


def _rms(x):
    return x * lax.rsqrt(jnp.mean(x * x, axis=-1, keepdims=True) + EPS)


def _attention_block(proj_ref, kvprev_ref, bias_ref, sink_ref, mixed_ref, blk, first_tile):
    r0 = blk * BLOCK
    cur = proj_ref[r0:r0 + BLOCK, COL_KV:COL_KV + 512]
    if blk == 0:
        prev = kvprev_ref[...]
    else:
        prev = proj_ref[r0 - BLOCK:r0, COL_KV:COL_KV + 512]
    band = jnp.concatenate([prev, cur], axis=0)
    lane = lax.broadcasted_iota(jnp.int32, (2 * BLOCK, LANES), 1)
    lo = lane < ATT_HEAD_DIM
    col = lax.broadcasted_iota(jnp.int32, (1, 2 * BLOCK), 1)
    if blk == 0:
        pen = jnp.where((col < BLOCK) & first_tile, NEG_INF, 0.0)
    kbs, vbs = [], []
    for g in range(ATT_KV_HEADS):
        kd = band[:, g * 128:(g + 1) * 128]
        vd = band[:, 256 + g * 128:256 + (g + 1) * 128]
        kbs.append(jnp.concatenate([jnp.where(lo, kd, 0.0), jnp.where(lo, 0.0, kd)], axis=0).astype(BF16))
        vbs.append(jnp.concatenate([jnp.where(lo, vd, 0.0), jnp.where(lo, 0.0, vd)], axis=0).astype(BF16))
    n_pairs = ATT_HEADS // 2
    ss = []
    for g in range(ATT_KV_HEADS):
        qg = jnp.concatenate(
            [(proj_ref[r0:r0 + BLOCK, pair * 128:(pair + 1) * 128] * ATT_SCALE).astype(BF16)
             for pair in (2 * g, 2 * g + 1)], axis=0)
        sg = _dot_nt(qg, kbs[g])
        ss += [sg[0:BLOCK], sg[BLOCK:2 * BLOCK]]
    ps, invs = [], []
    for h in range(ATT_HEADS):
        sh = ss[h // 2][:, (h % 2) * 256:(h % 2 + 1) * 256] + bias_ref[h]
        if blk == 0:
            sh = sh + pen
        sink = sink_ref[h]
        mx = jnp.maximum(jnp.max(sh, axis=-1, keepdims=True), sink)
        p = jnp.exp(sh - mx)
        den = jnp.sum(p, axis=-1, keepdims=True) + jnp.exp(sink - mx)
        ps.append(p.astype(BF16))
        invs.append(1.0 / den)
    os_ = []
    for g in range(ATT_KV_HEADS):
        pg = jnp.concatenate([jnp.concatenate(ps[2 * pair:2 * pair + 2], axis=1)
                              for pair in (2 * g, 2 * g + 1)], axis=0)
        og = _dot(pg, vbs[g])
        os_ += [og[0:BLOCK], og[BLOCK:2 * BLOCK]]
    lane_o = lax.broadcasted_iota(jnp.int32, (BLOCK, LANES), 1)
    for pair in range(n_pairs):
        o = os_[pair] * jnp.where(lane_o < ATT_HEAD_DIM, invs[2 * pair], invs[2 * pair + 1])
        mixed_ref[r0:r0 + BLOCK, pair * 128:(pair + 1) * 128] = o.astype(BF16)


def _hgrn_scaled_operands(b, q, k):
    pairs = []
    for m in HG_SAFE_LEVELS:
        qs, ks = [], []
        for bi in range(CHUNK // m):
            rows = slice(bi * m, (bi + 1) * m)
            if bi % 2 == 1:
                ref = b[bi * m - 1:bi * m]
                qs.append(q[rows] * jnp.exp(b[rows] - ref))
                ks.append(jnp.zeros((m, HG_D), F32))
            else:
                ref = b[(bi + 1) * m - 1:(bi + 1) * m]
                qs.append(jnp.zeros((m, HG_D), F32))
                ks.append(k[rows] * jnp.exp(ref - b[rows]))
        pairs.append((m, jnp.concatenate(qs, axis=0).astype(BF16), jnp.concatenate(ks, axis=0).astype(BF16)))
    qs, ks = [], []
    for bi in range(CHUNK // HG_DIAG):
        rows = slice(bi * HG_DIAG, (bi + 1) * HG_DIAG)
        c = b[rows] if bi == 0 else b[rows] - b[bi * HG_DIAG - 1:bi * HG_DIAG]
        qs.append(q[rows] * jnp.exp(c))
        ks.append(k[rows] * jnp.exp(jnp.minimum(-c, HG_EXP_CAP)))
    pairs.append((0, jnp.concatenate(qs, axis=0).astype(BF16), jnp.concatenate(ks, axis=0).astype(BF16)))
    return pairs


def _hgrn_chunks(proj_ref, states, b_all, kk_all, qf_all, hgw_ref, mixed_ref, chunks, masks):
    units = [(ck, hd) for ck in chunks for hd in range(HG_HEADS)]
    rows = {u: slice(u[0] * CHUNK, (u[0] + 1) * CHUNK) for u in units}
    cols = {u: slice(u[1] * HG_D, (u[1] + 1) * HG_D) for u in units}
    bs = {u: b_all[rows[u], cols[u]] for u in units}
    qs = {u: qf_all[rows[u], cols[u]] for u in units}
    ks = {u: kk_all[rows[u], cols[u]] for u in units}
    vs = {u: proj_ref[rows[u], COL_HI + u[1] * HG_D:COL_HI + (u[1] + 1) * HG_D].astype(BF16) for u in units}
    operands = {u: _hgrn_scaled_operands(bs[u], qs[u], ks[u]) for u in units}
    qbs = {u: (qs[u] * jnp.exp(bs[u])).astype(BF16) for u in units}
    bends = {u: bs[u][CHUNK - 1:CHUNK] for u in units}
    kdecs = {u: (ks[u] * jnp.exp(bends[u] - bs[u])).astype(BF16) for u in units}
    prods = {u: [(m, _dot_nt(ql, kl)) for m, ql, kl in operands[u]] for u in units}
    kvs = {u: _dot_tn(vs[u], kdecs[u]) for u in units}
    intras = {}
    for u in units:
        sc = jnp.zeros((CHUNK, CHUNK), F32)
        for m, pr in prods[u]:
            sc = sc + jnp.where(masks[m], pr, 0.0)
        intras[u] = _dot(sc.astype(BF16), vs[u])
    for ck in chunks:
        inters = [_dot_nt(qbs[(ck, hd)], states[hd].astype(BF16)) for hd in range(HG_HEADS)]
        for hd in range(HG_HEADS):
            u = (ck, hd)
            states[hd] = states[hd] * jnp.exp(bends[u]) + kvs[u]
            gate = proj_ref[rows[u], COL_HG + hd * HG_D:COL_HG + (hd + 1) * HG_D]
            on = _rms(intras[u] + inters[hd]) * hgw_ref[...] * (gate * _sigmoid(gate))
            mixed_ref[rows[u], ATT_Q + hd * HG_D:ATT_Q + (hd + 1) * HG_D] = on.astype(BF16)


def _hgrn_masks():
    r = lax.broadcasted_iota(jnp.int32, (CHUNK, CHUNK), 0)
    c = lax.broadcasted_iota(jnp.int32, (CHUNK, CHUNK), 1)
    masks = {}
    for m in HG_SAFE_LEVELS:
        masks[m] = ((r // (2 * m)) == (c // (2 * m))) & (((r // m) % 2) == 1) & (((c // m) % 2) == 0)
    masks[0] = ((r // HG_DIAG) == (c // HG_DIAG)) & (c <= r)
    return masks


def _mix_kernel(sink_ref, x_ref, mod_ref, gmix_ref, win_ref, bin_ref, bias_ref, lb_ref, hgw_ref,
                wout_ref, bout_ref, gffn_ref, wr_ref, br_ref, ltri_ref, utri_ref, etri_ref,
                x1_ref, pos_ref, wts_ref, cnt_ref, cur_ref, own_ref, nt_ref, xs_hbm,
                proj_ref, mixed_ref, kvprev_ref, state_ref, run_ref, win_sc, wout_sc,
                curs_ref, owns_ref, nfs_ref, h2buf, posv, pos_smem, rsem, psem):
    bi = pl.program_id(0)
    j = pl.program_id(1)
    step = bi * pl.num_programs(1) + j
    n_steps = pl.num_programs(0) * pl.num_programs(1)
    slot = step % 2
    n_tiles = xs_hbm.shape[0] // (TM * ROW) - DUMP_TILES

    def rows_wait(sl):
        for _ in range(TOP_K):
            pltpu.make_async_copy(h2buf.at[sl], xs_hbm.at[pl.ds(0, TS * ROW)], rsem.at[sl]).wait()

    def rows_issue(sl):
        for q in range(TOP_K * TS):
            _row_copy(h2buf.at[sl], q % TS, xs_hbm, pos_smem[sl, q // TS, q % TS],
                      rsem.at[sl]).start(priority=q % 2)

    def pos_copy(sl):
        return pltpu.make_async_copy(posv.at[sl], pos_smem.at[sl], psem.at[sl])

    @pl.when(j == 0)
    def _():
        kvprev_ref[...] = jnp.zeros_like(kvprev_ref)
        state_ref[...] = jnp.zeros_like(state_ref)

    @pl.when(step == 0)
    def _():
        run_ref[...] = jnp.zeros_like(run_ref)
        curs_ref[...] = jnp.zeros_like(curs_ref)
        owns_ref[...] = jnp.full(owns_ref.shape, -1.0, F32)
        nfs_ref[...] = jnp.zeros_like(nfs_ref)
        win_sc[...] = win_ref[...]
        wout_sc[...] = wout_ref[...]
        h2buf[1] = jnp.zeros(h2buf.shape[1:], F32)
        spare = n_tiles * TM + lax.broadcasted_iota(jnp.int32, (TOP_K, TS), 0) * TS \
            + lax.broadcasted_iota(jnp.int32, (TOP_K, TS), 1)
        posv[1] = spare
        pos_copy(1).start()

    pl.when(step >= 1)(lambda: rows_wait(slot))

    x = x_ref[0]
    mod = mod_ref[0]
    h = _rms(x) * (gmix_ref[...] * (1.0 + mod[1:2])) + mod[0:1]
    hb = h.astype(BF16)

    def project(c0, c1):
        proj_ref[:, c0:c1] = _dot(hb, win_sc[:, c0:c1]) + bin_ref[:, c0:c1]

    project(0, COL_HQ)
    pos_copy(1 - slot).wait()
    rows_issue(1 - slot)
    for blk in range(TS // BLOCK):
        _attention_block(proj_ref, kvprev_ref, bias_ref, sink_ref, mixed_ref, blk, j == 0)
        project(COL_HQ + blk * 1024, COL_HQ + (blk + 1) * 1024)
    kvprev_ref[...] = proj_ref[TS - BLOCK:TS, COL_KV:COL_KV + 512]

    lbv = lb_ref[...]
    fl = lbv + (1.0 - lbv) * _sigmoid(proj_ref[:, COL_HF:COL_HF + HG_W])
    g = jnp.log(fl)
    g_hi = g.astype(BF16)
    r1 = g - g_hi.astype(F32)
    g_mid = r1.astype(BF16)
    g_lo = (r1 - g_mid.astype(F32)).astype(BF16)
    ltri = ltri_ref[...]
    b_all = _dot(ltri, g_hi) + _dot(ltri, g_mid) + _dot(ltri, g_lo)

    masks = _hgrn_masks()
    qr = proj_ref[:, COL_HQ:COL_HQ + HG_W]
    qf_all = qr * _sigmoid(qr)
    kk_all = 1.0 - fl
    states = [state_ref[hd] for hd in range(HG_HEADS)]
    n_chunks = TS // CHUNK
    for c0 in range(0, n_chunks, HG_CHUNKS_PER_GROUP):
        _hgrn_chunks(proj_ref, states, b_all, kk_all, qf_all, hgw_ref, mixed_ref,
                     range(c0, c0 + HG_CHUNKS_PER_GROUP), masks)
    for hd in range(HG_HEADS):
        state_ref[hd] = states[hd]

    y = _dot(mixed_ref[...], wout_sc[...]) + bout_ref[...]
    x1 = x + mod[2:3] * y
    x1_ref[0] = x1
    h2 = _rms(x1) * (gffn_ref[...] * (1.0 + mod[4:5])) + mod[3:4]
    hb2 = h2buf.at[slot]
    for s in range(SUBLANES):
        hb2[pl.ds(s, TS, stride=SUBLANES), :] = h2[:, s * LANES:(s + 1) * LANES]

    h_hi = h2.astype(BF16)
    h_lo = (h2 - h_hi.astype(F32)).astype(BF16)
    wr = wr_ref[...]
    t1 = _dot_nt(wr, h_hi)
    t2 = _dot_nt(wr[0:N_EXPERTS], h_lo)
    logits = t1[0:N_EXPERTS] + t1[N_EXPERTS:2 * N_EXPERTS] + t2 + br_ref[:, 0:1]

    eidx = lax.broadcasted_iota(jnp.int32, (N_EXPERTS, TS), 0)
    l = logits
    vals, ohs, ids = [], [], []
    for _ in range(TOP_K):
        mx = jnp.max(l, axis=0, keepdims=True)
        ik = jnp.min(jnp.where(l == mx, eidx, N_EXPERTS), axis=0, keepdims=True)
        oh = eidx == ik
        vals.append(mx)
        ids.append(ik)
        ohs.append(oh)
        l = jnp.where(oh, NEG_INF, l)
    es = [jnp.exp(v - vals[0]) for v in vals]
    den = es[0] + es[1] + es[2] + es[3]
    run = run_ref[...]
    c0 = run[:, 0:1]
    base = c0
    utri = utri_ref[...]
    ranks = []
    for kq in range(TOP_K):
        ohf = jnp.where(ohs[kq], 1.0, 0.0)
        pref = _dot(ohf.astype(BF16), utri)
        ranks.append(base + pref)
        base = base + jnp.sum(ohf, axis=1, keepdims=True)
    run_new = jnp.broadcast_to(base, run.shape)
    run_ref[...] = run_new
    cnt_ref[...] = run_new.astype(jnp.int32)

    inv_tm = 1.0 / TM
    prev_ord = jnp.floor((c0 - 1.0) * inv_tm)
    new_e = jnp.floor((base - 1.0) * inv_tm) - prev_ord
    excl = _dot(etri_ref[...], jnp.broadcast_to(new_e, run.shape).astype(BF16))[:, 0:1]
    nf = nfs_ref[0:1, 0:1]
    newbase = nf + excl
    cur = curs_ref[:, 0:1]
    poss = []
    for kq in range(TOP_K):
        r = ranks[kq]
        ordv = jnp.floor(r * inv_tm)
        phys = jnp.where(ordv == prev_ord, cur, newbase + (ordv - prev_ord - 1.0))
        slot_of = phys * TM + (r - ordv * TM)
        poss.append(jnp.sum(jnp.where(ohs[kq], slot_of, 0.0), axis=0, keepdims=True))
    cur_new = jnp.where(new_e > 0.0, newbase + new_e - 1.0, cur)
    curs_ref[...] = jnp.broadcast_to(cur_new, curs_ref.shape)
    nf_new = nf + jnp.sum(new_e, axis=0, keepdims=True)
    nfs_ref[...] = jnp.broadcast_to(nf_new, nfs_ref.shape)
    tile_lane = lax.broadcasted_iota(jnp.int32, (N_EXPERTS, OWN_LANES), 1).astype(F32)
    started = (tile_lane >= newbase) & (tile_lane < newbase + new_e)
    cand = jnp.max(jnp.where(started, eidx[:, 0:1].astype(F32), -1.0), axis=0, keepdims=True)
    owner = jnp.where(cand >= 0.0, cand, owns_ref[...])
    owns_ref[...] = owner
    cur_ref[...] = curs_ref[...].astype(jnp.int32)
    own_ref[...] = owner.astype(jnp.int32)
    nt_ref[...] = nfs_ref[...].astype(jnp.int32)

    pos = jnp.concatenate(poss, axis=0).astype(jnp.int32)
    pos_ref[...] = pos
    posv[slot] = pos
    pos_copy(slot).start()
    wts_ref[...] = jnp.concatenate([e / den for e in es], axis=0)

    @pl.when(step == n_steps - 1)
    def _():
        pos_copy(slot).wait()
        rows_issue(slot)
        rows_wait(1 - slot)
        rows_wait(slot)


def _mix_call(sinks, x, mod, gmix, win, bin_, bias, lb, hgw, wout, bout, gffn, wr, br, ltri, utri, etri):
    B, S, D = x.shape
    N = B * S
    nj = S // TS
    n_tiles = N * TOP_K // TM + N_EXPERTS
    assert n_tiles <= OWN_LANES and TOP_K * TS <= DUMP_TILES * TM
    const2 = lambda b, j: (0, 0)
    tok = lambda b, j: (0, b * nj + j)
    in_specs = [
        pl.BlockSpec(memory_space=pltpu.SMEM),
        pl.BlockSpec((1, TS, D), lambda b, j: (b, j, 0)),
        pl.BlockSpec((1, 8, D), lambda b, j: (b, 0, 0)),
        pl.BlockSpec((1, D), const2),
        pl.BlockSpec((D, PROJ_COLS), const2),
        pl.BlockSpec((1, PROJ_COLS), const2),
        pl.BlockSpec((ATT_HEADS, BLOCK, 2 * BLOCK), lambda b, j: (0, 0, 0)),
        pl.BlockSpec((1, HG_W), const2),
        pl.BlockSpec((1, HG_D), const2),
        pl.BlockSpec((D, D), const2),
        pl.BlockSpec((1, D), const2),
        pl.BlockSpec((1, D), const2),
        pl.BlockSpec((2 * N_EXPERTS, D), const2),
        pl.BlockSpec((N_EXPERTS, LANES), const2),
        pl.BlockSpec((TS, TS), const2),
        pl.BlockSpec((TS, TS), const2),
        pl.BlockSpec((N_EXPERTS, N_EXPERTS), const2),
    ]
    out_specs = [
        pl.BlockSpec((1, TS, D), lambda b, j: (b, j, 0)),
        pl.BlockSpec((TOP_K, TS), tok),
        pl.BlockSpec((TOP_K, TS), tok),
        pl.BlockSpec((N_EXPERTS, LANES), const2),
        pl.BlockSpec((N_EXPERTS, LANES), const2),
        pl.BlockSpec((1, OWN_LANES), const2),
        pl.BlockSpec((1, LANES), const2),
        pl.BlockSpec(memory_space=pl.ANY),
    ]
    out_shape = [
        jax.ShapeDtypeStruct((B, S, D), F32),
        jax.ShapeDtypeStruct((TOP_K, N), jnp.int32),
        jax.ShapeDtypeStruct((TOP_K, N), F32),
        jax.ShapeDtypeStruct((N_EXPERTS, LANES), jnp.int32),
        jax.ShapeDtypeStruct((N_EXPERTS, LANES), jnp.int32),
        jax.ShapeDtypeStruct((1, OWN_LANES), jnp.int32),
        jax.ShapeDtypeStruct((1, LANES), jnp.int32),
        jax.ShapeDtypeStruct(((n_tiles + DUMP_TILES) * TM * ROW, LANES), F32),
    ]
    scratch = [
        pltpu.VMEM((TS, PROJ_COLS), F32),
        pltpu.VMEM((TS, D), BF16),
        pltpu.VMEM((BLOCK, 512), F32),
        pltpu.VMEM((HG_HEADS, HG_D, HG_D), F32),
        pltpu.VMEM((N_EXPERTS, LANES), F32),
        pltpu.VMEM((D, PROJ_COLS), BF16),
        pltpu.VMEM((D, D), BF16),
        pltpu.VMEM((N_EXPERTS, LANES), F32),
        pltpu.VMEM((1, OWN_LANES), F32),
        pltpu.VMEM((1, LANES), F32),
        pltpu.VMEM((2, TS * ROW, LANES), F32),
        pltpu.VMEM((2, TOP_K, TS), jnp.int32),
        pltpu.SMEM((2, TOP_K, TS), jnp.int32),
        pltpu.SemaphoreType.DMA((2,)),
        pltpu.SemaphoreType.DMA((2,)),
    ]
    return pl.pallas_call(
        _mix_kernel,
        grid=(B, nj),
        in_specs=in_specs,
        out_specs=out_specs,
        out_shape=out_shape,
        scratch_shapes=scratch,
        compiler_params=pltpu.CompilerParams(
            dimension_semantics=("arbitrary", "arbitrary"), vmem_limit_bytes=VMEM_LIMIT),
        name="mix_router",
    )(sinks, x, mod, gmix, win, bin_, bias, lb, hgw, wout, bout, gffn, wr, br, ltri, utri, etri)


TM = 512
DUMP_TILES = 2
OWN_LANES = 384
FF_CHUNKS = 4
TD = 256
ROW = SUBLANES


def _row_copy(src, src_row, dst, dst_row, sem):
    return pltpu.make_async_copy(src.at[pl.ds(pl.multiple_of(src_row * ROW, ROW), ROW)],
                                 dst.at[pl.ds(pl.multiple_of(dst_row * ROW, ROW), ROW)], sem)


def _col_from_row(w_row, n):
    rr = lax.broadcasted_iota(jnp.int32, (n, n), 0)
    cc = lax.broadcasted_iota(jnp.int32, (n, n), 1)
    return jnp.sum(jnp.where(rr == cc, w_row, 0.0), axis=1, keepdims=True)


def _rows_to_matrix(ref, base, n):
    return jnp.concatenate([ref[pl.ds(base + s, n, stride=ROW), :] for s in range(ROW)], axis=1)


def _padfill_kernel(last_ref, fill_ref, nt_ref, xs_in, xs_out, zbuf, sem):
    del xs_in
    n_tiles = xs_out.shape[0] // (TM * ROW) - DUMP_TILES
    zbuf[...] = jnp.zeros_like(zbuf)
    sites = []
    for e in range(N_EXPERTS):
        fill = fill_ref[e]
        row = last_ref[e] * TM + fill
        rem = TM - fill
        n = TM // 2
        while n >= 1:
            cond = (rem & n) != 0
            sites.append((cond, row, n))
            row = row + jnp.where(cond, n, 0)
            n //= 2
    for u in range(N_EXPERTS):
        tile = nt_ref[0] + u
        sites.append((tile < n_tiles, tile * TM, TM))

    def cp(row, n):
        return pltpu.make_async_copy(
            zbuf.at[pl.ds(0, n * ROW)], xs_out.at[pl.ds(pl.multiple_of(row * ROW, ROW), n * ROW)], sem.at[0])

    for cond, row, n in sites:
        pl.when(cond)(lambda row=row, n=n: cp(row, n).start())
    for cond, row, n in sites:
        pl.when(cond)(lambda row=row, n=n: cp(row, n).wait())


def _padfill_call(last, fill, nt, xs):
    grid_spec = pltpu.PrefetchScalarGridSpec(
        num_scalar_prefetch=3,
        grid=(1,),
        in_specs=[pl.BlockSpec(memory_space=pl.ANY)],
        out_specs=pl.BlockSpec(memory_space=pl.ANY),
        scratch_shapes=[pltpu.VMEM((TM * ROW, LANES), F32), pltpu.SemaphoreType.DMA((1,))],
    )
    return pl.pallas_call(
        _padfill_kernel,
        grid_spec=grid_spec,
        out_shape=jax.ShapeDtypeStruct(xs.shape, F32),
        input_output_aliases={3: 0},
        compiler_params=pltpu.CompilerParams(dimension_semantics=("arbitrary",)),
        name="moe_padfill",
    )(last, fill, nt, xs)


def _moe_kernel(order_ref, te_ref, nt_ref, nxt_ref, par_ref, xs_ref, wgu_hbm, bgu_ref, wd_hbm, bd_ref, perm_ref,
                ys_ref, wgu_st, wd_st, wgu_bf, wd_bf, wsem):
    del order_ref
    j = pl.program_id(0)
    nt = nt_ref[0]

    def weight_copies(e):
        s = par_ref[e]
        return (pltpu.make_async_copy(wgu_hbm.at[e], wgu_st.at[s], wsem.at[0, s]),
                pltpu.make_async_copy(wd_hbm.at[e], wd_st.at[s], wsem.at[1, s]))

    @pl.when(j == 0)
    def _():
        for cp in weight_copies(te_ref[0]):
            cp.start()

    @pl.when(j < nt)
    def _():
        e = te_ref[j]
        first = (j == 0) | (te_ref[jnp.maximum(j - 1, 0)] != e)

        @pl.when(first)
        def _():
            for cp in weight_copies(e):
                cp.wait()
            s = par_ref[e]
            perm = perm_ref[...]
            for g in range(2 * D_FF // 256):
                w = wgu_st[s, :, g * 256:(g + 1) * 256].astype(BF16)
                wgu_bf[:, g * 256:(g + 1) * 256] = _dot(w, perm).astype(BF16)
            wd_bf[...] = wd_st[s].astype(BF16)
            nxt = nxt_ref[e]

            @pl.when(nxt >= 0)
            def _():
                for cp in weight_copies(nxt):
                    cp.start()

        h = _rows_to_matrix(xs_ref, 0, TM).astype(BF16)
        gw = 2 * D_FF // FF_CHUNKS
        hw = D_FF // FF_CHUNKS

        def up_proj(c):
            return _dot(h, wgu_bf[:, c * gw:(c + 1) * gw]) + bgu_ref[0, :, c * gw:(c + 1) * gw]

        def activate(gu):
            parts = []
            for g in range(gw // 256):
                gate = jnp.minimum(gu[:, g * 256:g * 256 + 128], SWIGLU_LIMIT)
                up = jnp.clip(gu[:, g * 256 + 128:(g + 1) * 256], -SWIGLU_LIMIT, SWIGLU_LIMIT)
                parts.append(((up + 1.0) * gate * _sigmoid(SWIGLU_ALPHA * gate)).astype(BF16))
            return jnp.concatenate(parts, axis=1)

        y = bd_ref[0]
        gu_next = up_proj(0)
        for c in range(FF_CHUNKS):
            gu = gu_next
            if c + 1 < FF_CHUNKS:
                gu_next = up_proj(c + 1)
            y = y + _dot(activate(gu), wd_bf[c * hw:(c + 1) * hw, :])
        for s in range(ROW):
            ys_ref[pl.ds(s, TM, stride=ROW), :] = y[:, s * LANES:(s + 1) * LANES]

    @pl.when(j >= nt)
    def _():
        ys_ref[...] = jnp.zeros_like(ys_ref)


def _moe_call(order, te, nt, nxt, par, xs, wgu, bgu, wd, bd, perm):
    n_tiles = xs.shape[0] // (TM * ROW) - DUMP_TILES
    tile = lambda j, order, te, nt, nxt, par: (order[jnp.minimum(j, nt[0] - 1)], 0)
    exp = lambda j, order, te, nt, nxt, par: (te[jnp.minimum(j, nt[0] - 1)], 0, 0)
    grid_spec = pltpu.PrefetchScalarGridSpec(
        num_scalar_prefetch=5,
        grid=(n_tiles,),
        in_specs=[
            pl.BlockSpec((TM * ROW, LANES), tile),
            pl.BlockSpec(memory_space=pl.ANY),
            pl.BlockSpec((1, 1, 2 * D_FF), exp),
            pl.BlockSpec(memory_space=pl.ANY),
            pl.BlockSpec((1, 1, D_MODEL), exp),
            pl.BlockSpec((256, 256), lambda j, order, te, nt, nxt, par: (0, 0)),
        ],
        out_specs=pl.BlockSpec((TM * ROW, LANES), lambda j, order, te, nt, nxt, par: (order[j], 0)),
        scratch_shapes=[
            pltpu.VMEM((2, D_MODEL, 2 * D_FF), F32),
            pltpu.VMEM((2, D_FF, D_MODEL), F32),
            pltpu.VMEM((D_MODEL, 2 * D_FF), BF16),
            pltpu.VMEM((D_FF, D_MODEL), BF16),
            pltpu.SemaphoreType.DMA((2, 2)),
        ],
    )
    return pl.pallas_call(
        _moe_kernel,
        grid_spec=grid_spec,
        out_shape=jax.ShapeDtypeStruct((n_tiles * TM * ROW, LANES), F32),
        compiler_params=pltpu.CompilerParams(
            dimension_semantics=("arbitrary",), vmem_limit_bytes=VMEM_LIMIT),
        name="moe_experts",
    )(order, te, nt, nxt, par, xs, wgu, bgu, wd, bd, perm)


def _final_kernel(pa_ref, pb_ref, pc_ref, wts_ref, x1_ref, mod_ref, gfin_ref, ys_hbm, out_ref,
                  gbuf_a, gbuf_b, sem):
    i = pl.program_id(0)
    n = pl.num_programs(0)

    def issue(pref, buf, s):
        for q in range(TOP_K * TD):
            _row_copy(ys_hbm, pref[0, 0, q], buf, q, sem.at[s]).start(priority=q % 2)

    def wait(buf, s):
        pltpu.make_async_copy(ys_hbm.at[pl.ds(0, TOP_K * TD * ROW)], buf, sem.at[s]).wait()

    def combine(buf, half):
        acc = jnp.zeros((TD, D_MODEL), F32)
        for kq in range(TOP_K):
            v = _rows_to_matrix(buf, kq * TD * ROW, TD)
            acc = acc + _col_from_row(wts_ref[kq:kq + 1, half * TD:(half + 1) * TD], TD) * v
        xf = x1_ref[half * TD:(half + 1) * TD, :] + mod_ref[0][5:6] * acc
        out_ref[half * TD:(half + 1) * TD, :] = _rms(xf) * gfin_ref[...]

    pl.when(i == 0)(lambda: issue(pa_ref, gbuf_a, 0))
    wait(gbuf_a, 0)
    issue(pb_ref, gbuf_b, 1)
    combine(gbuf_a, 0)
    wait(gbuf_b, 1)
    issue(pc_ref, gbuf_a, 0)
    combine(gbuf_b, 1)
    pl.when(i == n - 1)(lambda: wait(gbuf_a, 0))


def _final_call(pos_t, wts, x1, mod, gfin, ys, S):
    N, D = x1.shape
    nstep = N // (2 * TD)
    per_b = S // (2 * TD)
    pos_spec = lambda f: pl.BlockSpec((1, 1, TOP_K * TD), lambda i: (f(i), 0, 0), memory_space=pltpu.SMEM)
    return pl.pallas_call(
        _final_kernel,
        grid=(nstep,),
        in_specs=[
            pos_spec(lambda i: 2 * i),
            pos_spec(lambda i: 2 * i + 1),
            pos_spec(lambda i: jnp.minimum(2 * i + 2, 2 * nstep - 2)),
            pl.BlockSpec((TOP_K, 2 * TD), lambda i: (0, i)),
            pl.BlockSpec((2 * TD, D), lambda i: (i, 0)),
            pl.BlockSpec((1, 8, D), lambda i: (i // per_b, 0, 0)),
            pl.BlockSpec((1, D), lambda i: (0, 0)),
            pl.BlockSpec(memory_space=pl.ANY),
        ],
        out_specs=pl.BlockSpec((2 * TD, D), lambda i: (i, 0)),
        out_shape=jax.ShapeDtypeStruct((N, D), F32),
        scratch_shapes=[
            pltpu.VMEM((TOP_K * TD * ROW, LANES), F32),
            pltpu.VMEM((TOP_K * TD * ROW, LANES), F32),
            pltpu.SemaphoreType.DMA((2,)),
        ],
        compiler_params=pltpu.CompilerParams(
            dimension_semantics=("arbitrary",), vmem_limit_bytes=VMEM_LIMIT),
        name="combine_norm",
    )(pos_t, pos_t, pos_t, wts, x1, mod, gfin, ys)


def _route_plan(pos, cnt, cur, own, ntiles, N):
    n_tiles = N * TOP_K // TM + N_EXPERTS
    counts = cnt[:, 0]
    nt = ntiles[0, 0:1]
    owner = own[0, :n_tiles]
    tiles = jnp.arange(n_tiles, dtype=jnp.int32)
    experts = jnp.arange(N_EXPERTS, dtype=jnp.int32)
    used = tiles < nt[0]
    key = jnp.where(used, owner, N_EXPERTS) * OWN_LANES + tiles
    place = jnp.sum(key[None, :] < key[:, None], axis=1)
    at = place[None, :] == tiles[:, None]
    order = jnp.sum(jnp.where(at, tiles[None, :], 0), axis=1).astype(jnp.int32)
    te = jnp.minimum(jnp.sum(jnp.where(at, jnp.where(used, owner, N_EXPERTS)[None, :], 0), axis=1),
                     N_EXPERTS - 1).astype(jnp.int32)
    ntile = jnp.sum((owner[None, :] == experts[:, None]) & used[None, :], axis=1)
    nonempty = ntile > 0
    last = jnp.where(nonempty, cur[:, 0], 0).astype(jnp.int32)
    fill = jnp.where(nonempty, counts - (ntile - 1) * TM, TM).astype(jnp.int32)
    pos_t = jnp.transpose(pos.reshape(TOP_K, N // TD, TD), (1, 0, 2)).reshape(N // TD, 1, TOP_K * TD)
    par = ((jnp.cumsum(nonempty) - 1) % 2).astype(jnp.int32)
    later = nonempty[None, :] & (experts[None, :] > experts[:, None])
    nxt = jnp.min(jnp.where(later, experts[None, :], N_EXPERTS), axis=1)
    nxt = jnp.where(nxt == N_EXPERTS, -1, nxt).astype(jnp.int32)
    return order, te, nt, nxt, par, last, fill, pos_t


def _tri_constants():
    r = np.arange(TS)[:, None]
    c = np.arange(TS)[None, :]
    ltri = ((r // CHUNK) == (c // CHUNK)) & (c <= r)
    utri = r < c
    return jnp.asarray(ltri, BF16), jnp.asarray(utri, BF16)


def kernel(x, c, w_ada, b_ada, g_mix, w_in, b_in, attn_sinks, rel_bias, hg_lb, hg_norm_w, w_out, b_out, g_ffn, w_router, b_router, w_gate_up, b_gate_up, w_down, b_down, g_final):
    B, S, D = x.shape
    N = B * S
    mod6 = _prep_call(c, w_ada[0], b_ada)
    mod = jnp.pad(jnp.transpose(mod6, (1, 0, 2)), ((0, 0), (0, 2), (0, 0)))
    lb, bias = _tables_call(rel_bias, hg_lb, _bucket_table())

    wi, bi_ = w_in[0], b_in[0]
    def cols(a):
        aq, ak, av, rest = a[..., :512], a[..., 512:640], a[..., 640:768], a[..., 768:]
        k0, k1, v0, v1 = ak[..., :64], ak[..., 64:], av[..., :64], av[..., 64:]
        return jnp.concatenate([aq, k0, k0, k1, k1, v0, v0, v1, v1, rest], axis=-1)
    win = cols(wi).astype(BF16)
    bin_ = cols(bi_)[None, :]
    wrt = jnp.transpose(w_router[0])
    wr_hi = wrt.astype(BF16)
    wr_lo = (wrt - wr_hi.astype(F32)).astype(BF16)
    wr = jnp.concatenate([wr_hi, wr_lo], axis=0)
    br = jnp.broadcast_to(b_router[0][:, None], (N_EXPERTS, LANES))
    ltri, utri = _tri_constants()
    etri = jnp.asarray(np.tril(np.ones((N_EXPERTS, N_EXPERTS), np.float32), -1), BF16)

    x1, pos, wts, cnt, cur, own, ntiles, xs = _mix_call(
        attn_sinks[0], x, mod, g_mix, win, bin_, bias, lb, hg_norm_w, w_out[0].astype(BF16),
        b_out, g_ffn, wr, br, ltri, utri, etri)

    order, te, nt, nxt, par, last, fill, pos_t = _route_plan(pos, cnt, cur, own, ntiles, N)
    ii = np.arange(128)
    perm_np = np.zeros((256, 256), np.float32)
    perm_np[2 * ii, ii] = 1.0
    perm_np[2 * ii + 1, 128 + ii] = 1.0
    perm = jnp.asarray(perm_np, BF16)
    bgu = b_gate_up[0].reshape(N_EXPERTS, D_FF // LANES, LANES, 2)
    bgu = jnp.transpose(bgu, (0, 1, 3, 2)).reshape(N_EXPERTS, 1, 2 * D_FF)
    xs = _padfill_call(last, fill, nt, xs)
    ys = _moe_call(order, te, nt, nxt, par, xs, w_gate_up[0], bgu, w_down[0], b_down[0][:, None, :], perm)
    out = _final_call(pos_t, wts, x1.reshape(N, D), mod, g_final[None, :], ys, S)
    return out.reshape(B, S, D)
```

```python
import functools
import math

import numpy as np
import jax
import jax.numpy as jnp
from jax import lax
from jax.experimental import pallas as pl
from jax.experimental.pallas import tpu as pltpu

D_MODEL = 1024
ATT_HEAD_DIM = 64
ATT_HEADS = 8
ATT_KV_HEADS = 2
ATT_SCALE = ATT_HEAD_DIM ** -0.5
BLOCK = 128
N_BUCKETS = 32
MAX_DISTANCE = 128
HG_HEADS = 4
HG_D = 128
CHUNK = 64
N_EXPERTS = 32
TOP_K = 4
D_FF = 1024
SWIGLU_LIMIT = 7.0
SWIGLU_ALPHA = 1.702
EPS = 1e-5

ATT_Q = ATT_HEADS * ATT_HEAD_DIM
HG_W = HG_HEADS * HG_D
COL_Q = 0
COL_KV = ATT_Q
COL_HQ = COL_KV + 4 * 128
COL_HF = COL_HQ + HG_W
COL_HI = COL_HF + HG_W
COL_HG = COL_HI + HG_W
PROJ_COLS = COL_HG + HG_W

LANES = 128
SUBLANES = 8
TS = 512
VMEM_LIMIT = 56 * 1024 * 1024
HG_SAFE_LEVELS = (32, 16, 8)
HG_DIAG = 8
HG_CHUNKS_PER_GROUP = 2
HG_EXP_CAP = 80.0

F32 = jnp.float32
BF16 = jnp.bfloat16
NEG_INF = float("-inf")


def _dot(a, b):
    return jnp.dot(a, b, preferred_element_type=F32)


def _dot_nt(a, b):
    return lax.dot_general(a, b, (((1,), (1,)), ((), ())), preferred_element_type=F32)


def _dot_tn(a, b):
    return lax.dot_general(a, b, (((0,), (0,)), ((), ())), preferred_element_type=F32)


def _sigmoid(x):
    return 1.0 / (1.0 + jnp.exp(-x))


def _prep_kernel(c_ref, w_ref, b_ref, o_ref):
    c = c_ref[...]
    cond = c * _sigmoid(c)
    o_ref[0] = jnp.dot(cond, w_ref[...], precision=lax.Precision.HIGHEST,
                       preferred_element_type=F32) + b_ref[...]


def _prep_call(c, w_ada, b_ada):
    B = c.shape[0]
    return pl.pallas_call(
        _prep_kernel,
        grid=(6,),
        in_specs=[
            pl.BlockSpec((B, D_MODEL), lambda j: (0, 0)),
            pl.BlockSpec((D_MODEL, D_MODEL), lambda j: (0, j)),
            pl.BlockSpec((1, D_MODEL), lambda j: (0, j)),
        ],
        out_specs=pl.BlockSpec((1, B, D_MODEL), lambda j: (j, 0, 0)),
        out_shape=jax.ShapeDtypeStruct((6, B, D_MODEL), F32),
        compiler_params=pltpu.CompilerParams(dimension_semantics=("arbitrary",)),
        name="adaln_mod",
    )(c, w_ada, b_ada)


def _tables_kernel(rb_ref, lbp_ref, bucket_ref, lb_ref, bias_ref):
    p = lbp_ref[...]
    e = jnp.exp(p - jnp.max(p, axis=0, keepdims=True))
    sm = e / jnp.sum(e, axis=0, keepdims=True)
    lb_ref[...] = sm[0:1]
    bucket = bucket_ref[...]
    for h in range(ATT_HEADS):
        acc = jnp.full(bucket.shape, NEG_INF, F32)
        for bk in range(N_BUCKETS):
            acc = jnp.where(bucket == bk, rb_ref[bk, h], acc)
        bias_ref[h] = acc


def _tables_call(rel_bias, hg_lb, bucket):
    return pl.pallas_call(
        _tables_kernel,
        in_specs=[
            pl.BlockSpec(memory_space=pltpu.SMEM),
            pl.BlockSpec(memory_space=pltpu.VMEM),
            pl.BlockSpec(memory_space=pltpu.VMEM),
        ],
        out_specs=[pl.BlockSpec(memory_space=pltpu.VMEM), pl.BlockSpec(memory_space=pltpu.VMEM)],
        out_shape=[jax.ShapeDtypeStruct((1, HG_W), F32),
                   jax.ShapeDtypeStruct((ATT_HEADS, BLOCK, 2 * BLOCK), F32)],
        name="tables",
    )(rel_bias, hg_lb, bucket)


def _bucket_table():
    i = np.arange(BLOCK, dtype=np.int32)[:, None]
    m = np.arange(2 * BLOCK, dtype=np.int32)[None, :]
    dist = i + BLOCK - m
    n = np.maximum(dist, 0)
    max_exact = N_BUCKETS // 2
    nf = np.maximum(n, 1).astype(np.float32)
    large = max_exact + (np.log(nf / np.float32(max_exact)) / np.float32(math.log(MAX_DISTANCE / max_exact))
                         * np.float32(N_BUCKETS - max_exact)).astype(np.int32)
    large = np.minimum(large, N_BUCKETS - 1)
    bucket = np.where(n < max_exact, n, large)
    return jnp.asarray(np.where((dist >= 0) & (dist < BLOCK), bucket, -1), jnp.int32)


def _rms(x):
    return x * lax.rsqrt(jnp.mean(x * x, axis=-1, keepdims=True) + EPS)


def _attention_block(proj_ref, kvprev_ref, bias_ref, sink_ref, mixed_ref, blk, first_tile):
    r0 = blk * BLOCK
    cur = proj_ref[r0:r0 + BLOCK, COL_KV:COL_KV + 512]
    if blk == 0:
        prev = kvprev_ref[...]
    else:
        prev = proj_ref[r0 - BLOCK:r0, COL_KV:COL_KV + 512]
    band = jnp.concatenate([prev, cur], axis=0)
    lane = lax.broadcasted_iota(jnp.int32, (2 * BLOCK, LANES), 1)
    lo = lane < ATT_HEAD_DIM
    col = lax.broadcasted_iota(jnp.int32, (1, 2 * BLOCK), 1)
    if blk == 0:
        pen = jnp.where((col < BLOCK) & first_tile, NEG_INF, 0.0)
    kbs, vbs = [], []
    for g in range(ATT_KV_HEADS):
        kd = band[:, g * 128:(g + 1) * 128]
        vd = band[:, 256 + g * 128:256 + (g + 1) * 128]
        kbs.append(jnp.concatenate([jnp.where(lo, kd, 0.0), jnp.where(lo, 0.0, kd)], axis=0).astype(BF16))
        vbs.append(jnp.concatenate([jnp.where(lo, vd, 0.0), jnp.where(lo, 0.0, vd)], axis=0).astype(BF16))
    n_pairs = ATT_HEADS // 2
    ss = []
    for g in range(ATT_KV_HEADS):
        qg = jnp.concatenate(
            [(proj_ref[r0:r0 + BLOCK, pair * 128:(pair + 1) * 128] * ATT_SCALE).astype(BF16)
             for pair in (2 * g, 2 * g + 1)], axis=0)
        sg = _dot_nt(qg, kbs[g])
        ss += [sg[0:BLOCK], sg[BLOCK:2 * BLOCK]]
    ps, invs = [], []
    for h in range(ATT_HEADS):
        sh = ss[h // 2][:, (h % 2) * 256:(h % 2 + 1) * 256] + bias_ref[h]
        if blk == 0:
            sh = sh + pen
        sink = sink_ref[h]
        mx = jnp.maximum(jnp.max(sh, axis=-1, keepdims=True), sink)
        p = jnp.exp(sh - mx)
        den = jnp.sum(p, axis=-1, keepdims=True) + jnp.exp(sink - mx)
        ps.append(p.astype(BF16))
        invs.append(1.0 / den)
    os_ = []
    for g in range(ATT_KV_HEADS):
        pg = jnp.concatenate([jnp.concatenate(ps[2 * pair:2 * pair + 2], axis=1)
                              for pair in (2 * g, 2 * g + 1)], axis=0)
        og = _dot(pg, vbs[g])
        os_ += [og[0:BLOCK], og[BLOCK:2 * BLOCK]]
    lane_o = lax.broadcasted_iota(jnp.int32, (BLOCK, LANES), 1)
    for pair in range(n_pairs):
        o = os_[pair] * jnp.where(lane_o < ATT_HEAD_DIM, invs[2 * pair], invs[2 * pair + 1])
        mixed_ref[r0:r0 + BLOCK, pair * 128:(pair + 1) * 128] = o.astype(BF16)


def _hgrn_scaled_operands(b, q, k):
    pairs = []
    for m in HG_SAFE_LEVELS:
        qs, ks = [], []
        for bi in range(CHUNK // m):
            rows = slice(bi * m, (bi + 1) * m)
            if bi % 2 == 1:
                ref = b[bi * m - 1:bi * m]
                qs.append(q[rows] * jnp.exp(b[rows] - ref))
                ks.append(jnp.zeros((m, HG_D), F32))
            else:
                ref = b[(bi + 1) * m - 1:(bi + 1) * m]
                qs.append(jnp.zeros((m, HG_D), F32))
                ks.append(k[rows] * jnp.exp(ref - b[rows]))
        pairs.append((m, jnp.concatenate(qs, axis=0).astype(BF16), jnp.concatenate(ks, axis=0).astype(BF16)))
    qs, ks = [], []
    for bi in range(CHUNK // HG_DIAG):
        rows = slice(bi * HG_DIAG, (bi + 1) * HG_DIAG)
        c = b[rows] if bi == 0 else b[rows] - b[bi * HG_DIAG - 1:bi * HG_DIAG]
        qs.append(q[rows] * jnp.exp(c))
        ks.append(k[rows] * jnp.exp(jnp.minimum(-c, HG_EXP_CAP)))
    pairs.append((0, jnp.concatenate(qs, axis=0).astype(BF16), jnp.concatenate(ks, axis=0).astype(BF16)))
    return pairs


def _hgrn_chunks(proj_ref, states, b_all, kk_all, qf_all, hgw_ref, mixed_ref, chunks, masks):
    units = [(ck, hd) for ck in chunks for hd in range(HG_HEADS)]
    rows = {u: slice(u[0] * CHUNK, (u[0] + 1) * CHUNK) for u in units}
    cols = {u: slice(u[1] * HG_D, (u[1] + 1) * HG_D) for u in units}
    bs = {u: b_all[rows[u], cols[u]] for u in units}
    qs = {u: qf_all[rows[u], cols[u]] for u in units}
    ks = {u: kk_all[rows[u], cols[u]] for u in units}
    vs = {u: proj_ref[rows[u], COL_HI + u[1] * HG_D:COL_HI + (u[1] + 1) * HG_D].astype(BF16) for u in units}
    operands = {u: _hgrn_scaled_operands(bs[u], qs[u], ks[u]) for u in units}
    qbs = {u: (qs[u] * jnp.exp(bs[u])).astype(BF16) for u in units}
    bends = {u: bs[u][CHUNK - 1:CHUNK] for u in units}
    kdecs = {u: (ks[u] * jnp.exp(bends[u] - bs[u])).astype(BF16) for u in units}
    prods = {u: [(m, _dot_nt(ql, kl)) for m, ql, kl in operands[u]] for u in units}
    kvs = {u: _dot_tn(vs[u], kdecs[u]) for u in units}
    intras = {}
    for u in units:
        sc = jnp.zeros((CHUNK, CHUNK), F32)
        for m, pr in prods[u]:
            sc = sc + jnp.where(masks[m], pr, 0.0)
        intras[u] = _dot(sc.astype(BF16), vs[u])
    for ck in chunks:
        inters = [_dot_nt(qbs[(ck, hd)], states[hd].astype(BF16)) for hd in range(HG_HEADS)]
        for hd in range(HG_HEADS):
            u = (ck, hd)
            states[hd] = states[hd] * jnp.exp(bends[u]) + kvs[u]
            gate = proj_ref[rows[u], COL_HG + hd * HG_D:COL_HG + (hd + 1) * HG_D]
            on = _rms(intras[u] + inters[hd]) * hgw_ref[...] * (gate * _sigmoid(gate))
            mixed_ref[rows[u], ATT_Q + hd * HG_D:ATT_Q + (hd + 1) * HG_D] = on.astype(BF16)


def _hgrn_masks():
    r = lax.broadcasted_iota(jnp.int32, (CHUNK, CHUNK), 0)
    c = lax.broadcasted_iota(jnp.int32, (CHUNK, CHUNK), 1)
    masks = {}
    for m in HG_SAFE_LEVELS:
        masks[m] = ((r // (2 * m)) == (c // (2 * m))) & (((r // m) % 2) == 1) & (((c // m) % 2) == 0)
    masks[0] = ((r // HG_DIAG) == (c // HG_DIAG)) & (c <= r)
    return masks


def _mix_kernel(sink_ref, x_ref, mod_ref, gmix_ref, win_ref, bin_ref, bias_ref, lb_ref, hgw_ref,
                wout_ref, bout_ref, gffn_ref, wr_ref, br_ref, ltri_ref, utri_ref, etri_ref,
                x1_ref, pos_ref, wts_ref, cnt_ref, cur_ref, own_ref, nt_ref, xs_hbm,
                proj_ref, mixed_ref, kvprev_ref, state_ref, run_ref, win_sc, wout_sc,
                curs_ref, owns_ref, nfs_ref, h2buf, posv, pos_smem, rsem, psem):
    bi = pl.program_id(0)
    j = pl.program_id(1)
    step = bi * pl.num_programs(1) + j
    n_steps = pl.num_programs(0) * pl.num_programs(1)
    slot = step % 2
    n_tiles = xs_hbm.shape[0] // (TM * ROW) - DUMP_TILES

    def rows_wait(sl):
        for _ in range(TOP_K):
            pltpu.make_async_copy(h2buf.at[sl], xs_hbm.at[pl.ds(0, TS * ROW)], rsem.at[sl]).wait()

    def rows_issue(sl):
        for q in range(TOP_K * TS):
            _row_copy(h2buf.at[sl], q % TS, xs_hbm, pos_smem[sl, q // TS, q % TS],
                      rsem.at[sl]).start(priority=q % 2)

    def pos_copy(sl):
        return pltpu.make_async_copy(posv.at[sl], pos_smem.at[sl], psem.at[sl])

    @pl.when(j == 0)
    def _():
        kvprev_ref[...] = jnp.zeros_like(kvprev_ref)
        state_ref[...] = jnp.zeros_like(state_ref)

    @pl.when(step == 0)
    def _():
        run_ref[...] = jnp.zeros_like(run_ref)
        curs_ref[...] = jnp.zeros_like(curs_ref)
        owns_ref[...] = jnp.full(owns_ref.shape, -1.0, F32)
        nfs_ref[...] = jnp.zeros_like(nfs_ref)
        win_sc[...] = win_ref[...]
        wout_sc[...] = wout_ref[...]
        h2buf[1] = jnp.zeros(h2buf.shape[1:], F32)
        spare = n_tiles * TM + lax.broadcasted_iota(jnp.int32, (TOP_K, TS), 0) * TS \
            + lax.broadcasted_iota(jnp.int32, (TOP_K, TS), 1)
        posv[1] = spare
        pos_copy(1).start()

    pl.when(step >= 1)(lambda: rows_wait(slot))

    x = x_ref[0]
    mod = mod_ref[0]
    h = _rms(x) * (gmix_ref[...] * (1.0 + mod[1:2])) + mod[0:1]
    hb = h.astype(BF16)

    def project(c0, c1):
        proj_ref[:, c0:c1] = _dot(hb, win_sc[:, c0:c1]) + bin_ref[:, c0:c1]

    project(0, COL_HQ)
    pos_copy(1 - slot).wait()
    rows_issue(1 - slot)
    n_blk = TS // BLOCK
    hg_cols = (PROJ_COLS - COL_HQ) // n_blk
    for blk in range(n_blk):
        _attention_block(proj_ref, kvprev_ref, bias_ref, sink_ref, mixed_ref, blk, j == 0)
        project(COL_HQ + blk * hg_cols, COL_HQ + (blk + 1) * hg_cols)
    kvprev_ref[...] = proj_ref[TS - BLOCK:TS, COL_KV:COL_KV + 512]

    lbv = lb_ref[...]
    fl = lbv + (1.0 - lbv) * _sigmoid(proj_ref[:, COL_HF:COL_HF + HG_W])
    g = jnp.log(fl)
    g_hi = g.astype(BF16)
    r1 = g - g_hi.astype(F32)
    g_mid = r1.astype(BF16)
    g_lo = (r1 - g_mid.astype(F32)).astype(BF16)
    ltri = ltri_ref[...]
    b_all = _dot(ltri, g_hi) + _dot(ltri, g_mid) + _dot(ltri, g_lo)

    masks = _hgrn_masks()
    qr = proj_ref[:, COL_HQ:COL_HQ + HG_W]
    qf_all = qr * _sigmoid(qr)
    kk_all = 1.0 - fl
    states = [state_ref[hd] for hd in range(HG_HEADS)]
    n_chunks = TS // CHUNK
    for c0 in range(0, n_chunks, HG_CHUNKS_PER_GROUP):
        _hgrn_chunks(proj_ref, states, b_all, kk_all, qf_all, hgw_ref, mixed_ref,
                     range(c0, c0 + HG_CHUNKS_PER_GROUP), masks)
    for hd in range(HG_HEADS):
        state_ref[hd] = states[hd]

    y = _dot(mixed_ref[...], wout_sc[...]) + bout_ref[...]
    x1 = x + mod[2:3] * y
    x1_ref[0] = x1
    h2 = _rms(x1) * (gffn_ref[...] * (1.0 + mod[4:5])) + mod[3:4]
    hb2 = h2buf.at[slot]
    for s in range(SUBLANES):
        hb2[pl.ds(s, TS, stride=SUBLANES), :] = h2[:, s * LANES:(s + 1) * LANES]

    h_hi = h2.astype(BF16)
    h_lo = (h2 - h_hi.astype(F32)).astype(BF16)
    wr = wr_ref[...]
    t1 = _dot_nt(wr, h_hi)
    t2 = _dot_nt(wr[0:N_EXPERTS], h_lo)
    logits = t1[0:N_EXPERTS] + t1[N_EXPERTS:2 * N_EXPERTS] + t2 + br_ref[:, 0:1]

    eidx = lax.broadcasted_iota(jnp.int32, (N_EXPERTS, TS), 0)
    l = logits
    vals, ohs, ids = [], [], []
    for _ in range(TOP_K):
        mx = jnp.max(l, axis=0, keepdims=True)
        ik = jnp.min(jnp.where(l == mx, eidx, N_EXPERTS), axis=0, keepdims=True)
        oh = eidx == ik
        vals.append(mx)
        ids.append(ik)
        ohs.append(oh)
        l = jnp.where(oh, NEG_INF, l)
    es = [jnp.exp(v - vals[0]) for v in vals]
    den = es[0] + es[1] + es[2] + es[3]
    run = run_ref[...]
    c0 = run[:, 0:1]
    base = c0
    utri = utri_ref[...]
    ranks = []
    for kq in range(TOP_K):
        ohf = jnp.where(ohs[kq], 1.0, 0.0)
        pref = _dot(ohf.astype(BF16), utri)
        ranks.append(base + pref)
        base = base + jnp.sum(ohf, axis=1, keepdims=True)
    run_new = jnp.broadcast_to(base, run.shape)
    run_ref[...] = run_new
    cnt_ref[...] = run_new.astype(jnp.int32)

    inv_tm = 1.0 / TM
    prev_ord = jnp.floor((c0 - 1.0) * inv_tm)
    new_e = jnp.floor((base - 1.0) * inv_tm) - prev_ord
    excl = _dot(etri_ref[...], jnp.broadcast_to(new_e, run.shape).astype(BF16))[:, 0:1]
    nf = nfs_ref[0:1, 0:1]
    newbase = nf + excl
    cur = curs_ref[:, 0:1]
    poss = []
    for kq in range(TOP_K):
        r = ranks[kq]
        ordv = jnp.floor(r * inv_tm)
        phys = jnp.where(ordv == prev_ord, cur, newbase + (ordv - prev_ord - 1.0))
        slot_of = phys * TM + (r - ordv * TM)
        poss.append(jnp.sum(jnp.where(ohs[kq], slot_of, 0.0), axis=0, keepdims=True))
    cur_new = jnp.where(new_e > 0.0, newbase + new_e - 1.0, cur)
    curs_ref[...] = jnp.broadcast_to(cur_new, curs_ref.shape)
    nf_new = nf + jnp.sum(new_e, axis=0, keepdims=True)
    nfs_ref[...] = jnp.broadcast_to(nf_new, nfs_ref.shape)
    tile_lane = lax.broadcasted_iota(jnp.int32, (N_EXPERTS, OWN_LANES), 1).astype(F32)
    started = (tile_lane >= newbase) & (tile_lane < newbase + new_e)
    cand = jnp.max(jnp.where(started, eidx[:, 0:1].astype(F32), -1.0), axis=0, keepdims=True)
    owner = jnp.where(cand >= 0.0, cand, owns_ref[...])
    owns_ref[...] = owner
    cur_ref[...] = curs_ref[...].astype(jnp.int32)
    own_ref[...] = owner.astype(jnp.int32)
    nt_ref[...] = nfs_ref[...].astype(jnp.int32)

    pos = jnp.concatenate(poss, axis=0).astype(jnp.int32)
    pos_ref[...] = pos
    posv[slot] = pos
    pos_copy(slot).start()
    wts_ref[...] = jnp.concatenate([e / den for e in es], axis=0)

    @pl.when(step == n_steps - 1)
    def _():
        pos_copy(slot).wait()
        rows_issue(slot)
        rows_wait(1 - slot)
        rows_wait(slot)


def _mix_call(sinks, x, mod, gmix, win, bin_, bias, lb, hgw, wout, bout, gffn, wr, br, ltri, utri, etri):
    B, S, D = x.shape
    N = B * S
    nj = S // TS
    n_tiles = N * TOP_K // TM + N_EXPERTS
    assert n_tiles <= OWN_LANES and TOP_K * TS <= DUMP_TILES * TM
    const2 = lambda b, j: (0, 0)
    tok = lambda b, j: (0, b * nj + j)
    in_specs = [
        pl.BlockSpec(memory_space=pltpu.SMEM),
        pl.BlockSpec((1, TS, D), lambda b, j: (b, j, 0)),
        pl.BlockSpec((1, 8, D), lambda b, j: (b, 0, 0)),
        pl.BlockSpec((1, D), const2),
        pl.BlockSpec((D, PROJ_COLS), const2),
        pl.BlockSpec((1, PROJ_COLS), const2),
        pl.BlockSpec((ATT_HEADS, BLOCK, 2 * BLOCK), lambda b, j: (0, 0, 0)),
        pl.BlockSpec((1, HG_W), const2),
        pl.BlockSpec((1, HG_D), const2),
        pl.BlockSpec((D, D), const2),
        pl.BlockSpec((1, D), const2),
        pl.BlockSpec((1, D), const2),
        pl.BlockSpec((2 * N_EXPERTS, D), const2),
        pl.BlockSpec((N_EXPERTS, LANES), const2),
        pl.BlockSpec((TS, TS), const2),
        pl.BlockSpec((TS, TS), const2),
        pl.BlockSpec((N_EXPERTS, N_EXPERTS), const2),
    ]
    out_specs = [
        pl.BlockSpec((1, TS, D), lambda b, j: (b, j, 0)),
        pl.BlockSpec((TOP_K, TS), tok),
        pl.BlockSpec((TOP_K, TS), tok),
        pl.BlockSpec((N_EXPERTS, LANES), const2),
        pl.BlockSpec((N_EXPERTS, LANES), const2),
        pl.BlockSpec((1, OWN_LANES), const2),
        pl.BlockSpec((1, LANES), const2),
        pl.BlockSpec(memory_space=pl.ANY),
    ]
    out_shape = [
        jax.ShapeDtypeStruct((B, S, D), F32),
        jax.ShapeDtypeStruct((TOP_K, N), jnp.int32),
        jax.ShapeDtypeStruct((TOP_K, N), F32),
        jax.ShapeDtypeStruct((N_EXPERTS, LANES), jnp.int32),
        jax.ShapeDtypeStruct((N_EXPERTS, LANES), jnp.int32),
        jax.ShapeDtypeStruct((1, OWN_LANES), jnp.int32),
        jax.ShapeDtypeStruct((1, LANES), jnp.int32),
        jax.ShapeDtypeStruct(((n_tiles + DUMP_TILES) * TM * ROW, LANES), F32),
    ]
    scratch = [
        pltpu.VMEM((TS, PROJ_COLS), F32),
        pltpu.VMEM((TS, D), BF16),
        pltpu.VMEM((BLOCK, 512), F32),
        pltpu.VMEM((HG_HEADS, HG_D, HG_D), F32),
        pltpu.VMEM((N_EXPERTS, LANES), F32),
        pltpu.VMEM((D, PROJ_COLS), BF16),
        pltpu.VMEM((D, D), BF16),
        pltpu.VMEM((N_EXPERTS, LANES), F32),
        pltpu.VMEM((1, OWN_LANES), F32),
        pltpu.VMEM((1, LANES), F32),
        pltpu.VMEM((2, TS * ROW, LANES), F32),
        pltpu.VMEM((2, TOP_K, TS), jnp.int32),
        pltpu.SMEM((2, TOP_K, TS), jnp.int32),
        pltpu.SemaphoreType.DMA((2,)),
        pltpu.SemaphoreType.DMA((2,)),
    ]
    return pl.pallas_call(
        _mix_kernel,
        grid=(B, nj),
        in_specs=in_specs,
        out_specs=out_specs,
        out_shape=out_shape,
        scratch_shapes=scratch,
        compiler_params=pltpu.CompilerParams(
            dimension_semantics=("arbitrary", "arbitrary"), vmem_limit_bytes=VMEM_LIMIT),
        name="mix_router",
    )(sinks, x, mod, gmix, win, bin_, bias, lb, hgw, wout, bout, gffn, wr, br, ltri, utri, etri)


TM = 512
DUMP_TILES = 4
OWN_LANES = 384
FF_CHUNKS = 4
TD = 256
ROW = SUBLANES


def _row_copy(src, src_row, dst, dst_row, sem):
    return pltpu.make_async_copy(src.at[pl.ds(pl.multiple_of(src_row * ROW, ROW), ROW)],
                                 dst.at[pl.ds(pl.multiple_of(dst_row * ROW, ROW), ROW)], sem)


def _col_from_row(w_row, n):
    rr = lax.broadcasted_iota(jnp.int32, (n, n), 0)
    cc = lax.broadcasted_iota(jnp.int32, (n, n), 1)
    return jnp.sum(jnp.where(rr == cc, w_row, 0.0), axis=1, keepdims=True)


def _rows_to_matrix(ref, base, n):
    return jnp.concatenate([ref[pl.ds(base + s, n, stride=ROW), :] for s in range(ROW)], axis=1)


def _padfill_kernel(last_ref, fill_ref, nt_ref, xs_in, xs_out, zbuf, sem):
    del xs_in
    n_tiles = xs_out.shape[0] // (TM * ROW) - DUMP_TILES
    zbuf[...] = jnp.zeros_like(zbuf)
    sites = []
    for e in range(N_EXPERTS):
        fill = fill_ref[e]
        row = last_ref[e] * TM + fill
        rem = TM - fill
        n = TM // 2
        while n >= 1:
            cond = (rem & n) != 0
            sites.append((cond, row, n))
            row = row + jnp.where(cond, n, 0)
            n //= 2
    for u in range(N_EXPERTS):
        tile = nt_ref[0] + u
        sites.append((tile < n_tiles, tile * TM, TM))

    def cp(row, n):
        return pltpu.make_async_copy(
            zbuf.at[pl.ds(0, n * ROW)], xs_out.at[pl.ds(pl.multiple_of(row * ROW, ROW), n * ROW)], sem.at[0])

    for cond, row, n in sites:
        pl.when(cond)(lambda row=row, n=n: cp(row, n).start())
    for cond, row, n in sites:
        pl.when(cond)(lambda row=row, n=n: cp(row, n).wait())


def _padfill_call(last, fill, nt, xs):
    grid_spec = pltpu.PrefetchScalarGridSpec(
        num_scalar_prefetch=3,
        grid=(1,),
        in_specs=[pl.BlockSpec(memory_space=pl.ANY)],
        out_specs=pl.BlockSpec(memory_space=pl.ANY),
        scratch_shapes=[pltpu.VMEM((TM * ROW, LANES), F32), pltpu.SemaphoreType.DMA((1,))],
    )
    return pl.pallas_call(
        _padfill_kernel,
        grid_spec=grid_spec,
        out_shape=jax.ShapeDtypeStruct(xs.shape, F32),
        input_output_aliases={3: 0},
        compiler_params=pltpu.CompilerParams(dimension_semantics=("arbitrary",)),
        name="moe_padfill",
    )(last, fill, nt, xs)


def _moe_kernel(order_ref, te_ref, nt_ref, nxt_ref, par_ref, xs_ref, wgu_hbm, bgu_ref, wd_hbm, bd_ref, perm_ref,
                ys_ref, wgu_st, wd_st, wgu_bf, wd_bf, wsem):
    del order_ref
    j = pl.program_id(0)
    nt = nt_ref[0]

    def weight_copies(e):
        s = par_ref[e]
        return (pltpu.make_async_copy(wgu_hbm.at[e], wgu_st.at[s], wsem.at[0, s]),
                pltpu.make_async_copy(wd_hbm.at[e], wd_st.at[s], wsem.at[1, s]))

    @pl.when(j == 0)
    def _():
        for cp in weight_copies(te_ref[0]):
            cp.start()

    @pl.when(j < nt)
    def _():
        e = te_ref[j]
        first = (j == 0) | (te_ref[jnp.maximum(j - 1, 0)] != e)

        @pl.when(first)
        def _():
            for cp in weight_copies(e):
                cp.wait()
            s = par_ref[e]
            perm = perm_ref[...]
            for g in range(2 * D_FF // 256):
                w = wgu_st[s, :, g * 256:(g + 1) * 256].astype(BF16)
                wgu_bf[:, g * 256:(g + 1) * 256] = _dot(w, perm).astype(BF16)
            wd_bf[...] = wd_st[s].astype(BF16)
            nxt = nxt_ref[e]

            @pl.when(nxt >= 0)
            def _():
                for cp in weight_copies(nxt):
                    cp.start()

        h = _rows_to_matrix(xs_ref, 0, TM).astype(BF16)
        gw = 2 * D_FF // FF_CHUNKS
        hw = D_FF // FF_CHUNKS

        def up_proj(c):
            return _dot(h, wgu_bf[:, c * gw:(c + 1) * gw]) + bgu_ref[0, :, c * gw:(c + 1) * gw]

        def activate(gu):
            parts = []
            for g in range(gw // 256):
                gate = jnp.minimum(gu[:, g * 256:g * 256 + 128], SWIGLU_LIMIT)
                up = jnp.clip(gu[:, g * 256 + 128:(g + 1) * 256], -SWIGLU_LIMIT, SWIGLU_LIMIT)
                parts.append(((up + 1.0) * gate * _sigmoid(SWIGLU_ALPHA * gate)).astype(BF16))
            return jnp.concatenate(parts, axis=1)

        y = bd_ref[0]
        gu_next = up_proj(0)
        for c in range(FF_CHUNKS):
            gu = gu_next
            if c + 1 < FF_CHUNKS:
                gu_next = up_proj(c + 1)
            y = y + _dot(activate(gu), wd_bf[c * hw:(c + 1) * hw, :])
        for s in range(ROW):
            ys_ref[pl.ds(s, TM, stride=ROW), :] = y[:, s * LANES:(s + 1) * LANES]

    @pl.when(j >= nt)
    def _():
        ys_ref[...] = jnp.zeros_like(ys_ref)


def _moe_call(order, te, nt, nxt, par, xs, wgu, bgu, wd, bd, perm):
    n_tiles = xs.shape[0] // (TM * ROW) - DUMP_TILES
    tile = lambda j, order, te, nt, nxt, par: (order[jnp.minimum(j, nt[0] - 1)], 0)
    exp = lambda j, order, te, nt, nxt, par: (te[jnp.minimum(j, nt[0] - 1)], 0, 0)
    grid_spec = pltpu.PrefetchScalarGridSpec(
        num_scalar_prefetch=5,
        grid=(n_tiles,),
        in_specs=[
            pl.BlockSpec((TM * ROW, LANES), tile),
            pl.BlockSpec(memory_space=pl.ANY),
            pl.BlockSpec((1, 1, 2 * D_FF), exp),
            pl.BlockSpec(memory_space=pl.ANY),
            pl.BlockSpec((1, 1, D_MODEL), exp),
            pl.BlockSpec((256, 256), lambda j, order, te, nt, nxt, par: (0, 0)),
        ],
        out_specs=pl.BlockSpec((TM * ROW, LANES), lambda j, order, te, nt, nxt, par: (order[j], 0)),
        scratch_shapes=[
            pltpu.VMEM((2, D_MODEL, 2 * D_FF), F32),
            pltpu.VMEM((2, D_FF, D_MODEL), F32),
            pltpu.VMEM((D_MODEL, 2 * D_FF), BF16),
            pltpu.VMEM((D_FF, D_MODEL), BF16),
            pltpu.SemaphoreType.DMA((2, 2)),
        ],
    )
    return pl.pallas_call(
        _moe_kernel,
        grid_spec=grid_spec,
        out_shape=jax.ShapeDtypeStruct((n_tiles * TM * ROW, LANES), F32),
        compiler_params=pltpu.CompilerParams(
            dimension_semantics=("arbitrary",), vmem_limit_bytes=VMEM_LIMIT),
        name="moe_experts",
    )(order, te, nt, nxt, par, xs, wgu, bgu, wd, bd, perm)


def _final_kernel(pa_ref, pb_ref, pc_ref, wts_ref, x1_ref, mod_ref, gfin_ref, ys_hbm, out_ref,
                  gbuf_a, gbuf_b, sem):
    i = pl.program_id(0)
    n = pl.num_programs(0)

    def issue(pref, buf, s):
        for q in range(TOP_K * TD):
            _row_copy(ys_hbm, pref[0, 0, q], buf, q, sem.at[s]).start(priority=q % 2)

    def wait(buf, s):
        pltpu.make_async_copy(ys_hbm.at[pl.ds(0, TOP_K * TD * ROW)], buf, sem.at[s]).wait()

    def combine(buf, half):
        acc = jnp.zeros((TD, D_MODEL), F32)
        for kq in range(TOP_K):
            v = _rows_to_matrix(buf, kq * TD * ROW, TD)
            acc = acc + _col_from_row(wts_ref[kq:kq + 1, half * TD:(half + 1) * TD], TD) * v
        xf = x1_ref[half * TD:(half + 1) * TD, :] + mod_ref[0][5:6] * acc
        out_ref[half * TD:(half + 1) * TD, :] = _rms(xf) * gfin_ref[...]

    pl.when(i == 0)(lambda: issue(pa_ref, gbuf_a, 0))
    wait(gbuf_a, 0)
    issue(pb_ref, gbuf_b, 1)
    combine(gbuf_a, 0)
    wait(gbuf_b, 1)
    issue(pc_ref, gbuf_a, 0)
    combine(gbuf_b, 1)
    pl.when(i == n - 1)(lambda: wait(gbuf_a, 0))


def _final_call(pos_t, wts, x1, mod, gfin, ys, S):
    N, D = x1.shape
    nstep = N // (2 * TD)
    per_b = S // (2 * TD)
    pos_spec = lambda f: pl.BlockSpec((1, 1, TOP_K * TD), lambda i: (f(i), 0, 0), memory_space=pltpu.SMEM)
    return pl.pallas_call(
        _final_kernel,
        grid=(nstep,),
        in_specs=[
            pos_spec(lambda i: 2 * i),
            pos_spec(lambda i: 2 * i + 1),
            pos_spec(lambda i: jnp.minimum(2 * i + 2, 2 * nstep - 2)),
            pl.BlockSpec((TOP_K, 2 * TD), lambda i: (0, i)),
            pl.BlockSpec((2 * TD, D), lambda i: (i, 0)),
            pl.BlockSpec((1, 8, D), lambda i: (i // per_b, 0, 0)),
            pl.BlockSpec((1, D), lambda i: (0, 0)),
            pl.BlockSpec(memory_space=pl.ANY),
        ],
        out_specs=pl.BlockSpec((2 * TD, D), lambda i: (i, 0)),
        out_shape=jax.ShapeDtypeStruct((N, D), F32),
        scratch_shapes=[
            pltpu.VMEM((TOP_K * TD * ROW, LANES), F32),
            pltpu.VMEM((TOP_K * TD * ROW, LANES), F32),
            pltpu.SemaphoreType.DMA((2,)),
        ],
        compiler_params=pltpu.CompilerParams(
            dimension_semantics=("arbitrary",), vmem_limit_bytes=VMEM_LIMIT),
        name="combine_norm",
    )(pos_t, pos_t, pos_t, wts, x1, mod, gfin, ys)


def _route_plan(pos, cnt, cur, own, ntiles, N):
    n_tiles = N * TOP_K // TM + N_EXPERTS
    counts = cnt[:, 0]
    nt = ntiles[0, 0:1]
    owner = own[0, :n_tiles]
    tiles = jnp.arange(n_tiles, dtype=jnp.int32)
    experts = jnp.arange(N_EXPERTS, dtype=jnp.int32)
    used = tiles < nt[0]
    key = jnp.where(used, owner, N_EXPERTS) * OWN_LANES + tiles
    place = jnp.sum(key[None, :] < key[:, None], axis=1)
    at = place[None, :] == tiles[:, None]
    order = jnp.sum(jnp.where(at, tiles[None, :], 0), axis=1).astype(jnp.int32)
    te = jnp.minimum(jnp.sum(jnp.where(at, jnp.where(used, owner, N_EXPERTS)[None, :], 0), axis=1),
                     N_EXPERTS - 1).astype(jnp.int32)
    ntile = jnp.sum((owner[None, :] == experts[:, None]) & used[None, :], axis=1)
    nonempty = ntile > 0
    last = jnp.where(nonempty, cur[:, 0], 0).astype(jnp.int32)
    fill = jnp.where(nonempty, counts - (ntile - 1) * TM, TM).astype(jnp.int32)
    pos_t = jnp.transpose(pos.reshape(TOP_K, N // TD, TD), (1, 0, 2)).reshape(N // TD, 1, TOP_K * TD)
    par = ((jnp.cumsum(nonempty) - 1) % 2).astype(jnp.int32)
    later = nonempty[None, :] & (experts[None, :] > experts[:, None])
    nxt = jnp.min(jnp.where(later, experts[None, :], N_EXPERTS), axis=1)
    nxt = jnp.where(nxt == N_EXPERTS, -1, nxt).astype(jnp.int32)
    return order, te, nt, nxt, par, last, fill, pos_t


def _tri_constants():
    r = np.arange(TS)[:, None]
    c = np.arange(TS)[None, :]
    ltri = ((r // CHUNK) == (c // CHUNK)) & (c <= r)
    utri = r < c
    return jnp.asarray(ltri, BF16), jnp.asarray(utri, BF16)


def kernel(x, c, w_ada, b_ada, g_mix, w_in, b_in, attn_sinks, rel_bias, hg_lb, hg_norm_w, w_out, b_out, g_ffn, w_router, b_router, w_gate_up, b_gate_up, w_down, b_down, g_final):
    B, S, D = x.shape
    N = B * S
    mod6 = _prep_call(c, w_ada[0], b_ada)
    mod = jnp.pad(jnp.transpose(mod6, (1, 0, 2)), ((0, 0), (0, 2), (0, 0)))
    lb, bias = _tables_call(rel_bias, hg_lb, _bucket_table())

    wi, bi_ = w_in[0], b_in[0]
    def cols(a):
        aq, ak, av, rest = a[..., :512], a[..., 512:640], a[..., 640:768], a[..., 768:]
        k0, k1, v0, v1 = ak[..., :64], ak[..., 64:], av[..., :64], av[..., 64:]
        return jnp.concatenate([aq, k0, k0, k1, k1, v0, v0, v1, v1, rest], axis=-1)
    win = cols(wi).astype(BF16)
    bin_ = cols(bi_)[None, :]
    wrt = jnp.transpose(w_router[0])
    wr_hi = wrt.astype(BF16)
    wr_lo = (wrt - wr_hi.astype(F32)).astype(BF16)
    wr = jnp.concatenate([wr_hi, wr_lo], axis=0)
    br = jnp.broadcast_to(b_router[0][:, None], (N_EXPERTS, LANES))
    ltri, utri = _tri_constants()
    etri = jnp.asarray(np.tril(np.ones((N_EXPERTS, N_EXPERTS), np.float32), -1), BF16)

    x1, pos, wts, cnt, cur, own, ntiles, xs = _mix_call(
        attn_sinks[0], x, mod, g_mix, win, bin_, bias, lb, hg_norm_w, w_out[0].astype(BF16),
        b_out, g_ffn, wr, br, ltri, utri, etri)

    order, te, nt, nxt, par, last, fill, pos_t = _route_plan(pos, cnt, cur, own, ntiles, N)
    ii = np.arange(128)
    perm_np = np.zeros((256, 256), np.float32)
    perm_np[2 * ii, ii] = 1.0
    perm_np[2 * ii + 1, 128 + ii] = 1.0
    perm = jnp.asarray(perm_np, BF16)
    bgu = b_gate_up[0].reshape(N_EXPERTS, D_FF // LANES, LANES, 2)
    bgu = jnp.transpose(bgu, (0, 1, 3, 2)).reshape(N_EXPERTS, 1, 2 * D_FF)
    xs = _padfill_call(last, fill, nt, xs)
    ys = _moe_call(order, te, nt, nxt, par, xs, w_gate_up[0], bgu, w_down[0], b_down[0][:, None, :], perm)
    out = _final_call(pos_t, wts, x1.reshape(N, D), mod, g_final[None, :], ys, S)
    return out.reshape(B, S, D)
```

```python
import functools
import math

import numpy as np
import jax
import jax.numpy as jnp
from jax import lax
from jax.experimental import pallas as pl
from jax.experimental.pallas import tpu as pltpu

D_MODEL = 1024
ATT_HEAD_DIM = 64
ATT_HEADS = 8
ATT_KV_HEADS = 2
ATT_SCALE = ATT_HEAD_DIM ** -0.5
BLOCK = 128
N_BUCKETS = 32
MAX_DISTANCE = 128
HG_HEADS = 4
HG_D = 128
CHUNK = 64
N_EXPERTS = 32
TOP_K = 4
D_FF = 1024
SWIGLU_LIMIT = 7.0
SWIGLU_ALPHA = 1.702
EPS = 1e-5

ATT_Q = ATT_HEADS * ATT_HEAD_DIM
HG_W = HG_HEADS * HG_D
COL_Q = 0
COL_KV = ATT_Q
COL_HQ = COL_KV + 4 * 128
COL_HF = COL_HQ + HG_W
COL_HI = COL_HF + HG_W
COL_HG = COL_HI + HG_W
PROJ_COLS = COL_HG + HG_W

LANES = 128
SUBLANES = 8
TS = 512
VMEM_LIMIT = 56 * 1024 * 1024
HG_SAFE_LEVELS = (32, 16, 8)
HG_DIAG = 8
HG_CHUNKS_PER_GROUP = 2
HG_EXP_CAP = 80.0

F32 = jnp.float32
BF16 = jnp.bfloat16
NEG_INF = float("-inf")


def _dot(a, b):
    return jnp.dot(a, b, preferred_element_type=F32)


def _dot_nt(a, b):
    return lax.dot_general(a, b, (((1,), (1,)), ((), ())), preferred_element_type=F32)


def _dot_tn(a, b):
    return lax.dot_general(a, b, (((0,), (0,)), ((), ())), preferred_element_type=F32)


def _sigmoid(x):
    return 1.0 / (1.0 + jnp.exp(-x))


def _prep_kernel(c_ref, w_ref, b_ref, o_ref):
    c = c_ref[...]
    cond = c * _sigmoid(c)
    o_ref[0] = jnp.dot(cond, w_ref[...], precision=lax.Precision.HIGHEST,
                       preferred_element_type=F32) + b_ref[...]


def _prep_call(c, w_ada, b_ada):
    B = c.shape[0]
    return pl.pallas_call(
        _prep_kernel,
        grid=(6,),
        in_specs=[
            pl.BlockSpec((B, D_MODEL), lambda j: (0, 0)),
            pl.BlockSpec((D_MODEL, D_MODEL), lambda j: (0, j)),
            pl.BlockSpec((1, D_MODEL), lambda j: (0, j)),
        ],
        out_specs=pl.BlockSpec((1, B, D_MODEL), lambda j: (j, 0, 0)),
        out_shape=jax.ShapeDtypeStruct((6, B, D_MODEL), F32),
        compiler_params=pltpu.CompilerParams(dimension_semantics=("arbitrary",)),
        name="adaln_mod",
    )(c, w_ada, b_ada)


def _tables_kernel(rb_ref, lbp_ref, bucket_ref, lb_ref, bias_ref):
    p = lbp_ref[...]
    e = jnp.exp(p - jnp.max(p, axis=0, keepdims=True))
    sm = e / jnp.sum(e, axis=0, keepdims=True)
    lb_ref[...] = sm[0:1]
    bucket = bucket_ref[...]
    for h in range(ATT_HEADS):
        acc = jnp.full(bucket.shape, NEG_INF, F32)
        for bk in range(N_BUCKETS):
            acc = jnp.where(bucket == bk, rb_ref[bk, h], acc)
        bias_ref[h] = acc


def _tables_call(rel_bias, hg_lb, bucket):
    return pl.pallas_call(
        _tables_kernel,
        in_specs=[
            pl.BlockSpec(memory_space=pltpu.SMEM),
            pl.BlockSpec(memory_space=pltpu.VMEM),
            pl.BlockSpec(memory_space=pltpu.VMEM),
        ],
        out_specs=[pl.BlockSpec(memory_space=pltpu.VMEM), pl.BlockSpec(memory_space=pltpu.VMEM)],
        out_shape=[jax.ShapeDtypeStruct((1, HG_W), F32),
                   jax.ShapeDtypeStruct((ATT_HEADS, BLOCK, 2 * BLOCK), F32)],
        name="tables",
    )(rel_bias, hg_lb, bucket)


def _bucket_table():
    i = np.arange(BLOCK, dtype=np.int32)[:, None]
    m = np.arange(2 * BLOCK, dtype=np.int32)[None, :]
    dist = i + BLOCK - m
    n = np.maximum(dist, 0)
    max_exact = N_BUCKETS // 2
    nf = np.maximum(n, 1).astype(np.float32)
    large = max_exact + (np.log(nf / np.float32(max_exact)) / np.float32(math.log(MAX_DISTANCE / max_exact))
                         * np.float32(N_BUCKETS - max_exact)).astype(np.int32)
    large = np.minimum(large, N_BUCKETS - 1)
    bucket = np.where(n < max_exact, n, large)
    return jnp.asarray(np.where((dist >= 0) & (dist < BLOCK), bucket, -1), jnp.int32)


def _rms(x):
    return x * lax.rsqrt(jnp.mean(x * x, axis=-1, keepdims=True) + EPS)


def _attention_block(proj_ref, kvprev_ref, bias_ref, sink_ref, mixed_ref, blk, first_tile):
    r0 = blk * BLOCK
    cur = proj_ref[r0:r0 + BLOCK, COL_KV:COL_KV + 512]
    if blk == 0:
        prev = kvprev_ref[...]
    else:
        prev = proj_ref[r0 - BLOCK:r0, COL_KV:COL_KV + 512]
    band = jnp.concatenate([prev, cur], axis=0)
    lane = lax.broadcasted_iota(jnp.int32, (2 * BLOCK, LANES), 1)
    lo = lane < ATT_HEAD_DIM
    col = lax.broadcasted_iota(jnp.int32, (1, 2 * BLOCK), 1)
    if blk == 0:
        pen = jnp.where((col < BLOCK) & first_tile, NEG_INF, 0.0)
    kbs, vbs = [], []
    for g in range(ATT_KV_HEADS):
        kd = band[:, g * 128:(g + 1) * 128]
        vd = band[:, 256 + g * 128:256 + (g + 1) * 128]
        kbs.append(jnp.concatenate([jnp.where(lo, kd, 0.0), jnp.where(lo, 0.0, kd)], axis=0).astype(BF16))
        vbs.append(jnp.concatenate([jnp.where(lo, vd, 0.0), jnp.where(lo, 0.0, vd)], axis=0).astype(BF16))
    n_pairs = ATT_HEADS // 2
    ss = []
    for g in range(ATT_KV_HEADS):
        qg = jnp.concatenate(
            [(proj_ref[r0:r0 + BLOCK, pair * 128:(pair + 1) * 128] * ATT_SCALE).astype(BF16)
             for pair in (2 * g, 2 * g + 1)], axis=0)
        sg = _dot_nt(qg, kbs[g])
        ss += [sg[0:BLOCK], sg[BLOCK:2 * BLOCK]]
    ps, invs = [], []
    for h in range(ATT_HEADS):
        sh = ss[h // 2][:, (h % 2) * 256:(h % 2 + 1) * 256] + bias_ref[h]
        if blk == 0:
            sh = sh + pen
        sink = sink_ref[h]
        mx = jnp.maximum(jnp.max(sh, axis=-1, keepdims=True), sink)
        p = jnp.exp(sh - mx)
        den = jnp.sum(p, axis=-1, keepdims=True) + jnp.exp(sink - mx)
        ps.append(p.astype(BF16))
        invs.append(1.0 / den)
    os_ = []
    for g in range(ATT_KV_HEADS):
        pg = jnp.concatenate([jnp.concatenate(ps[2 * pair:2 * pair + 2], axis=1)
                              for pair in (2 * g, 2 * g + 1)], axis=0)
        og = _dot(pg, vbs[g])
        os_ += [og[0:BLOCK], og[BLOCK:2 * BLOCK]]
    lane_o = lax.broadcasted_iota(jnp.int32, (BLOCK, LANES), 1)
    for pair in range(n_pairs):
        o = os_[pair] * jnp.where(lane_o < ATT_HEAD_DIM, invs[2 * pair], invs[2 * pair + 1])
        mixed_ref[r0:r0 + BLOCK, pair * 128:(pair + 1) * 128] = o.astype(BF16)


def _hgrn_scaled_operands(b, q, k):
    pairs = []
    for m in HG_SAFE_LEVELS:
        qs, ks = [], []
        for bi in range(CHUNK // m):
            rows = slice(bi * m, (bi + 1) * m)
            if bi % 2 == 1:
                ref = b[bi * m - 1:bi * m]
                qs.append(q[rows] * jnp.exp(b[rows] - ref))
                ks.append(jnp.zeros((m, HG_D), F32))
            else:
                ref = b[(bi + 1) * m - 1:(bi + 1) * m]
                qs.append(jnp.zeros((m, HG_D), F32))
                ks.append(k[rows] * jnp.exp(ref - b[rows]))
        pairs.append((m, jnp.concatenate(qs, axis=0).astype(BF16), jnp.concatenate(ks, axis=0).astype(BF16)))
    qs, ks = [], []
    for bi in range(CHUNK // HG_DIAG):
        rows = slice(bi * HG_DIAG, (bi + 1) * HG_DIAG)
        c = b[rows] if bi == 0 else b[rows] - b[bi * HG_DIAG - 1:bi * HG_DIAG]
        qs.append(q[rows] * jnp.exp(c))
        ks.append(k[rows] * jnp.exp(jnp.minimum(-c, HG_EXP_CAP)))
    pairs.append((0, jnp.concatenate(qs, axis=0).astype(BF16), jnp.concatenate(ks, axis=0).astype(BF16)))
    return pairs


def _hgrn_chunks(proj_ref, states, b_all, kk_all, qf_all, hgw_ref, mixed_ref, chunks, masks):
    units = [(ck, hd) for ck in chunks for hd in range(HG_HEADS)]
    rows = {u: slice(u[0] * CHUNK, (u[0] + 1) * CHUNK) for u in units}
    cols = {u: slice(u[1] * HG_D, (u[1] + 1) * HG_D) for u in units}
    bs = {u: b_all[rows[u], cols[u]] for u in units}
    qs = {u: qf_all[rows[u], cols[u]] for u in units}
    ks = {u: kk_all[rows[u], cols[u]] for u in units}
    vs = {u: proj_ref[rows[u], COL_HI + u[1] * HG_D:COL_HI + (u[1] + 1) * HG_D].astype(BF16) for u in units}
    operands = {u: _hgrn_scaled_operands(bs[u], qs[u], ks[u]) for u in units}
    qbs = {u: (qs[u] * jnp.exp(bs[u])).astype(BF16) for u in units}
    bends = {u: bs[u][CHUNK - 1:CHUNK] for u in units}
    kdecs = {u: (ks[u] * jnp.exp(bends[u] - bs[u])).astype(BF16) for u in units}
    prods = {u: [(m, _dot_nt(ql, kl)) for m, ql, kl in operands[u]] for u in units}
    kvs = {u: _dot_tn(vs[u], kdecs[u]) for u in units}
    intras = {}
    for u in units:
        sc = jnp.zeros((CHUNK, CHUNK), F32)
        for m, pr in prods[u]:
            sc = sc + jnp.where(masks[m], pr, 0.0)
        intras[u] = _dot(sc.astype(BF16), vs[u])
    for ck in chunks:
        inters = [_dot_nt(qbs[(ck, hd)], states[hd].astype(BF16)) for hd in range(HG_HEADS)]
        for hd in range(HG_HEADS):
            u = (ck, hd)
            states[hd] = states[hd] * jnp.exp(bends[u]) + kvs[u]
            gate = proj_ref[rows[u], COL_HG + hd * HG_D:COL_HG + (hd + 1) * HG_D]
            on = _rms(intras[u] + inters[hd]) * hgw_ref[...] * (gate * _sigmoid(gate))
            mixed_ref[rows[u], ATT_Q + hd * HG_D:ATT_Q + (hd + 1) * HG_D] = on.astype(BF16)


def _hgrn_masks():
    r = lax.broadcasted_iota(jnp.int32, (CHUNK, CHUNK), 0)
    c = lax.broadcasted_iota(jnp.int32, (CHUNK, CHUNK), 1)
    masks = {}
    for m in HG_SAFE_LEVELS:
        masks[m] = ((r // (2 * m)) == (c // (2 * m))) & (((r // m) % 2) == 1) & (((c // m) % 2) == 0)
    masks[0] = ((r // HG_DIAG) == (c // HG_DIAG)) & (c <= r)
    return masks


def _mix_kernel(sink_ref, x_ref, mod_ref, gmix_ref, win_ref, bin_ref, bias_ref, lb_ref, hgw_ref,
                wout_ref, bout_ref, gffn_ref, wr_ref, br_ref, ltri_ref, utri_ref, etri_ref,
                x1_ref, pos_ref, wts_ref, own_ref, nt_ref, xs_hbm,
                proj_ref, mixed_ref, kvprev_ref, state_ref, run_ref, win_sc, wout_sc,
                curs_ref, owns_ref, nfs_ref, h2buf, posv, pos_smem, metav, meta_smem, rsem, psem):
    bi = pl.program_id(0)
    j = pl.program_id(1)
    step = bi * pl.num_programs(1) + j
    n_steps = pl.num_programs(0) * pl.num_programs(1)
    slot = step % 2
    n_tiles = xs_hbm.shape[0] // (TM * ROW) - DUMP_TILES

    def rows_wait(sl):
        for _ in range(TOP_K):
            pltpu.make_async_copy(h2buf.at[sl], xs_hbm.at[pl.ds(0, TS * ROW)], rsem.at[sl]).wait()

    def rows_issue(sl):
        for q in range(TOP_K * TS):
            _row_copy(h2buf.at[sl], q % TS, xs_hbm, pos_smem[sl, q // TS, q % TS],
                      rsem.at[sl]).start(priority=q % 2)

    def pos_copy(sl):
        return pltpu.make_async_copy(posv.at[sl], pos_smem.at[sl], psem.at[sl])

    @pl.when(j == 0)
    def _():
        kvprev_ref[...] = jnp.zeros_like(kvprev_ref)
        state_ref[...] = jnp.zeros_like(state_ref)

    @pl.when(step == 0)
    def _():
        run_ref[...] = jnp.zeros_like(run_ref)
        curs_ref[...] = jnp.zeros_like(curs_ref)
        owns_ref[...] = jnp.full(owns_ref.shape, -1.0, F32)
        nfs_ref[...] = jnp.zeros_like(nfs_ref)
        win_sc[...] = win_ref[...]
        wout_sc[...] = wout_ref[...]
        h2buf[1] = jnp.zeros(h2buf.shape[1:], F32)
        spare = n_tiles * TM + lax.broadcasted_iota(jnp.int32, (TOP_K, TS), 0) * TS \
            + lax.broadcasted_iota(jnp.int32, (TOP_K, TS), 1)
        posv[1] = spare
        pos_copy(1).start()

    pl.when(step >= 1)(lambda: rows_wait(slot))

    x = x_ref[0]
    mod = mod_ref[0]
    h = _rms(x) * (gmix_ref[...] * (1.0 + mod[1:2])) + mod[0:1]
    hb = h.astype(BF16)

    def project(c0, c1):
        proj_ref[:, c0:c1] = _dot(hb, win_sc[:, c0:c1]) + bin_ref[:, c0:c1]

    project(0, COL_HQ)
    pos_copy(1 - slot).wait()
    rows_issue(1 - slot)
    n_blk = TS // BLOCK
    hg_cols = (PROJ_COLS - COL_HQ) // n_blk
    for blk in range(n_blk):
        _attention_block(proj_ref, kvprev_ref, bias_ref, sink_ref, mixed_ref, blk, j == 0)
        project(COL_HQ + blk * hg_cols, COL_HQ + (blk + 1) * hg_cols)
    kvprev_ref[...] = proj_ref[TS - BLOCK:TS, COL_KV:COL_KV + 512]

    lbv = lb_ref[...]
    fl = lbv + (1.0 - lbv) * _sigmoid(proj_ref[:, COL_HF:COL_HF + HG_W])
    g = jnp.log(fl)
    g_hi = g.astype(BF16)
    r1 = g - g_hi.astype(F32)
    g_mid = r1.astype(BF16)
    g_lo = (r1 - g_mid.astype(F32)).astype(BF16)
    ltri = ltri_ref[...]
    b_all = _dot(ltri, g_hi) + _dot(ltri, g_mid) + _dot(ltri, g_lo)

    masks = _hgrn_masks()
    qr = proj_ref[:, COL_HQ:COL_HQ + HG_W]
    qf_all = qr * _sigmoid(qr)
    kk_all = 1.0 - fl
    states = [state_ref[hd] for hd in range(HG_HEADS)]
    n_chunks = TS // CHUNK
    for c0 in range(0, n_chunks, HG_CHUNKS_PER_GROUP):
        _hgrn_chunks(proj_ref, states, b_all, kk_all, qf_all, hgw_ref, mixed_ref,
                     range(c0, c0 + HG_CHUNKS_PER_GROUP), masks)
    for hd in range(HG_HEADS):
        state_ref[hd] = states[hd]

    y = _dot(mixed_ref[...], wout_sc[...]) + bout_ref[...]
    x1 = x + mod[2:3] * y
    x1_ref[0] = x1
    h2 = _rms(x1) * (gffn_ref[...] * (1.0 + mod[4:5])) + mod[3:4]
    hb2 = h2buf.at[slot]
    for s in range(SUBLANES):
        hb2[pl.ds(s, TS, stride=SUBLANES), :] = h2[:, s * LANES:(s + 1) * LANES]

    h_hi = h2.astype(BF16)
    h_lo = (h2 - h_hi.astype(F32)).astype(BF16)
    wr = wr_ref[...]
    t1 = _dot_nt(wr, h_hi)
    t2 = _dot_nt(wr[0:N_EXPERTS], h_lo)
    logits = t1[0:N_EXPERTS] + t1[N_EXPERTS:2 * N_EXPERTS] + t2 + br_ref[:, 0:1]

    eidx = lax.broadcasted_iota(jnp.int32, (N_EXPERTS, TS), 0)
    l = logits
    vals, ohs, ids = [], [], []
    for _ in range(TOP_K):
        mx = jnp.max(l, axis=0, keepdims=True)
        ik = jnp.min(jnp.where(l == mx, eidx, N_EXPERTS), axis=0, keepdims=True)
        oh = eidx == ik
        vals.append(mx)
        ids.append(ik)
        ohs.append(oh)
        l = jnp.where(oh, NEG_INF, l)
    es = [jnp.exp(v - vals[0]) for v in vals]
    den = es[0] + es[1] + es[2] + es[3]
    run = run_ref[...]
    c0 = run[:, 0:1]
    base = c0
    utri = utri_ref[...]
    ranks = []
    for kq in range(TOP_K):
        ohf = jnp.where(ohs[kq], 1.0, 0.0)
        pref = _dot(ohf.astype(BF16), utri)
        ranks.append(base + pref)
        base = base + jnp.sum(ohf, axis=1, keepdims=True)
    run_new = jnp.broadcast_to(base, run.shape)
    run_ref[...] = run_new

    inv_tm = 1.0 / TM
    prev_ord = jnp.floor((c0 - 1.0) * inv_tm)
    new_e = jnp.floor((base - 1.0) * inv_tm) - prev_ord
    excl = _dot(etri_ref[...], jnp.broadcast_to(new_e, run.shape).astype(BF16))[:, 0:1]
    nf = nfs_ref[0:1, 0:1]
    newbase = nf + excl
    cur = curs_ref[:, 0:1]
    poss = []
    for kq in range(TOP_K):
        r = ranks[kq]
        ordv = jnp.floor(r * inv_tm)
        phys = jnp.where(ordv == prev_ord, cur, newbase + (ordv - prev_ord - 1.0))
        slot_of = phys * TM + (r - ordv * TM)
        poss.append(jnp.sum(jnp.where(ohs[kq], slot_of, 0.0), axis=0, keepdims=True))
    cur_new = jnp.where(new_e > 0.0, newbase + new_e - 1.0, cur)
    curs_ref[...] = jnp.broadcast_to(cur_new, curs_ref.shape)
    nf_new = nf + jnp.sum(new_e, axis=0, keepdims=True)
    nfs_ref[...] = jnp.broadcast_to(nf_new, nfs_ref.shape)
    tile_lane = lax.broadcasted_iota(jnp.int32, (N_EXPERTS, OWN_LANES), 1).astype(F32)
    started = (tile_lane >= newbase) & (tile_lane < newbase + new_e)
    cand = jnp.max(jnp.where(started, eidx[:, 0:1].astype(F32), -1.0), axis=0, keepdims=True)
    owner = jnp.where(cand >= 0.0, cand, owns_ref[...])
    owns_ref[...] = owner
    own_ref[...] = owner.astype(jnp.int32)
    nt_ref[...] = nfs_ref[...].astype(jnp.int32)

    pos = jnp.concatenate(poss, axis=0).astype(jnp.int32)
    pos_ref[...] = pos
    posv[slot] = pos
    pos_copy(slot).start()
    wts_ref[...] = jnp.concatenate([e / den for e in es], axis=0)

    @pl.when(step == n_steps - 1)
    def _():
        pos_copy(slot).wait()
        rows_issue(slot)
        rows_wait(1 - slot)
        rows_wait(slot)
        fill = jnp.where(base > 0.0, base - jnp.floor((base - 1.0) * inv_tm) * TM, float(TM))
        lane = lax.broadcasted_iota(jnp.int32, (N_EXPERTS, LANES), 1)
        metav[...] = jnp.where(lane == 0, fill, jnp.where(lane == 1, cur_new, nf_new)).astype(jnp.int32)
        meta_copy = pltpu.make_async_copy(metav, meta_smem, psem.at[slot])
        meta_copy.start()
        zeros = h2buf.at[slot]
        zeros[...] = jnp.zeros(zeros.shape, F32)
        meta_copy.wait()
        sites = []
        for e in range(N_EXPERTS):
            present = meta_smem[e, 0]
            row = meta_smem[e, 1] * TM + present
            rem = TM - present
            n = TM // 2
            while n >= 1:
                cond = (rem & n) != 0
                sites.append((cond, row, n))
                row = row + jnp.where(cond, n, 0)
                n //= 2
        for u in range(N_EXPERTS):
            tile = meta_smem[0, 2] + u
            sites.append((tile < n_tiles, tile * TM, TM))

        def fill_copy(row, n):
            return pltpu.make_async_copy(
                zeros.at[pl.ds(0, n * ROW)], xs_hbm.at[pl.ds(pl.multiple_of(row * ROW, ROW), n * ROW)],
                rsem.at[slot])

        for cond, row, n in sites:
            pl.when(cond)(lambda row=row, n=n: fill_copy(row, n).start())
        for cond, row, n in sites:
            pl.when(cond)(lambda row=row, n=n: fill_copy(row, n).wait())


def _mix_call(sinks, x, mod, gmix, win, bin_, bias, lb, hgw, wout, bout, gffn, wr, br, ltri, utri, etri):
    B, S, D = x.shape
    N = B * S
    nj = S // TS
    n_tiles = N * TOP_K // TM + N_EXPERTS
    assert n_tiles <= OWN_LANES and TOP_K * TS <= DUMP_TILES * TM and TS >= TM
    const2 = lambda b, j: (0, 0)
    tok = lambda b, j: (0, b * nj + j)
    in_specs = [
        pl.BlockSpec(memory_space=pltpu.SMEM),
        pl.BlockSpec((1, TS, D), lambda b, j: (b, j, 0)),
        pl.BlockSpec((1, 8, D), lambda b, j: (b, 0, 0)),
        pl.BlockSpec((1, D), const2),
        pl.BlockSpec((D, PROJ_COLS), const2),
        pl.BlockSpec((1, PROJ_COLS), const2),
        pl.BlockSpec((ATT_HEADS, BLOCK, 2 * BLOCK), lambda b, j: (0, 0, 0)),
        pl.BlockSpec((1, HG_W), const2),
        pl.BlockSpec((1, HG_D), const2),
        pl.BlockSpec((D, D), const2),
        pl.BlockSpec((1, D), const2),
        pl.BlockSpec((1, D), const2),
        pl.BlockSpec((2 * N_EXPERTS, D), const2),
        pl.BlockSpec((N_EXPERTS, LANES), const2),
        pl.BlockSpec((TS, TS), const2),
        pl.BlockSpec((TS, TS), const2),
        pl.BlockSpec((N_EXPERTS, N_EXPERTS), const2),
    ]
    out_specs = [
        pl.BlockSpec((1, TS, D), lambda b, j: (b, j, 0)),
        pl.BlockSpec((TOP_K, TS), tok),
        pl.BlockSpec((TOP_K, TS), tok),
        pl.BlockSpec((1, OWN_LANES), const2),
        pl.BlockSpec((1, LANES), const2),
        pl.BlockSpec(memory_space=pl.ANY),
    ]
    out_shape = [
        jax.ShapeDtypeStruct((B, S, D), F32),
        jax.ShapeDtypeStruct((TOP_K, N), jnp.int32),
        jax.ShapeDtypeStruct((TOP_K, N), F32),
        jax.ShapeDtypeStruct((1, OWN_LANES), jnp.int32),
        jax.ShapeDtypeStruct((1, LANES), jnp.int32),
        jax.ShapeDtypeStruct(((n_tiles + DUMP_TILES) * TM * ROW, LANES), F32),
    ]
    scratch = [
        pltpu.VMEM((TS, PROJ_COLS), F32),
        pltpu.VMEM((TS, D), BF16),
        pltpu.VMEM((BLOCK, 512), F32),
        pltpu.VMEM((HG_HEADS, HG_D, HG_D), F32),
        pltpu.VMEM((N_EXPERTS, LANES), F32),
        pltpu.VMEM((D, PROJ_COLS), BF16),
        pltpu.VMEM((D, D), BF16),
        pltpu.VMEM((N_EXPERTS, LANES), F32),
        pltpu.VMEM((1, OWN_LANES), F32),
        pltpu.VMEM((1, LANES), F32),
        pltpu.VMEM((2, TS * ROW, LANES), F32),
        pltpu.VMEM((2, TOP_K, TS), jnp.int32),
        pltpu.SMEM((2, TOP_K, TS), jnp.int32),
        pltpu.VMEM((N_EXPERTS, LANES), jnp.int32),
        pltpu.SMEM((N_EXPERTS, LANES), jnp.int32),
        pltpu.SemaphoreType.DMA((2,)),
        pltpu.SemaphoreType.DMA((2,)),
    ]
    return pl.pallas_call(
        _mix_kernel,
        grid=(B, nj),
        in_specs=in_specs,
        out_specs=out_specs,
        out_shape=out_shape,
        scratch_shapes=scratch,
        compiler_params=pltpu.CompilerParams(
            dimension_semantics=("arbitrary", "arbitrary"), vmem_limit_bytes=VMEM_LIMIT),
        name="mix_router",
    )(sinks, x, mod, gmix, win, bin_, bias, lb, hgw, wout, bout, gffn, wr, br, ltri, utri, etri)


TM = 512
DUMP_TILES = 4
OWN_LANES = 384
FF_CHUNKS = 4
TD = 256
ROW = SUBLANES


def _row_copy(src, src_row, dst, dst_row, sem):
    return pltpu.make_async_copy(src.at[pl.ds(pl.multiple_of(src_row * ROW, ROW), ROW)],
                                 dst.at[pl.ds(pl.multiple_of(dst_row * ROW, ROW), ROW)], sem)


def _col_from_row(w_row, n):
    rr = lax.broadcasted_iota(jnp.int32, (n, n), 0)
    cc = lax.broadcasted_iota(jnp.int32, (n, n), 1)
    return jnp.sum(jnp.where(rr == cc, w_row, 0.0), axis=1, keepdims=True)


def _rows_to_matrix(ref, base, n):
    return jnp.concatenate([ref[pl.ds(base + s, n, stride=ROW), :] for s in range(ROW)], axis=1)


def _moe_kernel(order_ref, te_ref, nt_ref, nxt_ref, par_ref, xs_ref, wgu_hbm, bgu_ref, wd_hbm, bd_ref, perm_ref,
                ys_ref, wgu_st, wd_st, wgu_bf, wd_bf, wsem):
    del order_ref
    j = pl.program_id(0)
    nt = nt_ref[0]

    def weight_copies(e):
        s = par_ref[e]
        return (pltpu.make_async_copy(wgu_hbm.at[e], wgu_st.at[s], wsem.at[0, s]),
                pltpu.make_async_copy(wd_hbm.at[e], wd_st.at[s], wsem.at[1, s]))

    @pl.when(j == 0)
    def _():
        for cp in weight_copies(te_ref[0]):
            cp.start()

    @pl.when(j < nt)
    def _():
        e = te_ref[j]
        first = (j == 0) | (te_ref[jnp.maximum(j - 1, 0)] != e)

        @pl.when(first)
        def _():
            for cp in weight_copies(e):
                cp.wait()
            s = par_ref[e]
            perm = perm_ref[...]
            for g in range(2 * D_FF // 256):
                w = wgu_st[s, :, g * 256:(g + 1) * 256].astype(BF16)
                wgu_bf[:, g * 256:(g + 1) * 256] = _dot(w, perm).astype(BF16)
            wd_bf[...] = wd_st[s].astype(BF16)
            nxt = nxt_ref[e]

            @pl.when(nxt >= 0)
            def _():
                for cp in weight_copies(nxt):
                    cp.start()

        h = _rows_to_matrix(xs_ref, 0, TM).astype(BF16)
        gw = 2 * D_FF // FF_CHUNKS
        hw = D_FF // FF_CHUNKS

        def up_proj(c):
            return _dot(h, wgu_bf[:, c * gw:(c + 1) * gw]) + bgu_ref[0, :, c * gw:(c + 1) * gw]

        def activate(gu):
            parts = []
            for g in range(gw // 256):
                gate = jnp.minimum(gu[:, g * 256:g * 256 + 128], SWIGLU_LIMIT)
                up = jnp.clip(gu[:, g * 256 + 128:(g + 1) * 256], -SWIGLU_LIMIT, SWIGLU_LIMIT)
                parts.append(((up + 1.0) * gate * _sigmoid(SWIGLU_ALPHA * gate)).astype(BF16))
            return jnp.concatenate(parts, axis=1)

        y = bd_ref[0]
        gu_next = up_proj(0)
        for c in range(FF_CHUNKS):
            gu = gu_next
            if c + 1 < FF_CHUNKS:
                gu_next = up_proj(c + 1)
            y = y + _dot(activate(gu), wd_bf[c * hw:(c + 1) * hw, :])
        for s in range(ROW):
            ys_ref[pl.ds(s, TM, stride=ROW), :] = y[:, s * LANES:(s + 1) * LANES]

    @pl.when(j >= nt)
    def _():
        ys_ref[...] = jnp.zeros_like(ys_ref)


def _moe_call(order, te, nt, nxt, par, xs, wgu, bgu, wd, bd, perm):
    n_tiles = xs.shape[0] // (TM * ROW) - DUMP_TILES
    tile = lambda j, order, te, nt, nxt, par: (order[jnp.minimum(j, nt[0] - 1)], 0)
    exp = lambda j, order, te, nt, nxt, par: (te[jnp.minimum(j, nt[0] - 1)], 0, 0)
    grid_spec = pltpu.PrefetchScalarGridSpec(
        num_scalar_prefetch=5,
        grid=(n_tiles,),
        in_specs=[
            pl.BlockSpec((TM * ROW, LANES), tile),
            pl.BlockSpec(memory_space=pl.ANY),
            pl.BlockSpec((1, 1, 2 * D_FF), exp),
            pl.BlockSpec(memory_space=pl.ANY),
            pl.BlockSpec((1, 1, D_MODEL), exp),
            pl.BlockSpec((256, 256), lambda j, order, te, nt, nxt, par: (0, 0)),
        ],
        out_specs=pl.BlockSpec((TM * ROW, LANES), lambda j, order, te, nt, nxt, par: (order[j], 0)),
        scratch_shapes=[
            pltpu.VMEM((2, D_MODEL, 2 * D_FF), F32),
            pltpu.VMEM((2, D_FF, D_MODEL), F32),
            pltpu.VMEM((D_MODEL, 2 * D_FF), BF16),
            pltpu.VMEM((D_FF, D_MODEL), BF16),
            pltpu.SemaphoreType.DMA((2, 2)),
        ],
    )
    return pl.pallas_call(
        _moe_kernel,
        grid_spec=grid_spec,
        out_shape=jax.ShapeDtypeStruct((n_tiles * TM * ROW, LANES), F32),
        compiler_params=pltpu.CompilerParams(
            dimension_semantics=("arbitrary",), vmem_limit_bytes=VMEM_LIMIT),
        name="moe_experts",
    )(order, te, nt, nxt, par, xs, wgu, bgu, wd, bd, perm)


def _final_kernel(pa_ref, pb_ref, pc_ref, wts_ref, x1_ref, mod_ref, gfin_ref, ys_hbm, out_ref,
                  gbuf_a, gbuf_b, sem):
    i = pl.program_id(0)
    n = pl.num_programs(0)

    def issue(pref, buf, s):
        for q in range(TOP_K * TD):
            _row_copy(ys_hbm, pref[0, 0, q], buf, q, sem.at[s]).start(priority=q % 2)

    def wait(buf, s):
        pltpu.make_async_copy(ys_hbm.at[pl.ds(0, TOP_K * TD * ROW)], buf, sem.at[s]).wait()

    def combine(buf, half):
        acc = jnp.zeros((TD, D_MODEL), F32)
        for kq in range(TOP_K):
            v = _rows_to_matrix(buf, kq * TD * ROW, TD)
            acc = acc + _col_from_row(wts_ref[kq:kq + 1, half * TD:(half + 1) * TD], TD) * v
        xf = x1_ref[half * TD:(half + 1) * TD, :] + mod_ref[0][5:6] * acc
        out_ref[half * TD:(half + 1) * TD, :] = _rms(xf) * gfin_ref[...]

    pl.when(i == 0)(lambda: issue(pa_ref, gbuf_a, 0))
    wait(gbuf_a, 0)
    issue(pb_ref, gbuf_b, 1)
    combine(gbuf_a, 0)
    wait(gbuf_b, 1)
    issue(pc_ref, gbuf_a, 0)
    combine(gbuf_b, 1)
    pl.when(i == n - 1)(lambda: wait(gbuf_a, 0))


def _final_call(pos_t, wts, x1, mod, gfin, ys, S):
    N, D = x1.shape
    nstep = N // (2 * TD)
    per_b = S // (2 * TD)
    pos_spec = lambda f: pl.BlockSpec((1, 1, TOP_K * TD), lambda i: (f(i), 0, 0), memory_space=pltpu.SMEM)
    return pl.pallas_call(
        _final_kernel,
        grid=(nstep,),
        in_specs=[
            pos_spec(lambda i: 2 * i),
            pos_spec(lambda i: 2 * i + 1),
            pos_spec(lambda i: jnp.minimum(2 * i + 2, 2 * nstep - 2)),
            pl.BlockSpec((TOP_K, 2 * TD), lambda i: (0, i)),
            pl.BlockSpec((2 * TD, D), lambda i: (i, 0)),
            pl.BlockSpec((1, 8, D), lambda i: (i // per_b, 0, 0)),
            pl.BlockSpec((1, D), lambda i: (0, 0)),
            pl.BlockSpec(memory_space=pl.ANY),
        ],
        out_specs=pl.BlockSpec((2 * TD, D), lambda i: (i, 0)),
        out_shape=jax.ShapeDtypeStruct((N, D), F32),
        scratch_shapes=[
            pltpu.VMEM((TOP_K * TD * ROW, LANES), F32),
            pltpu.VMEM((TOP_K * TD * ROW, LANES), F32),
            pltpu.SemaphoreType.DMA((2,)),
        ],
        compiler_params=pltpu.CompilerParams(
            dimension_semantics=("arbitrary",), vmem_limit_bytes=VMEM_LIMIT),
        name="combine_norm",
    )(pos_t, pos_t, pos_t, wts, x1, mod, gfin, ys)


def _route_plan(pos, own, ntiles, N):
    n_tiles = N * TOP_K // TM + N_EXPERTS
    nt = ntiles[0, 0:1]
    owner = own[0, :n_tiles]
    tiles = jnp.arange(n_tiles, dtype=jnp.int32)
    experts = jnp.arange(N_EXPERTS, dtype=jnp.int32)
    used = tiles < nt[0]
    key = jnp.where(used, owner, N_EXPERTS) * OWN_LANES + tiles
    place = jnp.sum(key[None, :] < key[:, None], axis=1)
    at = place[None, :] == tiles[:, None]
    order = jnp.sum(jnp.where(at, tiles[None, :], 0), axis=1).astype(jnp.int32)
    te = jnp.minimum(jnp.sum(jnp.where(at, jnp.where(used, owner, N_EXPERTS)[None, :], 0), axis=1),
                     N_EXPERTS - 1).astype(jnp.int32)
    ntile = jnp.sum((owner[None, :] == experts[:, None]) & used[None, :], axis=1)
    nonempty = ntile > 0
    pos_t = jnp.transpose(pos.reshape(TOP_K, N // TD, TD), (1, 0, 2)).reshape(N // TD, 1, TOP_K * TD)
    par = ((jnp.cumsum(nonempty) - 1) % 2).astype(jnp.int32)
    later = nonempty[None, :] & (experts[None, :] > experts[:, None])
    nxt = jnp.min(jnp.where(later, experts[None, :], N_EXPERTS), axis=1)
    nxt = jnp.where(nxt == N_EXPERTS, -1, nxt).astype(jnp.int32)
    return order, te, nt, nxt, par, pos_t


def _tri_constants():
    r = np.arange(TS)[:, None]
    c = np.arange(TS)[None, :]
    ltri = ((r // CHUNK) == (c // CHUNK)) & (c <= r)
    utri = r < c
    return jnp.asarray(ltri, BF16), jnp.asarray(utri, BF16)


def kernel(x, c, w_ada, b_ada, g_mix, w_in, b_in, attn_sinks, rel_bias, hg_lb, hg_norm_w, w_out, b_out, g_ffn, w_router, b_router, w_gate_up, b_gate_up, w_down, b_down, g_final):
    B, S, D = x.shape
    N = B * S
    mod6 = _prep_call(c, w_ada[0], b_ada)
    mod = jnp.pad(jnp.transpose(mod6, (1, 0, 2)), ((0, 0), (0, 2), (0, 0)))
    lb, bias = _tables_call(rel_bias, hg_lb, _bucket_table())

    wi, bi_ = w_in[0], b_in[0]
    def cols(a):
        aq, ak, av, rest = a[..., :512], a[..., 512:640], a[..., 640:768], a[..., 768:]
        k0, k1, v0, v1 = ak[..., :64], ak[..., 64:], av[..., :64], av[..., 64:]
        return jnp.concatenate([aq, k0, k0, k1, k1, v0, v0, v1, v1, rest], axis=-1)
    win = cols(wi).astype(BF16)
    bin_ = cols(bi_)[None, :]
    wrt = jnp.transpose(w_router[0])
    wr_hi = wrt.astype(BF16)
    wr_lo = (wrt - wr_hi.astype(F32)).astype(BF16)
    wr = jnp.concatenate([wr_hi, wr_lo], axis=0)
    br = jnp.broadcast_to(b_router[0][:, None], (N_EXPERTS, LANES))
    ltri, utri = _tri_constants()
    etri = jnp.asarray(np.tril(np.ones((N_EXPERTS, N_EXPERTS), np.float32), -1), BF16)

    x1, pos, wts, own, ntiles, xs = _mix_call(
        attn_sinks[0], x, mod, g_mix, win, bin_, bias, lb, hg_norm_w, w_out[0].astype(BF16),
        b_out, g_ffn, wr, br, ltri, utri, etri)

    order, te, nt, nxt, par, pos_t = _route_plan(pos, own, ntiles, N)
    ii = np.arange(128)
    perm_np = np.zeros((256, 256), np.float32)
    perm_np[2 * ii, ii] = 1.0
    perm_np[2 * ii + 1, 128 + ii] = 1.0
    perm = jnp.asarray(perm_np, BF16)
    bgu = b_gate_up[0].reshape(N_EXPERTS, D_FF // LANES, LANES, 2)
    bgu = jnp.transpose(bgu, (0, 1, 3, 2)).reshape(N_EXPERTS, 1, 2 * D_FF)
    ys = _moe_call(order, te, nt, nxt, par, xs, w_gate_up[0], bgu, w_down[0], b_down[0][:, None, :], perm)
    out = _final_call(pos_t, wts, x1.reshape(N, D), mod, g_final[None, :], ys, S)
    return out.reshape(B, S, D)
```

```python
import functools
import math

import numpy as np
import jax
import jax.numpy as jnp
from jax import lax
from jax.experimental import pallas as pl
from jax.experimental.pallas import tpu as pltpu

D_MODEL = 1024
ATT_HEAD_DIM = 64
ATT_HEADS = 8
ATT_KV_HEADS = 2
ATT_SCALE = ATT_HEAD_DIM ** -0.5
BLOCK = 128
N_BUCKETS = 32
MAX_DISTANCE = 128
HG_HEADS = 4
HG_D = 128
CHUNK = 64
N_EXPERTS = 32
TOP_K = 4
D_FF = 1024
SWIGLU_LIMIT = 7.0
SWIGLU_ALPHA = 1.702
EPS = 1e-5

ATT_Q = ATT_HEADS * ATT_HEAD_DIM
HG_W = HG_HEADS * HG_D
COL_Q = 0
COL_KV = ATT_Q
COL_HQ = COL_KV + 4 * 128
COL_HF = COL_HQ + HG_W
COL_HI = COL_HF + HG_W
COL_HG = COL_HI + HG_W
PROJ_COLS = COL_HG + HG_W

LANES = 128
SUBLANES = 8
TS = 512
VMEM_LIMIT = 56 * 1024 * 1024
HG_SAFE_LEVELS = (32, 16, 8)
HG_DIAG = 8
HG_CHUNKS_PER_GROUP = 2
HG_EXP_CAP = 80.0

F32 = jnp.float32
BF16 = jnp.bfloat16
NEG_INF = float("-inf")


def _dot(a, b):
    return jnp.dot(a, b, preferred_element_type=F32)


def _dot_nt(a, b):
    return lax.dot_general(a, b, (((1,), (1,)), ((), ())), preferred_element_type=F32)


def _dot_tn(a, b):
    return lax.dot_general(a, b, (((0,), (0,)), ((), ())), preferred_element_type=F32)


def _sigmoid(x):
    return 1.0 / (1.0 + jnp.exp(-x))


def _prep_kernel(c_ref, w_ref, b_ref, o_ref):
    c = c_ref[...]
    cond = c * _sigmoid(c)
    o_ref[0] = jnp.dot(cond, w_ref[...], precision=lax.Precision.HIGHEST,
                       preferred_element_type=F32) + b_ref[...]


def _prep_call(c, w_ada, b_ada):
    B = c.shape[0]
    return pl.pallas_call(
        _prep_kernel,
        grid=(6,),
        in_specs=[
            pl.BlockSpec((B, D_MODEL), lambda j: (0, 0)),
            pl.BlockSpec((D_MODEL, D_MODEL), lambda j: (0, j)),
            pl.BlockSpec((1, D_MODEL), lambda j: (0, j)),
        ],
        out_specs=pl.BlockSpec((1, B, D_MODEL), lambda j: (j, 0, 0)),
        out_shape=jax.ShapeDtypeStruct((6, B, D_MODEL), F32),
        compiler_params=pltpu.CompilerParams(dimension_semantics=("arbitrary",)),
        name="adaln_mod",
    )(c, w_ada, b_ada)


def _tables_kernel(rb_ref, lbp_ref, bucket_ref, lb_ref, bias_ref):
    p = lbp_ref[...]
    e = jnp.exp(p - jnp.max(p, axis=0, keepdims=True))
    sm = e / jnp.sum(e, axis=0, keepdims=True)
    lb_ref[...] = sm[0:1]
    bucket = bucket_ref[...]
    for h in range(ATT_HEADS):
        acc = jnp.full(bucket.shape, NEG_INF, F32)
        for bk in range(N_BUCKETS):
            acc = jnp.where(bucket == bk, rb_ref[bk, h], acc)
        bias_ref[h] = acc


def _tables_call(rel_bias, hg_lb, bucket):
    return pl.pallas_call(
        _tables_kernel,
        in_specs=[
            pl.BlockSpec(memory_space=pltpu.SMEM),
            pl.BlockSpec(memory_space=pltpu.VMEM),
            pl.BlockSpec(memory_space=pltpu.VMEM),
        ],
        out_specs=[pl.BlockSpec(memory_space=pltpu.VMEM), pl.BlockSpec(memory_space=pltpu.VMEM)],
        out_shape=[jax.ShapeDtypeStruct((1, HG_W), F32),
                   jax.ShapeDtypeStruct((ATT_HEADS, BLOCK, 2 * BLOCK), F32)],
        name="tables",
    )(rel_bias, hg_lb, bucket)


def _bucket_table():
    i = np.arange(BLOCK, dtype=np.int32)[:, None]
    m = np.arange(2 * BLOCK, dtype=np.int32)[None, :]
    dist = i + BLOCK - m
    n = np.maximum(dist, 0)
    max_exact = N_BUCKETS // 2
    nf = np.maximum(n, 1).astype(np.float32)
    large = max_exact + (np.log(nf / np.float32(max_exact)) / np.float32(math.log(MAX_DISTANCE / max_exact))
                         * np.float32(N_BUCKETS - max_exact)).astype(np.int32)
    large = np.minimum(large, N_BUCKETS - 1)
    bucket = np.where(n < max_exact, n, large)
    return jnp.asarray(np.where((dist >= 0) & (dist < BLOCK), bucket, -1), jnp.int32)


def _rms(x):
    return x * lax.rsqrt(jnp.mean(x * x, axis=-1, keepdims=True) + EPS)


def _attention_block(proj_ref, kvprev_ref, bias_ref, sink_ref, mixed_ref, blk, first_tile):
    r0 = blk * BLOCK
    cur = proj_ref[r0:r0 + BLOCK, COL_KV:COL_KV + 512]
    if blk == 0:
        prev = kvprev_ref[...]
    else:
        prev = proj_ref[r0 - BLOCK:r0, COL_KV:COL_KV + 512]
    band = jnp.concatenate([prev, cur], axis=0)
    lane = lax.broadcasted_iota(jnp.int32, (2 * BLOCK, LANES), 1)
    lo = lane < ATT_HEAD_DIM
    col = lax.broadcasted_iota(jnp.int32, (1, 2 * BLOCK), 1)
    if blk == 0:
        pen = jnp.where((col < BLOCK) & first_tile, NEG_INF, 0.0)
    kbs, vbs = [], []
    for g in range(ATT_KV_HEADS):
        kd = band[:, g * 128:(g + 1) * 128]
        vd = band[:, 256 + g * 128:256 + (g + 1) * 128]
        kbs.append(jnp.concatenate([jnp.where(lo, kd, 0.0), jnp.where(lo, 0.0, kd)], axis=0).astype(BF16))
        vbs.append(jnp.concatenate([jnp.where(lo, vd, 0.0), jnp.where(lo, 0.0, vd)], axis=0).astype(BF16))
    n_pairs = ATT_HEADS // 2
    ss = []
    for g in range(ATT_KV_HEADS):
        qg = jnp.concatenate(
            [(proj_ref[r0:r0 + BLOCK, pair * 128:(pair + 1) * 128] * ATT_SCALE).astype(BF16)
             for pair in (2 * g, 2 * g + 1)], axis=0)
        sg = _dot_nt(qg, kbs[g])
        ss += [sg[0:BLOCK], sg[BLOCK:2 * BLOCK]]
    ps, invs = [], []
    for h in range(ATT_HEADS):
        sh = ss[h // 2][:, (h % 2) * 256:(h % 2 + 1) * 256] + bias_ref[h]
        if blk == 0:
            sh = sh + pen
        sink = sink_ref[h]
        mx = jnp.maximum(jnp.max(sh, axis=-1, keepdims=True), sink)
        p = jnp.exp(sh - mx)
        den = jnp.sum(p, axis=-1, keepdims=True) + jnp.exp(sink - mx)
        ps.append(p.astype(BF16))
        invs.append(1.0 / den)
    os_ = []
    for g in range(ATT_KV_HEADS):
        pg = jnp.concatenate([jnp.concatenate(ps[2 * pair:2 * pair + 2], axis=1)
                              for pair in (2 * g, 2 * g + 1)], axis=0)
        og = _dot(pg, vbs[g])
        os_ += [og[0:BLOCK], og[BLOCK:2 * BLOCK]]
    lane_o = lax.broadcasted_iota(jnp.int32, (BLOCK, LANES), 1)
    for pair in range(n_pairs):
        o = os_[pair] * jnp.where(lane_o < ATT_HEAD_DIM, invs[2 * pair], invs[2 * pair + 1])
        mixed_ref[r0:r0 + BLOCK, pair * 128:(pair + 1) * 128] = o.astype(BF16)


def _hgrn_scaled_operands(b, q, k):
    pairs = []
    for m in HG_SAFE_LEVELS:
        qs, ks = [], []
        for bi in range(CHUNK // m):
            rows = slice(bi * m, (bi + 1) * m)
            if bi % 2 == 1:
                ref = b[bi * m - 1:bi * m]
                qs.append(q[rows] * jnp.exp(b[rows] - ref))
                ks.append(jnp.zeros((m, HG_D), F32))
            else:
                ref = b[(bi + 1) * m - 1:(bi + 1) * m]
                qs.append(jnp.zeros((m, HG_D), F32))
                ks.append(k[rows] * jnp.exp(ref - b[rows]))
        pairs.append((m, jnp.concatenate(qs, axis=0).astype(BF16), jnp.concatenate(ks, axis=0).astype(BF16)))
    qs, ks = [], []
    for bi in range(CHUNK // HG_DIAG):
        rows = slice(bi * HG_DIAG, (bi + 1) * HG_DIAG)
        c = b[rows] if bi == 0 else b[rows] - b[bi * HG_DIAG - 1:bi * HG_DIAG]
        qs.append(q[rows] * jnp.exp(c))
        ks.append(k[rows] * jnp.exp(jnp.minimum(-c, HG_EXP_CAP)))
    pairs.append((0, jnp.concatenate(qs, axis=0).astype(BF16), jnp.concatenate(ks, axis=0).astype(BF16)))
    return pairs


def _hgrn_chunks(proj_ref, states, b_all, kk_all, qf_all, hgw_ref, mixed_ref, chunks, masks):
    units = [(ck, hd) for ck in chunks for hd in range(HG_HEADS)]
    rows = {u: slice(u[0] * CHUNK, (u[0] + 1) * CHUNK) for u in units}
    cols = {u: slice(u[1] * HG_D, (u[1] + 1) * HG_D) for u in units}
    bs = {u: b_all[rows[u], cols[u]] for u in units}
    qs = {u: qf_all[rows[u], cols[u]] for u in units}
    ks = {u: kk_all[rows[u], cols[u]] for u in units}
    vs = {u: proj_ref[rows[u], COL_HI + u[1] * HG_D:COL_HI + (u[1] + 1) * HG_D].astype(BF16) for u in units}
    operands = {u: _hgrn_scaled_operands(bs[u], qs[u], ks[u]) for u in units}
    qbs = {u: (qs[u] * jnp.exp(bs[u])).astype(BF16) for u in units}
    bends = {u: bs[u][CHUNK - 1:CHUNK] for u in units}
    kdecs = {u: (ks[u] * jnp.exp(bends[u] - bs[u])).astype(BF16) for u in units}
    prods = {u: [(m, _dot_nt(ql, kl)) for m, ql, kl in operands[u]] for u in units}
    kvs = {u: _dot_tn(vs[u], kdecs[u]) for u in units}
    intras = {}
    for u in units:
        sc = jnp.zeros((CHUNK, CHUNK), F32)
        for m, pr in prods[u]:
            sc = sc + jnp.where(masks[m], pr, 0.0)
        intras[u] = _dot(sc.astype(BF16), vs[u])
    for ck in chunks:
        inters = [_dot_nt(qbs[(ck, hd)], states[hd].astype(BF16)) for hd in range(HG_HEADS)]
        for hd in range(HG_HEADS):
            u = (ck, hd)
            states[hd] = states[hd] * jnp.exp(bends[u]) + kvs[u]
            gate = proj_ref[rows[u], COL_HG + hd * HG_D:COL_HG + (hd + 1) * HG_D]
            on = _rms(intras[u] + inters[hd]) * hgw_ref[...] * (gate * _sigmoid(gate))
            mixed_ref[rows[u], ATT_Q + hd * HG_D:ATT_Q + (hd + 1) * HG_D] = on.astype(BF16)


def _hgrn_masks():
    r = lax.broadcasted_iota(jnp.int32, (CHUNK, CHUNK), 0)
    c = lax.broadcasted_iota(jnp.int32, (CHUNK, CHUNK), 1)
    masks = {}
    for m in HG_SAFE_LEVELS:
        masks[m] = ((r // (2 * m)) == (c // (2 * m))) & (((r // m) % 2) == 1) & (((c // m) % 2) == 0)
    masks[0] = ((r // HG_DIAG) == (c // HG_DIAG)) & (c <= r)
    return masks


def _mix_kernel(sink_ref, x_ref, mod_ref, gmix_ref, win_ref, bin_ref, bias_ref, lb_ref, hgw_ref,
                wout_ref, bout_ref, gffn_ref, wr_ref, br_ref, ltri_ref, utri_ref, etri_ref,
                x1_ref, pos_ref, wts_ref, own_ref, nt_ref, xs_hbm,
                proj_ref, mixed_ref, kvprev_ref, state_ref, run_ref, win_sc, wout_sc,
                curs_ref, owns_ref, nfs_ref, h2buf, posv, pos_smem, metav, meta_smem, rsem, psem):
    bi = pl.program_id(0)
    j = pl.program_id(1)
    step = bi * pl.num_programs(1) + j
    n_steps = pl.num_programs(0) * pl.num_programs(1)
    slot = step % 2
    n_tiles = xs_hbm.shape[0] // (TM * ROW) - DUMP_TILES

    def rows_wait(sl):
        for _ in range(TOP_K):
            pltpu.make_async_copy(h2buf.at[sl], xs_hbm.at[pl.ds(0, TS * ROW)], rsem.at[sl]).wait()

    def rows_issue(sl):
        for q in range(TOP_K * TS):
            _row_copy(h2buf.at[sl], q % TS, xs_hbm, pos_smem[sl, q // TS, q % TS],
                      rsem.at[sl]).start(priority=q % 2)

    def pos_copy(sl):
        return pltpu.make_async_copy(posv.at[sl], pos_smem.at[sl], psem.at[sl])

    @pl.when(j == 0)
    def _():
        kvprev_ref[...] = jnp.zeros_like(kvprev_ref)
        state_ref[...] = jnp.zeros_like(state_ref)

    @pl.when(step == 0)
    def _():
        run_ref[...] = jnp.zeros_like(run_ref)
        curs_ref[...] = jnp.zeros_like(curs_ref)
        owns_ref[...] = jnp.full(owns_ref.shape, -1.0, F32)
        nfs_ref[...] = jnp.zeros_like(nfs_ref)
        win_sc[...] = win_ref[...]
        wout_sc[...] = wout_ref[...]
        h2buf[1] = jnp.zeros(h2buf.shape[1:], F32)
        spare = n_tiles * TM + lax.broadcasted_iota(jnp.int32, (TOP_K, TS), 0) * TS \
            + lax.broadcasted_iota(jnp.int32, (TOP_K, TS), 1)
        posv[1] = spare
        pos_copy(1).start()

    pl.when(step >= 1)(lambda: rows_wait(slot))

    x = x_ref[0]
    mod = mod_ref[0]
    h = _rms(x) * (gmix_ref[...] * (1.0 + mod[1:2])) + mod[0:1]
    hb = h.astype(BF16)

    def project(c0, c1):
        proj_ref[:, c0:c1] = _dot(hb, win_sc[:, c0:c1]) + bin_ref[:, c0:c1]

    project(0, COL_HQ)
    pos_copy(1 - slot).wait()
    rows_issue(1 - slot)
    n_blk = TS // BLOCK
    hg_cols = (PROJ_COLS - COL_HQ) // n_blk
    for blk in range(n_blk):
        _attention_block(proj_ref, kvprev_ref, bias_ref, sink_ref, mixed_ref, blk, j == 0)
        project(COL_HQ + blk * hg_cols, COL_HQ + (blk + 1) * hg_cols)
    kvprev_ref[...] = proj_ref[TS - BLOCK:TS, COL_KV:COL_KV + 512]

    lbv = lb_ref[...]
    fl = lbv + (1.0 - lbv) * _sigmoid(proj_ref[:, COL_HF:COL_HF + HG_W])
    g = jnp.log(fl)
    g_hi = g.astype(BF16)
    r1 = g - g_hi.astype(F32)
    g_mid = r1.astype(BF16)
    g_lo = (r1 - g_mid.astype(F32)).astype(BF16)
    ltri = ltri_ref[...]
    b_all = _dot(ltri, g_hi) + _dot(ltri, g_mid) + _dot(ltri, g_lo)

    masks = _hgrn_masks()
    qr = proj_ref[:, COL_HQ:COL_HQ + HG_W]
    qf_all = qr * _sigmoid(qr)
    kk_all = 1.0 - fl
    states = [state_ref[hd] for hd in range(HG_HEADS)]
    n_chunks = TS // CHUNK
    for c0 in range(0, n_chunks, HG_CHUNKS_PER_GROUP):
        _hgrn_chunks(proj_ref, states, b_all, kk_all, qf_all, hgw_ref, mixed_ref,
                     range(c0, c0 + HG_CHUNKS_PER_GROUP), masks)
    for hd in range(HG_HEADS):
        state_ref[hd] = states[hd]

    y = _dot(mixed_ref[...], wout_sc[...]) + bout_ref[...]
    x1 = x + mod[2:3] * y
    x1_ref[0] = x1
    h2 = _rms(x1) * (gffn_ref[...] * (1.0 + mod[4:5])) + mod[3:4]
    hb2 = h2buf.at[slot]
    for s in range(SUBLANES):
        hb2[pl.ds(s, TS, stride=SUBLANES), :] = h2[:, s * LANES:(s + 1) * LANES]

    h_hi = h2.astype(BF16)
    h_lo = (h2 - h_hi.astype(F32)).astype(BF16)
    wr = wr_ref[...]
    t1 = _dot_nt(wr, h_hi)
    t2 = _dot_nt(wr[0:N_EXPERTS], h_lo)
    logits = t1[0:N_EXPERTS] + t1[N_EXPERTS:2 * N_EXPERTS] + t2 + br_ref[:, 0:1]

    eidx = lax.broadcasted_iota(jnp.int32, (N_EXPERTS, TS), 0)
    l = logits
    vals, ohs, ids = [], [], []
    for _ in range(TOP_K):
        mx = jnp.max(l, axis=0, keepdims=True)
        ik = jnp.min(jnp.where(l == mx, eidx, N_EXPERTS), axis=0, keepdims=True)
        oh = eidx == ik
        vals.append(mx)
        ids.append(ik)
        ohs.append(oh)
        l = jnp.where(oh, NEG_INF, l)
    es = [jnp.exp(v - vals[0]) for v in vals]
    den = es[0] + es[1] + es[2] + es[3]
    run = run_ref[...]
    c0 = run[:, 0:1]
    base = c0
    utri = utri_ref[...]
    ranks = []
    for kq in range(TOP_K):
        ohf = jnp.where(ohs[kq], 1.0, 0.0)
        pref = _dot(ohf.astype(BF16), utri)
        ranks.append(base + pref)
        base = base + jnp.sum(ohf, axis=1, keepdims=True)
    run_new = jnp.broadcast_to(base, run.shape)
    run_ref[...] = run_new

    inv_tm = 1.0 / TM
    prev_ord = jnp.floor((c0 - 1.0) * inv_tm)
    new_e = jnp.floor((base - 1.0) * inv_tm) - prev_ord
    excl = _dot(etri_ref[...], jnp.broadcast_to(new_e, run.shape).astype(BF16))[:, 0:1]
    nf = nfs_ref[0:1, 0:1]
    newbase = nf + excl
    cur = curs_ref[:, 0:1]
    poss = []
    for kq in range(TOP_K):
        r = ranks[kq]
        ordv = jnp.floor(r * inv_tm)
        phys = jnp.where(ordv == prev_ord, cur, newbase + (ordv - prev_ord - 1.0))
        slot_of = phys * TM + (r - ordv * TM)
        poss.append(jnp.sum(jnp.where(ohs[kq], slot_of, 0.0), axis=0, keepdims=True))
    cur_new = jnp.where(new_e > 0.0, newbase + new_e - 1.0, cur)
    curs_ref[...] = jnp.broadcast_to(cur_new, curs_ref.shape)
    nf_new = nf + jnp.sum(new_e, axis=0, keepdims=True)
    nfs_ref[...] = jnp.broadcast_to(nf_new, nfs_ref.shape)
    tile_lane = lax.broadcasted_iota(jnp.int32, (N_EXPERTS, OWN_LANES), 1).astype(F32)
    started = (tile_lane >= newbase) & (tile_lane < newbase + new_e)
    cand = jnp.max(jnp.where(started, eidx[:, 0:1].astype(F32), -1.0), axis=0, keepdims=True)
    owner = jnp.where(cand >= 0.0, cand, owns_ref[...])
    owns_ref[...] = owner
    own_ref[...] = owner.astype(jnp.int32)
    nt_ref[...] = nfs_ref[...].astype(jnp.int32)

    pos = jnp.concatenate(poss, axis=0).astype(jnp.int32)
    pos_ref[...] = pos
    posv[slot] = pos
    pos_copy(slot).start()
    wts_ref[...] = jnp.concatenate([e / den for e in es], axis=0)

    @pl.when(step == n_steps - 1)
    def _():
        pos_copy(slot).wait()
        rows_issue(slot)
        rows_wait(1 - slot)
        rows_wait(slot)
        fill = jnp.where(base > 0.0, base - jnp.floor((base - 1.0) * inv_tm) * TM, float(TM))
        lane = lax.broadcasted_iota(jnp.int32, (N_EXPERTS, LANES), 1)
        metav[...] = jnp.where(lane == 0, fill, jnp.where(lane == 1, cur_new, nf_new)).astype(jnp.int32)
        meta_copy = pltpu.make_async_copy(metav, meta_smem, psem.at[slot])
        meta_copy.start()
        zeros = h2buf.at[slot]
        zeros[...] = jnp.zeros(zeros.shape, F32)
        meta_copy.wait()
        sites = []
        for e in range(N_EXPERTS):
            present = meta_smem[e, 0]
            row = meta_smem[e, 1] * TM + present
            rem = TM - present
            n = TM // 2
            while n >= 1:
                cond = (rem & n) != 0
                sites.append((cond, row, n))
                row = row + jnp.where(cond, n, 0)
                n //= 2
        for u in range(N_EXPERTS):
            tile = meta_smem[0, 2] + u
            sites.append((tile < n_tiles, tile * TM, TM))

        def fill_copy(row, n):
            return pltpu.make_async_copy(
                zeros.at[pl.ds(0, n * ROW)], xs_hbm.at[pl.ds(pl.multiple_of(row * ROW, ROW), n * ROW)],
                rsem.at[slot])

        for cond, row, n in sites:
            pl.when(cond)(lambda row=row, n=n: fill_copy(row, n).start())
        for cond, row, n in sites:
            pl.when(cond)(lambda row=row, n=n: fill_copy(row, n).wait())


def _mix_call(sinks, x, mod, gmix, win, bin_, bias, lb, hgw, wout, bout, gffn, wr, br, ltri, utri, etri):
    B, S, D = x.shape
    N = B * S
    nj = S // TS
    n_tiles = N * TOP_K // TM + N_EXPERTS
    assert n_tiles <= OWN_LANES and TOP_K * TS <= DUMP_TILES * TM and TS >= TM
    const2 = lambda b, j: (0, 0)
    tok = lambda b, j: (0, b * nj + j)
    in_specs = [
        pl.BlockSpec(memory_space=pltpu.SMEM),
        pl.BlockSpec((1, TS, D), lambda b, j: (b, j, 0)),
        pl.BlockSpec((1, 8, D), lambda b, j: (b, 0, 0)),
        pl.BlockSpec((1, D), const2),
        pl.BlockSpec((D, PROJ_COLS), const2),
        pl.BlockSpec((1, PROJ_COLS), const2),
        pl.BlockSpec((ATT_HEADS, BLOCK, 2 * BLOCK), lambda b, j: (0, 0, 0)),
        pl.BlockSpec((1, HG_W), const2),
        pl.BlockSpec((1, HG_D), const2),
        pl.BlockSpec((D, D), const2),
        pl.BlockSpec((1, D), const2),
        pl.BlockSpec((1, D), const2),
        pl.BlockSpec((2 * N_EXPERTS, D), const2),
        pl.BlockSpec((N_EXPERTS, LANES), const2),
        pl.BlockSpec((TS, TS), const2),
        pl.BlockSpec((TS, TS), const2),
        pl.BlockSpec((N_EXPERTS, N_EXPERTS), const2),
    ]
    out_specs = [
        pl.BlockSpec((1, TS, D), lambda b, j: (b, j, 0)),
        pl.BlockSpec((TOP_K, TS), tok),
        pl.BlockSpec((TOP_K, TS), tok),
        pl.BlockSpec((1, OWN_LANES), const2),
        pl.BlockSpec((1, LANES), const2),
        pl.BlockSpec(memory_space=pl.ANY),
    ]
    out_shape = [
        jax.ShapeDtypeStruct((B, S, D), F32),
        jax.ShapeDtypeStruct((TOP_K, N), jnp.int32),
        jax.ShapeDtypeStruct((TOP_K, N), F32),
        jax.ShapeDtypeStruct((1, OWN_LANES), jnp.int32),
        jax.ShapeDtypeStruct((1, LANES), jnp.int32),
        jax.ShapeDtypeStruct(((n_tiles + DUMP_TILES) * TM * ROW, LANES), F32),
    ]
    scratch = [
        pltpu.VMEM((TS, PROJ_COLS), F32),
        pltpu.VMEM((TS, D), BF16),
        pltpu.VMEM((BLOCK, 512), F32),
        pltpu.VMEM((HG_HEADS, HG_D, HG_D), F32),
        pltpu.VMEM((N_EXPERTS, LANES), F32),
        pltpu.VMEM((D, PROJ_COLS), BF16),
        pltpu.VMEM((D, D), BF16),
        pltpu.VMEM((N_EXPERTS, LANES), F32),
        pltpu.VMEM((1, OWN_LANES), F32),
        pltpu.VMEM((1, LANES), F32),
        pltpu.VMEM((2, TS * ROW, LANES), F32),
        pltpu.VMEM((2, TOP_K, TS), jnp.int32),
        pltpu.SMEM((2, TOP_K, TS), jnp.int32),
        pltpu.VMEM((N_EXPERTS, LANES), jnp.int32),
        pltpu.SMEM((N_EXPERTS, LANES), jnp.int32),
        pltpu.SemaphoreType.DMA((2,)),
        pltpu.SemaphoreType.DMA((2,)),
    ]
    return pl.pallas_call(
        _mix_kernel,
        grid=(B, nj),
        in_specs=in_specs,
        out_specs=out_specs,
        out_shape=out_shape,
        scratch_shapes=scratch,
        compiler_params=pltpu.CompilerParams(
            dimension_semantics=("arbitrary", "arbitrary"), vmem_limit_bytes=VMEM_LIMIT),
        name="mix_router",
    )(sinks, x, mod, gmix, win, bin_, bias, lb, hgw, wout, bout, gffn, wr, br, ltri, utri, etri)


TM = 512
DUMP_TILES = 4
OWN_LANES = 384
FF_CHUNKS = 4
TD = 256
COMBINE_TILES = 4
COMBINE_AHEAD = 2
ROW = SUBLANES


def _row_copy(src, src_row, dst, dst_row, sem):
    return pltpu.make_async_copy(src.at[pl.ds(pl.multiple_of(src_row * ROW, ROW), ROW)],
                                 dst.at[pl.ds(pl.multiple_of(dst_row * ROW, ROW), ROW)], sem)


def _col_from_row(w_row, n):
    rr = lax.broadcasted_iota(jnp.int32, (n, n), 0)
    cc = lax.broadcasted_iota(jnp.int32, (n, n), 1)
    return jnp.sum(jnp.where(rr == cc, w_row, 0.0), axis=1, keepdims=True)


def _rows_to_matrix(ref, base, n):
    return jnp.concatenate([ref[pl.ds(base + s, n, stride=ROW), :] for s in range(ROW)], axis=1)


def _moe_kernel(order_ref, te_ref, nt_ref, nxt_ref, par_ref, xs_ref, wgu_hbm, bgu_ref, wd_hbm, bd_ref, perm_ref,
                ys_ref, wgu_st, wd_st, wgu_bf, wd_bf, wsem):
    del order_ref
    j = pl.program_id(0)
    nt = nt_ref[0]

    def weight_copies(e):
        s = par_ref[e]
        return (pltpu.make_async_copy(wgu_hbm.at[e], wgu_st.at[s], wsem.at[0, s]),
                pltpu.make_async_copy(wd_hbm.at[e], wd_st.at[s], wsem.at[1, s]))

    @pl.when(j == 0)
    def _():
        for cp in weight_copies(te_ref[0]):
            cp.start()

    @pl.when(j < nt)
    def _():
        e = te_ref[j]
        first = (j == 0) | (te_ref[jnp.maximum(j - 1, 0)] != e)

        @pl.when(first)
        def _():
            for cp in weight_copies(e):
                cp.wait()
            s = par_ref[e]
            perm = perm_ref[...]
            for g in range(2 * D_FF // 256):
                w = wgu_st[s, :, g * 256:(g + 1) * 256].astype(BF16)
                wgu_bf[:, g * 256:(g + 1) * 256] = _dot(w, perm).astype(BF16)
            wd_bf[...] = wd_st[s].astype(BF16)
            nxt = nxt_ref[e]

            @pl.when(nxt >= 0)
            def _():
                for cp in weight_copies(nxt):
                    cp.start()

        h = _rows_to_matrix(xs_ref, 0, TM).astype(BF16)
        gw = 2 * D_FF // FF_CHUNKS
        hw = D_FF // FF_CHUNKS

        def up_proj(c):
            return _dot(h, wgu_bf[:, c * gw:(c + 1) * gw]) + bgu_ref[0, :, c * gw:(c + 1) * gw]

        def activate(gu):
            parts = []
            for g in range(gw // 256):
                gate = jnp.minimum(gu[:, g * 256:g * 256 + 128], SWIGLU_LIMIT)
                up = jnp.clip(gu[:, g * 256 + 128:(g + 1) * 256], -SWIGLU_LIMIT, SWIGLU_LIMIT)
                parts.append(((up + 1.0) * gate * _sigmoid(SWIGLU_ALPHA * gate)).astype(BF16))
            return jnp.concatenate(parts, axis=1)

        y = bd_ref[0]
        gu_next = up_proj(0)
        for c in range(FF_CHUNKS):
            gu = gu_next
            if c + 1 < FF_CHUNKS:
                gu_next = up_proj(c + 1)
            y = y + _dot(activate(gu), wd_bf[c * hw:(c + 1) * hw, :])
        for s in range(ROW):
            ys_ref[pl.ds(s, TM, stride=ROW), :] = y[:, s * LANES:(s + 1) * LANES]

    @pl.when(j >= nt)
    def _():
        ys_ref[...] = jnp.zeros_like(ys_ref)


def _moe_call(order, te, nt, nxt, par, xs, wgu, bgu, wd, bd, perm):
    n_tiles = xs.shape[0] // (TM * ROW) - DUMP_TILES
    tile = lambda j, order, te, nt, nxt, par: (order[jnp.minimum(j, nt[0] - 1)], 0)
    exp = lambda j, order, te, nt, nxt, par: (te[jnp.minimum(j, nt[0] - 1)], 0, 0)
    grid_spec = pltpu.PrefetchScalarGridSpec(
        num_scalar_prefetch=5,
        grid=(n_tiles,),
        in_specs=[
            pl.BlockSpec((TM * ROW, LANES), tile),
            pl.BlockSpec(memory_space=pl.ANY),
            pl.BlockSpec((1, 1, 2 * D_FF), exp),
            pl.BlockSpec(memory_space=pl.ANY),
            pl.BlockSpec((1, 1, D_MODEL), exp),
            pl.BlockSpec((256, 256), lambda j, order, te, nt, nxt, par: (0, 0)),
        ],
        out_specs=pl.BlockSpec((TM * ROW, LANES), lambda j, order, te, nt, nxt, par: (order[j], 0)),
        scratch_shapes=[
            pltpu.VMEM((2, D_MODEL, 2 * D_FF), F32),
            pltpu.VMEM((2, D_FF, D_MODEL), F32),
            pltpu.VMEM((D_MODEL, 2 * D_FF), BF16),
            pltpu.VMEM((D_FF, D_MODEL), BF16),
            pltpu.SemaphoreType.DMA((2, 2)),
        ],
    )
    return pl.pallas_call(
        _moe_kernel,
        grid_spec=grid_spec,
        out_shape=jax.ShapeDtypeStruct((n_tiles * TM * ROW, LANES), F32),
        compiler_params=pltpu.CompilerParams(
            dimension_semantics=("arbitrary",), vmem_limit_bytes=VMEM_LIMIT),
        name="moe_experts",
    )(order, te, nt, nxt, par, xs, wgu, bgu, wd, bd, perm)


def _final_kernel(*refs):
    n_pos = COMBINE_TILES + COMBINE_AHEAD
    pos_refs = refs[:n_pos]
    wts_ref, x1_ref, mod_ref, gfin_ref, ys_hbm, out_ref = refs[n_pos:n_pos + 6]
    bufs = refs[n_pos + 6:n_pos + 6 + COMBINE_TILES]
    sem = refs[-1]
    i = pl.program_id(0)
    n = pl.num_programs(0)

    def issue(pref, b):
        for q in range(TOP_K * TD):
            _row_copy(ys_hbm, pref[0, 0, q], bufs[b], q, sem.at[b]).start(priority=q % 2)

    def wait(b):
        pltpu.make_async_copy(ys_hbm.at[pl.ds(0, TOP_K * TD * ROW)], bufs[b], sem.at[b]).wait()

    def combine(b, t):
        acc = jnp.zeros((TD, D_MODEL), F32)
        for kq in range(TOP_K):
            v = _rows_to_matrix(bufs[b], kq * TD * ROW, TD)
            acc = acc + _col_from_row(wts_ref[kq:kq + 1, t * TD:(t + 1) * TD], TD) * v
        xf = x1_ref[t * TD:(t + 1) * TD, :] + mod_ref[0][5:6] * acc
        out_ref[t * TD:(t + 1) * TD, :] = _rms(xf) * gfin_ref[...]

    @pl.when(i == 0)
    def _():
        for t in range(COMBINE_AHEAD):
            issue(pos_refs[t], t)

    for t in range(COMBINE_TILES):
        wait(t)
        issue(pos_refs[t + COMBINE_AHEAD], (t + COMBINE_AHEAD) % COMBINE_TILES)
        combine(t, t)

    @pl.when(i == n - 1)
    def _():
        for t in range(COMBINE_AHEAD):
            wait(t)


def _final_call(pos_t, wts, x1, mod, gfin, ys, S):
    N, D = x1.shape
    n_tiles = N // TD
    nstep = n_tiles // COMBINE_TILES
    per_b = S // (COMBINE_TILES * TD)
    n_pos = COMBINE_TILES + COMBINE_AHEAD

    def pos_spec(t):
        return pl.BlockSpec((1, 1, TOP_K * TD),
                            lambda i: (jnp.minimum(COMBINE_TILES * i + t, n_tiles - 1), 0, 0),
                            memory_space=pltpu.SMEM)

    rows = COMBINE_TILES * TD
    return pl.pallas_call(
        _final_kernel,
        grid=(nstep,),
        in_specs=[pos_spec(t) for t in range(n_pos)] + [
            pl.BlockSpec((TOP_K, rows), lambda i: (0, i)),
            pl.BlockSpec((rows, D), lambda i: (i, 0)),
            pl.BlockSpec((1, 8, D), lambda i: (i // per_b, 0, 0)),
            pl.BlockSpec((1, D), lambda i: (0, 0)),
            pl.BlockSpec(memory_space=pl.ANY),
        ],
        out_specs=pl.BlockSpec((rows, D), lambda i: (i, 0)),
        out_shape=jax.ShapeDtypeStruct((N, D), F32),
        scratch_shapes=[pltpu.VMEM((TOP_K * TD * ROW, LANES), F32) for _ in range(COMBINE_TILES)]
        + [pltpu.SemaphoreType.DMA((COMBINE_TILES,))],
        compiler_params=pltpu.CompilerParams(
            dimension_semantics=("arbitrary",), vmem_limit_bytes=VMEM_LIMIT),
        name="combine_norm",
    )(*([pos_t] * n_pos), wts, x1, mod, gfin, ys)


def _route_plan(pos, own, ntiles, N):
    n_tiles = N * TOP_K // TM + N_EXPERTS
    nt = ntiles[0, 0:1]
    owner = own[0, :n_tiles]
    tiles = jnp.arange(n_tiles, dtype=jnp.int32)
    experts = jnp.arange(N_EXPERTS, dtype=jnp.int32)
    used = tiles < nt[0]
    key = jnp.where(used, owner, N_EXPERTS) * OWN_LANES + tiles
    place = jnp.sum(key[None, :] < key[:, None], axis=1)
    at = place[None, :] == tiles[:, None]
    order = jnp.sum(jnp.where(at, tiles[None, :], 0), axis=1).astype(jnp.int32)
    te = jnp.minimum(jnp.sum(jnp.where(at, jnp.where(used, owner, N_EXPERTS)[None, :], 0), axis=1),
                     N_EXPERTS - 1).astype(jnp.int32)
    ntile = jnp.sum((owner[None, :] == experts[:, None]) & used[None, :], axis=1)
    nonempty = ntile > 0
    pos_t = jnp.transpose(pos.reshape(TOP_K, N // TD, TD), (1, 0, 2)).reshape(N // TD, 1, TOP_K * TD)
    par = ((jnp.cumsum(nonempty) - 1) % 2).astype(jnp.int32)
    later = nonempty[None, :] & (experts[None, :] > experts[:, None])
    nxt = jnp.min(jnp.where(later, experts[None, :], N_EXPERTS), axis=1)
    nxt = jnp.where(nxt == N_EXPERTS, -1, nxt).astype(jnp.int32)
    return order, te, nt, nxt, par, pos_t


def _tri_constants():
    r = np.arange(TS)[:, None]
    c = np.arange(TS)[None, :]
    ltri = ((r // CHUNK) == (c // CHUNK)) & (c <= r)
    utri = r < c
    return jnp.asarray(ltri, BF16), jnp.asarray(utri, BF16)


def kernel(x, c, w_ada, b_ada, g_mix, w_in, b_in, attn_sinks, rel_bias, hg_lb, hg_norm_w, w_out, b_out, g_ffn, w_router, b_router, w_gate_up, b_gate_up, w_down, b_down, g_final):
    B, S, D = x.shape
    N = B * S
    mod6 = _prep_call(c, w_ada[0], b_ada)
    mod = jnp.pad(jnp.transpose(mod6, (1, 0, 2)), ((0, 0), (0, 2), (0, 0)))
    lb, bias = _tables_call(rel_bias, hg_lb, _bucket_table())

    wi, bi_ = w_in[0], b_in[0]
    def cols(a):
        aq, ak, av, rest = a[..., :512], a[..., 512:640], a[..., 640:768], a[..., 768:]
        k0, k1, v0, v1 = ak[..., :64], ak[..., 64:], av[..., :64], av[..., 64:]
        return jnp.concatenate([aq, k0, k0, k1, k1, v0, v0, v1, v1, rest], axis=-1)
    win = cols(wi).astype(BF16)
    bin_ = cols(bi_)[None, :]
    wrt = jnp.transpose(w_router[0])
    wr_hi = wrt.astype(BF16)
    wr_lo = (wrt - wr_hi.astype(F32)).astype(BF16)
    wr = jnp.concatenate([wr_hi, wr_lo], axis=0)
    br = jnp.broadcast_to(b_router[0][:, None], (N_EXPERTS, LANES))
    ltri, utri = _tri_constants()
    etri = jnp.asarray(np.tril(np.ones((N_EXPERTS, N_EXPERTS), np.float32), -1), BF16)

    x1, pos, wts, own, ntiles, xs = _mix_call(
        attn_sinks[0], x, mod, g_mix, win, bin_, bias, lb, hg_norm_w, w_out[0].astype(BF16),
        b_out, g_ffn, wr, br, ltri, utri, etri)

    order, te, nt, nxt, par, pos_t = _route_plan(pos, own, ntiles, N)
    ii = np.arange(128)
    perm_np = np.zeros((256, 256), np.float32)
    perm_np[2 * ii, ii] = 1.0
    perm_np[2 * ii + 1, 128 + ii] = 1.0
    perm = jnp.asarray(perm_np, BF16)
    bgu = b_gate_up[0].reshape(N_EXPERTS, D_FF // LANES, LANES, 2)
    bgu = jnp.transpose(bgu, (0, 1, 3, 2)).reshape(N_EXPERTS, 1, 2 * D_FF)
    ys = _moe_call(order, te, nt, nxt, par, xs, w_gate_up[0], bgu, w_down[0], b_down[0][:, None, :], perm)
    out = _final_call(pos_t, wts, x1.reshape(N, D), mod, g_final[None, :], ys, S)
    return out.reshape(B, S, D)
```
